```python
import jax, jax.numpy as jnp
from jax import lax
import numpy as np

D_MODEL = 2048
BATCH = 2
SEQ = 4096
DEPTH = 1
DEC_BATCH = 8
DEC_SEQ = 1
PAST_LEN = 16384
PAGE_SIZE = 128

CHUNK = 128
GM_GROUPS = 8
GM_WIDTH = D_MODEL // 2
GM_GROUP_DIM = GM_WIDTH // GM_GROUPS
N_HEADS = 8
HEAD_DIM = 128
ATT_WIDTH = N_HEADS * HEAD_DIM
IDX_HEADS = 16
IDX_DIM = 128
TOPK_MAX = 256
Q_BLOCK = 128
MEM_TOKENS = 256
MEM_HEADS = 4
MEM_HEAD_DIM = 128
MEM_WIDTH = MEM_HEADS * MEM_HEAD_DIM
N_BRANCH = 3
D_FF = ((8 * D_MODEL // 3 + 255) // 256) * 256
P_IN = 2 * GM_WIDTH + 3 * ATT_WIDTH + IDX_HEADS * IDX_DIM + IDX_DIM + IDX_HEADS + MEM_WIDTH + N_BRANCH * D_MODEL
EPS = 1e-6

kernel_name = "hybrid_gmlp_dsa_memory_decoder_step"


def _rms_norm(x, g):
    xf = x.astype(jnp.float32)
    y = xf * lax.rsqrt(jnp.mean(xf * xf, axis=-1, keepdims=True) + EPS)
    return (y * g.astype(jnp.float32)).astype(x.dtype)


def _layer_norm(x, g, b):
    xf = x.astype(jnp.float32)
    xc = xf - jnp.mean(xf, axis=-1, keepdims=True)
    y = xc * lax.rsqrt(jnp.mean(xc * xc, axis=-1, keepdims=True) + EPS)
    return (y * g.astype(jnp.float32) + b.astype(jnp.float32)).astype(x.dtype)


def _project(h, w_in, ln_v_g, ln_v_b):
    b, t = h.shape[0], h.shape[1]
    sizes = (GM_WIDTH, GM_WIDTH, ATT_WIDTH, ATT_WIDTH, ATT_WIDTH, IDX_HEADS * IDX_DIM, IDX_DIM, IDX_HEADS, MEM_WIDTH)
    cuts = []
    acc = 0
    for s in sizes:
        acc += s
        cuts.append(acc)
    ua, va, q, k, v, qi, kidx, wi, qm, gate_logits = jnp.split(h @ w_in, cuts, axis=-1)
    u = jax.nn.gelu(ua)
    vn = _layer_norm(jax.nn.gelu(va), ln_v_g, ln_v_b)
    q = q.reshape(b, t, N_HEADS, HEAD_DIM)
    k = k.reshape(b, t, N_HEADS, HEAD_DIM)
    v = v.reshape(b, t, N_HEADS, HEAD_DIM)
    qi = qi.reshape(b, t, IDX_HEADS, IDX_DIM)
    wi = wi * (IDX_HEADS ** -0.5 * IDX_DIM ** -0.5)
    qm = qm.reshape(b, t, MEM_HEADS, MEM_HEAD_DIM)
    return u, vn, q, k, v, qi, kidx, wi, qm, gate_logits


def _gmlp_spatial(u, vn, w_s, b_s):
    b, t = u.shape[0], u.shape[1]
    n = min(t, CHUNK)
    nc = t // n
    causal = jnp.tril(jnp.ones((n, n), dtype=bool))
    w = jnp.where(causal[None], w_s[:, :n, :n], 0).astype(vn.dtype)
    v5 = vn.reshape(b, nc, n, GM_GROUPS, GM_GROUP_DIM)
    mixed = jnp.einsum('gts,bcsgd->bctgd', w, v5) + b_s[:, :n].T[None, None, :, :, None]
    return u * mixed.reshape(b, t, GM_WIDTH)


def _indexer_scores(qi, wi, kidx):
    s = jnp.einsum('bqhd,bld->bqhl', qi, kidx).astype(jnp.float32)
    return jnp.einsum('bqhl,bqh->bql', jax.nn.relu(s), wi.astype(jnp.float32))


def _attend_selected(q, k_sel, v_sel, valid):
    s = jnp.einsum('bqhd,bqkhd->bqhk', q, k_sel).astype(jnp.float32) * HEAD_DIM ** -0.5
    s = jnp.where(valid[:, :, None, :], s, -jnp.inf)
    p = jax.nn.softmax(s, axis=-1).astype(v_sel.dtype)
    return jnp.einsum('bqhk,bqkhd->bqhd', p, v_sel)


def _gather_rows(a, idx):
    return jax.vmap(lambda rows, i: rows[i])(a, idx)


def _dsa_prompt(q, k, v, qi, kidx, wi):
    b, t = q.shape[0], q.shape[1]
    nb = t // Q_BLOCK
    topk = min(TOPK_MAX, t // 4)
    key_pos = jnp.arange(t)

    def block(args):
        q_b, qi_b, wi_b, qpos = args
        sc = _indexer_scores(qi_b, wi_b, kidx)
        sc = jnp.where((key_pos[None, :] <= qpos[:, None])[None], sc, -jnp.inf)
        _, idx = lax.top_k(sc, topk)
        valid = idx <= qpos[None, :, None]
        return _attend_selected(q_b, _gather_rows(k, idx), _gather_rows(v, idx), valid)

    def to_blocks(a):
        return jnp.swapaxes(a.reshape((b, nb, Q_BLOCK) + a.shape[2:]), 0, 1)

    out = lax.map(block, (to_blocks(q), to_blocks(qi), to_blocks(wi), jnp.arange(t).reshape(nb, Q_BLOCK)))
    return jnp.swapaxes(out, 0, 1).reshape(b, t, ATT_WIDTH)


def _dsa_sample(q, k_new, v_new, qi, kidx_new, wi, cache_k, cache_v, cache_kidx, page_table):
    db, t = q.shape[0], q.shape[1]
    past = page_table.shape[1] * PAGE_SIZE
    n_keys = past + t
    kidx_past = cache_kidx[page_table].reshape(db, past, IDX_DIM)
    kidx_all = jnp.concatenate([kidx_past, kidx_new.astype(kidx_past.dtype)], axis=1)
    sc = _indexer_scores(qi, wi, kidx_all)
    qpos = past + jnp.arange(t)
    key_pos = jnp.arange(n_keys)
    sc = jnp.where((key_pos[None, :] <= qpos[:, None])[None], sc, -jnp.inf)
    topk = min(TOPK_MAX, n_keys // 4)
    _, idx = lax.top_k(sc, topk)
    valid = idx <= qpos[None, :, None]
    is_new = (idx >= past)[..., None, None]
    idx_past = jnp.minimum(idx, past - 1)
    phys = jax.vmap(lambda pt, i: pt[i // PAGE_SIZE])(page_table, idx_past)
    off = idx_past % PAGE_SIZE
    idx_new = jnp.clip(idx - past, 0, t - 1)
    k_sel = jnp.where(is_new, _gather_rows(k_new, idx_new).astype(cache_k.dtype), cache_k[phys, off])
    v_sel = jnp.where(is_new, _gather_rows(v_new, idx_new).astype(cache_v.dtype), cache_v[phys, off])
    return _attend_selected(q, k_sel, v_sel, valid).reshape(db, t, ATT_WIDTH)


def _mem_kv(mem, g_mem, w_mem_kv):
    b, m = mem.shape[0], mem.shape[1]
    mk, mv = jnp.split(_rms_norm(mem, g_mem) @ w_mem_kv, 2, axis=-1)
    return mk.reshape(b, m, MEM_HEADS, MEM_HEAD_DIM), mv.reshape(b, m, MEM_HEADS, MEM_HEAD_DIM)


def _mem_attend(qm, mk, mv):
    b, t = qm.shape[0], qm.shape[1]
    s = jnp.einsum('bthd,bmhd->bhtm', qm, mk).astype(jnp.float32) * MEM_HEAD_DIM ** -0.5
    p = jax.nn.softmax(s, axis=-1).astype(mv.dtype)
    return jnp.einsum('bhtm,bmhd->bthd', p, mv).reshape(b, t, MEM_WIDTH)


def _finish(x, gate_logits, y_a, y_b, y_m, g_post_mix, w_oa, w_ob, w_om, w_out, g_pre_ffn, g_post_ffn, w_gu, w_down):
    b, t = x.shape[0], x.shape[1]
    g = jax.nn.sigmoid(gate_logits.astype(jnp.float32)).astype(x.dtype).reshape(b, t, N_BRANCH, D_MODEL)
    merged = g[:, :, 0] * (y_a @ w_oa) + g[:, :, 1] * (y_b @ w_ob) + g[:, :, 2] * (y_m @ w_om)
    x = x + _rms_norm(merged @ w_out, g_post_mix)
    a, c = jnp.split(_rms_norm(x, g_pre_ffn) @ w_gu, 2, axis=-1)
    return x + _rms_norm((jax.nn.silu(a) * c) @ w_down, g_post_ffn)


def setup_inputs(seed: int = 0) -> dict:
    key = jax.random.key(seed)
    ks = jax.random.split(key, 32)
    n_pages = PAST_LEN // PAGE_SIZE
    n_pool = (5 * DEC_BATCH * n_pages + 3) // 4
    f32 = jnp.float32

    def nrm(k, shape, scale=1.0):
        return jax.random.normal(k, shape, f32) * scale

    def gain(k, shape):
        return 1.0 + 0.01 * jax.random.normal(k, shape, f32)

    page_table = jax.random.permutation(ks[8], n_pool)[: DEC_BATCH * n_pages].reshape(DEC_BATCH, n_pages).astype(jnp.int32)
    return {
        "x_prompt": nrm(ks[0], (BATCH, SEQ, D_MODEL)),
        "x_sample": nrm(ks[1], (DEC_BATCH, DEC_SEQ, D_MODEL)),
        "mem_prompt": nrm(ks[2], (BATCH, MEM_TOKENS, D_MODEL)),
        "cache_k": nrm(ks[3], (DEPTH, n_pool, PAGE_SIZE, N_HEADS, HEAD_DIM)),
        "cache_v": nrm(ks[4], (DEPTH, n_pool, PAGE_SIZE, N_HEADS, HEAD_DIM)),
        "cache_kidx": nrm(ks[5], (DEPTH, n_pool, PAGE_SIZE, IDX_DIM)),
        "cache_mem_k": nrm(ks[6], (DEPTH, DEC_BATCH, MEM_TOKENS, MEM_HEADS, MEM_HEAD_DIM)),
        "cache_mem_v": nrm(ks[7], (DEPTH, DEC_BATCH, MEM_TOKENS, MEM_HEADS, MEM_HEAD_DIM)),
        "page_table": page_table,
        "g_pre_mix": gain(ks[9], (DEPTH, D_MODEL)),
        "g_post_mix": gain(ks[10], (DEPTH, D_MODEL)),
        "g_pre_ffn": gain(ks[11], (DEPTH, D_MODEL)),
        "g_post_ffn": gain(ks[12], (DEPTH, D_MODEL)),
        "g_mem": gain(ks[13], (DEPTH, D_MODEL)),
        "ln_v_g": gain(ks[14], (DEPTH, GM_WIDTH)),
        "ln_v_b": nrm(ks[15], (DEPTH, GM_WIDTH), 0.01),
        "w_in": nrm(ks[16], (DEPTH, D_MODEL, P_IN), D_MODEL ** -0.5),
        "w_s": nrm(ks[17], (DEPTH, GM_GROUPS, CHUNK, CHUNK), CHUNK ** -0.5),
        "b_s": gain(ks[18], (DEPTH, GM_GROUPS, CHUNK)),
        "w_oa": nrm(ks[19], (DEPTH, GM_WIDTH, D_MODEL), GM_WIDTH ** -0.5),
        "w_ob": nrm(ks[20], (DEPTH, ATT_WIDTH, D_MODEL), ATT_WIDTH ** -0.5),
        "w_om": nrm(ks[21], (DEPTH, MEM_WIDTH, D_MODEL), MEM_WIDTH ** -0.5),
        "w_out": nrm(ks[22], (DEPTH, D_MODEL, D_MODEL), D_MODEL ** -0.5),
        "w_mem_kv": nrm(ks[23], (DEPTH, D_MODEL, 2 * MEM_WIDTH), D_MODEL ** -0.5),
        "w_gu": nrm(ks[24], (DEPTH, D_MODEL, 2 * D_FF), D_MODEL ** -0.5),
        "w_down": nrm(ks[25], (DEPTH, D_FF, D_MODEL), D_FF ** -0.5),
    }


def reference(x_prompt, x_sample, mem_prompt, cache_k, cache_v, cache_kidx, cache_mem_k, cache_mem_v, page_table,
              g_pre_mix, g_post_mix, g_pre_ffn, g_post_ffn, g_mem, ln_v_g, ln_v_b, w_in, w_s, b_s,
              w_oa, w_ob, w_om, w_out, w_mem_kv, w_gu, w_down):
    yp, ys = x_prompt, x_sample
    kp_l, vp_l, kip_l, mkp_l, mvp_l = [], [], [], [], []
    ks_l, vs_l, kis_l, cvs_l = [], [], [], []
    for l in range(DEPTH):
        tail = (g_post_mix[l], w_oa[l], w_ob[l], w_om[l], w_out[l], g_pre_ffn[l], g_post_ffn[l], w_gu[l], w_down[l])
        h = _rms_norm(yp, g_pre_mix[l])
        u, vn, q, k, v, qi, kidx, wi, qm, gl = _project(h, w_in[l], ln_v_g[l], ln_v_b[l])
        y_a = _gmlp_spatial(u, vn, w_s[l], b_s[l])
        y_b = _dsa_prompt(q, k, v, qi, kidx, wi)
        mk, mv = _mem_kv(mem_prompt, g_mem[l], w_mem_kv[l])
        y_m = _mem_attend(qm, mk, mv)
        yp = _finish(yp, gl, y_a, y_b, y_m, *tail)
        kp_l.append(k); vp_l.append(v); kip_l.append(kidx); mkp_l.append(mk); mvp_l.append(mv)
        h = _rms_norm(ys, g_pre_mix[l])
        u, vn, q, k, v, qi, kidx, wi, qm, gl = _project(h, w_in[l], ln_v_g[l], ln_v_b[l])
        y_a = _gmlp_spatial(u, vn, w_s[l], b_s[l])
        y_b = _dsa_sample(q, k, v, qi, kidx, wi, cache_k[l], cache_v[l], cache_kidx[l], page_table)
        y_m = _mem_attend(qm, cache_mem_k[l], cache_mem_v[l])
        ys = _finish(ys, gl, y_a, y_b, y_m, *tail)
        ks_l.append(k); vs_l.append(v); kis_l.append(kidx); cvs_l.append(vn)
    return (yp, ys, jnp.stack(kp_l), jnp.stack(vp_l), jnp.stack(kip_l), jnp.stack(mkp_l), jnp.stack(mvp_l),
            jnp.stack(ks_l), jnp.stack(vs_l), jnp.stack(kis_l), jnp.stack(cvs_l))
```

```python
import functools

import jax
import jax.numpy as jnp
from jax import lax
from jax.experimental import pallas as pl
from jax.experimental.pallas import tpu as pltpu

EPS = 1e-6
CHUNK = 128
GM_GROUPS = 8
N_HEADS = 8
HEAD_DIM = 128
IDX_HEADS = 16
IDX_DIM = 128
TOPK_MAX = 256
MEM_HEADS = 4
MEM_HEAD_DIM = 128
N_BRANCH = 3
PAGE_SIZE = 128

LANES = 128
SUBLANES = 8
VMEM_LIMIT_BYTES = 56 * 2**20
INT_MIN = -2**31
NEG = -1e30

F32 = jnp.float32
BF16 = jnp.bfloat16
NT_DIMS = (((1,), (1,)), ((), ()))


def _params(*sem):
    return pltpu.CompilerParams(dimension_semantics=sem, vmem_limit_bytes=VMEM_LIMIT_BYTES)


def _tile(n, pref):
    if n <= pref:
        return n
    t = pref
    while n % t:
        t //= 2
    return t


def _sortable(x):
    bits = pltpu.bitcast(x, jnp.int32)
    return bits ^ ((bits >> 31) & 0x7FFFFFFF)


def _kth_largest_key(count_ge, topk, like):
    base = jnp.where(count_ge(jnp.zeros_like(like)) >= topk, 0, INT_MIN).astype(jnp.int32)

    def bit_body(b, t):
        cand = t | jnp.left_shift(jnp.int32(1), 30 - b)
        return jnp.where(count_ge(cand) >= topk, cand, t)

    return lax.fori_loop(0, 31, bit_body, base)


def _rmsnorm_rows(x, g):
    return x * lax.rsqrt(jnp.mean(x * x, axis=-1, keepdims=True) + EPS) * g


def _rmsnorm_kernel(x_ref, g_ref, o_ref):
    o_ref[...] = _rmsnorm_rows(x_ref[...], g_ref[...]).astype(o_ref.dtype)


def _rmsnorm(x, g, out_dtype):
    m, d = x.shape
    tm = _tile(m, 512)
    return pl.pallas_call(
        _rmsnorm_kernel,
        grid=(m // tm,),
        in_specs=[pl.BlockSpec((tm, d), lambda i: (i, 0)), pl.BlockSpec((1, d), lambda i: (0, 0))],
        out_specs=pl.BlockSpec((tm, d), lambda i: (i, 0)),
        out_shape=jax.ShapeDtypeStruct((m, d), out_dtype),
        compiler_params=_params("parallel"),
        name="rmsnorm",
    )(x, g.reshape(1, d))


def _mm_kernel(x_ref, w_ref, *o_refs, act):
    acc = jnp.dot(x_ref[...], w_ref[...], preferred_element_type=F32)
    if act == "sigmoid":
        acc = jax.nn.sigmoid(acc)
    for o_ref in o_refs:
        o_ref[...] = acc.astype(o_ref.dtype)


def _matmul(x, w, out_dtypes, act=None, name="matmul"):
    m, k = x.shape
    n = w.shape[1]
    tm = _tile(m, 1024)
    tn = _tile(n, 512)
    outs = pl.pallas_call(
        functools.partial(_mm_kernel, act=act),
        grid=(m // tm, n // tn),
        in_specs=[pl.BlockSpec((tm, k), lambda i, j: (i, 0)), pl.BlockSpec((k, tn), lambda i, j: (0, j))],
        out_specs=[pl.BlockSpec((tm, tn), lambda i, j: (i, j)) for _ in out_dtypes],
        out_shape=[jax.ShapeDtypeStruct((m, n), dt) for dt in out_dtypes],
        compiler_params=_params("parallel", "parallel"),
        name=name,
    )(x, w)
    return outs


def _uv_kernel(x_ref, w_ref, lng_ref, lnb_ref, o_ref):
    y = jax.nn.gelu(jnp.dot(x_ref[...], w_ref[...], preferred_element_type=F32))

    @pl.when(pl.program_id(1) == 0)
    def _():
        o_ref[...] = y.astype(o_ref.dtype)

    @pl.when(pl.program_id(1) == 1)
    def _():
        yc = y - jnp.mean(y, axis=-1, keepdims=True)
        yn = yc * lax.rsqrt(jnp.mean(yc * yc, axis=-1, keepdims=True) + EPS)
        o_ref[...] = (yn * lng_ref[...] + lnb_ref[...]).astype(o_ref.dtype)


def _project_uv(h, w_uv, ln_g, ln_b, out_dtype):
    m, k = h.shape
    gw = w_uv.shape[1] // 2
    tm = _tile(m, 512)
    return pl.pallas_call(
        _uv_kernel,
        grid=(m // tm, 2),
        in_specs=[
            pl.BlockSpec((tm, k), lambda i, j: (i, 0)),
            pl.BlockSpec((k, gw), lambda i, j: (0, j)),
            pl.BlockSpec((1, gw), lambda i, j: (0, 0)),
            pl.BlockSpec((1, gw), lambda i, j: (0, 0)),
        ],
        out_specs=pl.BlockSpec((tm, gw), lambda i, j: (i, j)),
        out_shape=jax.ShapeDtypeStruct((m, 2 * gw), out_dtype),
        compiler_params=_params("parallel", "arbitrary"),
        name="project_uv",
    )(h, w_uv, ln_g.reshape(1, gw), ln_b.reshape(1, gw))


def _gmlp_kernel(u_ref, v_ref, ws_ref, bt_ref, o_ref, *, n_chunks):
    c = CHUNK
    gd = u_ref.shape[1] // GM_GROUPS
    causal = lax.broadcasted_iota(jnp.int32, (c, c), 1) <= lax.broadcasted_iota(jnp.int32, (c, c), 0)
    for g in range(GM_GROUPS):
        w = jnp.where(causal, ws_ref[g], 0.0).astype(v_ref.dtype)
        bcol = bt_ref[:, g:g + 1]
        for ci in range(n_chunks):
            rows = slice(ci * c, (ci + 1) * c)
            cols = slice(g * gd, (g + 1) * gd)
            mixed = jnp.dot(w, v_ref[rows, cols], preferred_element_type=F32) + bcol
            o_ref[rows, cols] = (u_ref[rows, cols].astype(F32) * mixed).astype(o_ref.dtype)


def _gmlp_spatial(uv, w_s, b_s):
    m = uv.shape[0]
    gw = uv.shape[1] // 2
    tm = _tile(m, 4 * CHUNK)
    return pl.pallas_call(
        functools.partial(_gmlp_kernel, n_chunks=tm // CHUNK),
        grid=(m // tm,),
        in_specs=[
            pl.BlockSpec((tm, gw), lambda i: (i, 0)),
            pl.BlockSpec((tm, gw), lambda i: (i, 1)),
            pl.BlockSpec((GM_GROUPS, CHUNK, CHUNK), lambda i: (0, 0, 0)),
            pl.BlockSpec((CHUNK, GM_GROUPS), lambda i: (0, 0)),
        ],
        out_specs=pl.BlockSpec((tm, gw), lambda i: (i, 0)),
        out_shape=jax.ShapeDtypeStruct((m, gw), BF16),
        compiler_params=_params("parallel"),
        name="gmlp_spatial",
    )(uv, uv, w_s, b_s.T)


def _gmlp_first_kernel(u_ref, v_ref, w_ref, b_ref, o_ref):
    o_ref[...] = (u_ref[...] * (w_ref[...] * v_ref[...] + b_ref[...])).astype(o_ref.dtype)


def _gmlp_first_row(uv, w_s, b_s):
    m = uv.shape[0]
    gw = uv.shape[1] // 2
    gd = gw // GM_GROUPS
    w_row = jnp.repeat(w_s[:, 0, 0], gd).reshape(1, gw)
    b_row = jnp.repeat(b_s[:, 0], gd).reshape(1, gw)
    return pl.pallas_call(
        _gmlp_first_kernel,
        grid=(1,),
        in_specs=[
            pl.BlockSpec((m, gw), lambda i: (0, 0)),
            pl.BlockSpec((m, gw), lambda i: (0, 1)),
            pl.BlockSpec((1, gw), lambda i: (0, 0)),
            pl.BlockSpec((1, gw), lambda i: (0, 0)),
        ],
        out_specs=pl.BlockSpec((m, gw), lambda i: (0, 0)),
        out_shape=jax.ShapeDtypeStruct((m, gw), BF16),
        compiler_params=_params("arbitrary"),
        name="gmlp_first_row",
    )(uv, uv, w_row, b_row)


def _dsa_kernel(q_ref, qi_ref, wit_ref, k_ref, kidx_ref, vt_ref, o_ref,
                keys_ref, m_ref, l_ref, acc_ref, *, tq, tk, topk, wscale):
    i = pl.program_id(1)
    n_chunks = ((i + 1) * tq + tk - 1) // tk
    qpos = i * tq + lax.broadcasted_iota(jnp.int32, (1, tq), 1)

    def idx_body(c, carry):
        k0 = pl.multiple_of(c * tk, tk)
        kc = kidx_ref[0, pl.ds(k0, tk), :]
        acc = jnp.zeros((tk, tq), F32)
        for h in range(IDX_HEADS):
            s = lax.dot_general(kc, qi_ref[0, :, h * IDX_DIM:(h + 1) * IDX_DIM], NT_DIMS,
                                preferred_element_type=F32)
            acc = acc + jnp.maximum(s, 0.0) * (wit_ref[0, h:h + 1, :] * wscale)
        kpos = k0 + lax.broadcasted_iota(jnp.int32, (tk, 1), 0)
        keys_ref[pl.ds(k0, tk), :] = jnp.where(kpos <= qpos, _sortable(acc), INT_MIN)
        return carry

    lax.fori_loop(0, n_chunks, idx_body, 0)

    def count_ge(cand):
        def body(c, cnt):
            k0 = pl.multiple_of(c * tk, tk)
            hit = jnp.where(keys_ref[pl.ds(k0, tk), :] >= cand, 1.0, 0.0)
            return cnt + jnp.sum(hit.reshape(tk // SUBLANES, SUBLANES, tq), axis=0)

        cnt = lax.fori_loop(0, n_chunks, body, jnp.zeros((SUBLANES, tq), F32))
        return jnp.sum(cnt, axis=0, keepdims=True)

    thr = _kth_largest_key(count_ge, topk, qpos)
    thr = jnp.maximum(thr, INT_MIN + 1)

    m_ref[...] = jnp.full(m_ref.shape, NEG, F32)
    l_ref[...] = jnp.zeros(l_ref.shape, F32)
    acc_ref[...] = jnp.zeros(acc_ref.shape, F32)
    scale = HEAD_DIM ** -0.5

    def att_body(c, carry):
        k0 = pl.multiple_of(c * tk, tk)
        sel = keys_ref[pl.ds(k0, tk), :] >= thr
        for h in range(N_HEADS):
            hs = slice(h * HEAD_DIM, (h + 1) * HEAD_DIM)
            s = lax.dot_general(k_ref[0, pl.ds(k0, tk), hs], q_ref[0, :, hs], NT_DIMS,
                                preferred_element_type=F32) * scale
            m_old = m_ref[h:h + 1, :]
            m_new = jnp.maximum(m_old, jnp.max(jnp.where(sel, s, NEG), axis=0, keepdims=True))
            p = jnp.where(sel, jnp.exp(s - m_new), 0.0)
            alpha = jnp.exp(m_old - m_new)
            l_ref[h:h + 1, :] = alpha * l_ref[h:h + 1, :] + jnp.sum(p, axis=0, keepdims=True)
            m_ref[h:h + 1, :] = m_new
            pv = jnp.dot(vt_ref[0, hs, pl.ds(k0, tk)], p.astype(BF16), preferred_element_type=F32)
            acc_ref[hs, :] = alpha * acc_ref[hs, :] + pv
        return carry

    lax.fori_loop(0, n_chunks, att_body, 0)

    for h in range(N_HEADS):
        hs = slice(h * HEAD_DIM, (h + 1) * HEAD_DIM)
        o_ref[0, :, hs] = (acc_ref[hs, :] / l_ref[h:h + 1, :]).T.astype(o_ref.dtype)


def _dsa_prompt(q, qi, wi_t, k, kidx, v_t, tq=256, tk=256):
    b, t, aw = q.shape
    tq = _tile(t, tq)
    tk = _tile(tq, tk)
    topk = min(TOPK_MAX, t // 4)
    once = pl.Buffered(1)
    return pl.pallas_call(
        functools.partial(_dsa_kernel, tq=tq, tk=tk, topk=topk,
                          wscale=IDX_HEADS ** -0.5 * IDX_DIM ** -0.5),
        grid=(b, t // tq),
        in_specs=[
            pl.BlockSpec((1, tq, aw), lambda bi, i: (bi, i, 0)),
            pl.BlockSpec((1, tq, IDX_HEADS * IDX_DIM), lambda bi, i: (bi, i, 0)),
            pl.BlockSpec((1, IDX_HEADS, tq), lambda bi, i: (bi, 0, i)),
            pl.BlockSpec((1, t, aw), lambda bi, i: (bi, 0, 0), pipeline_mode=once),
            pl.BlockSpec((1, t, IDX_DIM), lambda bi, i: (bi, 0, 0), pipeline_mode=once),
            pl.BlockSpec((1, aw, t), lambda bi, i: (bi, 0, 0), pipeline_mode=once),
        ],
        out_specs=pl.BlockSpec((1, tq, aw), lambda bi, i: (bi, i, 0)),
        out_shape=jax.ShapeDtypeStruct((b, t, aw), BF16),
        scratch_shapes=[
            pltpu.VMEM((t, tq), jnp.int32),
            pltpu.VMEM((N_HEADS, tq), F32),
            pltpu.VMEM((N_HEADS, tq), F32),
            pltpu.VMEM((aw, tq), F32),
        ],
        compiler_params=_params("parallel", "arbitrary"),
        name="dsa_prompt",
    )(q, qi, wi_t, k, kidx, v_t)


def _memattn_kernel(qm_ref, mk_ref, mv_ref, o_ref):
    scale = MEM_HEAD_DIM ** -0.5
    for h in range(MEM_HEADS):
        hs = slice(h * MEM_HEAD_DIM, (h + 1) * MEM_HEAD_DIM)
        s = lax.dot_general(qm_ref[0, :, hs], mk_ref[0, :, hs].astype(BF16), NT_DIMS,
                            preferred_element_type=F32) * scale
        e = jnp.exp(s - jnp.max(s, axis=-1, keepdims=True))
        p = e / jnp.sum(e, axis=-1, keepdims=True)
        o = jnp.dot(p.astype(BF16), mv_ref[0, :, hs].astype(BF16), preferred_element_type=F32)
        o_ref[0, :, hs] = o.astype(o_ref.dtype)


def _mem_attend(qm, mk, mv):
    b, t, mw = qm.shape
    n_mem = mk.shape[1]
    tm = _tile(t, 512)
    return pl.pallas_call(
        _memattn_kernel,
        grid=(b, t // tm),
        in_specs=[
            pl.BlockSpec((1, tm, mw), lambda bi, i: (bi, i, 0)),
            pl.BlockSpec((1, n_mem, mw), lambda bi, i: (bi, 0, 0)),
            pl.BlockSpec((1, n_mem, mw), lambda bi, i: (bi, 0, 0)),
        ],
        out_specs=pl.BlockSpec((1, tm, mw), lambda bi, i: (bi, i, 0)),
        out_shape=jax.ShapeDtypeStruct((b, t, mw), BF16),
        compiler_params=_params("parallel", "parallel"),
        name="mem_attend",
    )(qm, mk, mv)


def _sscore_kernel(pt_ref, qi_ref, wi_ref, kidx_ref, knew_ref, o_ref, *, wscale):
    p = pl.program_id(1)
    is_new = p == pl.num_programs(1) - 1
    page = kidx_ref[0, 0]
    kc = jnp.where(is_new, jnp.broadcast_to(knew_ref[0], page.shape), page).astype(BF16)
    s = lax.dot_general(qi_ref[0], kc, NT_DIMS, preferred_element_type=F32)
    r = jnp.maximum(s, 0.0) * (wi_ref[0] * wscale)
    o_ref[0, 0] = jnp.sum(r, axis=0, keepdims=True)


def _sample_scores(qi, wi, kidx_new, cache_kidx, page_table):
    db, n_pages = page_table.shape
    grid_spec = pltpu.PrefetchScalarGridSpec(
        num_scalar_prefetch=1,
        grid=(db, n_pages + 1),
        in_specs=[
            pl.BlockSpec((1, IDX_HEADS, IDX_DIM), lambda b, p, pt: (b, 0, 0)),
            pl.BlockSpec((1, IDX_HEADS, 1), lambda b, p, pt: (b, 0, 0)),
            pl.BlockSpec((1, 1, PAGE_SIZE, IDX_DIM),
                         lambda b, p, pt: (0, pt[b, jnp.minimum(p, pt.shape[1] - 1)], 0, 0)),
            pl.BlockSpec((1, 1, IDX_DIM), lambda b, p, pt: (b, 0, 0)),
        ],
        out_specs=pl.BlockSpec((1, 1, 1, PAGE_SIZE), lambda b, p, pt: (b, p, 0, 0)),
    )
    out = pl.pallas_call(
        functools.partial(_sscore_kernel, wscale=IDX_HEADS ** -0.5 * IDX_DIM ** -0.5),
        grid_spec=grid_spec,
        out_shape=jax.ShapeDtypeStruct((db, n_pages + 1, 1, PAGE_SIZE), F32),
        compiler_params=_params("parallel", "arbitrary"),
        name="sample_scores",
    )(page_table, qi.reshape(db, IDX_HEADS, IDX_DIM), wi.reshape(db, IDX_HEADS, 1),
      cache_kidx.reshape(1, -1, PAGE_SIZE, IDX_DIM), kidx_new.reshape(db, 1, IDX_DIM))
    return out.reshape(db, n_pages + 1, PAGE_SIZE)


def _sselect_kernel(s_ref, mask_ref, *, past, topk):
    s = s_ref[0]
    pos = (lax.broadcasted_iota(jnp.int32, s.shape, 0) * PAGE_SIZE
           + lax.broadcasted_iota(jnp.int32, s.shape, 1))
    key = jnp.where(pos <= past, _sortable(s), INT_MIN)

    def count_ge(cand):
        hit = jnp.where(key >= cand, 1.0, 0.0)
        return jnp.sum(jnp.sum(hit, axis=1, keepdims=True), axis=0, keepdims=True)

    thr = _kth_largest_key(count_ge, topk, jnp.zeros((1, 1), jnp.int32))
    thr = jnp.maximum(thr, INT_MIN + 1)
    mask_ref[0] = jnp.where(key >= thr, 1.0, 0.0)


def _sample_select(scores, past):
    db, rows, _ = scores.shape
    topk = min(TOPK_MAX, (past + 1) // 4)
    return pl.pallas_call(
        functools.partial(_sselect_kernel, past=past, topk=topk),
        grid=(db,),
        in_specs=[pl.BlockSpec((1, rows, PAGE_SIZE), lambda b: (b, 0, 0))],
        out_specs=pl.BlockSpec((1, rows, PAGE_SIZE), lambda b: (b, 0, 0)),
        out_shape=jax.ShapeDtypeStruct(scores.shape, F32),
        compiler_params=_params("parallel"),
        name="sample_select",
    )(scores)


def _sattn_kernel(pt_ref, q_ref, k_ref, v_ref, mask_ref, mnew_ref, knew_ref, vnew_ref, o_ref,
                  m_ref, l_ref, acc_ref):
    p = pl.program_id(1)
    aw = q_ref.shape[2]
    scale = HEAD_DIM ** -0.5
    own = (lax.broadcasted_iota(jnp.int32, (N_HEADS, aw), 1) // HEAD_DIM
           == lax.broadcasted_iota(jnp.int32, (N_HEADS, aw), 0))
    qbd = jnp.where(own, q_ref[0].astype(F32), 0.0).astype(BF16)

    @pl.when(p == 0)
    def _():
        m_ref[...] = jnp.full(m_ref.shape, NEG, F32)
        l_ref[...] = jnp.zeros(l_ref.shape, F32)
        acc_ref[...] = jnp.zeros(acc_ref.shape, F32)

    def update(s, sel, pv_fn):
        m_old = m_ref[...]
        m_new = jnp.maximum(m_old, jnp.max(jnp.where(sel, s, NEG), axis=1, keepdims=True))
        pr = jnp.where(sel, jnp.exp(s - m_new), 0.0)
        alpha = jnp.exp(m_old - m_new)
        l_ref[...] = alpha * l_ref[...] + jnp.sum(pr, axis=1, keepdims=True)
        acc_ref[...] = alpha * acc_ref[...] + pv_fn(pr.astype(BF16))
        m_ref[...] = m_new

    s = lax.dot_general(qbd, k_ref[0, 0].astype(BF16), NT_DIMS, preferred_element_type=F32) * scale
    update(s, mask_ref[0, 0] > 0.5,
           lambda pr: jnp.dot(pr, v_ref[0, 0].astype(BF16), preferred_element_type=F32))

    @pl.when(p == pl.num_programs(1) - 1)
    def _():
        kn = knew_ref[0].astype(BF16).astype(F32)
        vn = vnew_ref[0].astype(BF16).astype(F32)
        s_new = jnp.sum(qbd.astype(F32) * kn, axis=1, keepdims=True) * scale
        update(s_new, mnew_ref[0, 0][:, 0:1] > 0.5, lambda pr: pr.astype(F32) * vn)
        o = jnp.where(own, acc_ref[...] / l_ref[...], 0.0)
        o_ref[0] = jnp.sum(o, axis=0, keepdims=True).astype(o_ref.dtype)


def _sample_attend(q, k_new, v_new, mask, cache_k, cache_v, page_table):
    db, n_pages = page_table.shape
    aw = q.shape[1]
    mask4 = mask.reshape(db, n_pages + 1, 1, PAGE_SIZE)
    grid_spec = pltpu.PrefetchScalarGridSpec(
        num_scalar_prefetch=1,
        grid=(db, n_pages),
        in_specs=[
            pl.BlockSpec((1, 1, aw), lambda b, p, pt: (b, 0, 0)),
            pl.BlockSpec((1, 1, PAGE_SIZE, aw), lambda b, p, pt: (0, pt[b, p], 0, 0)),
            pl.BlockSpec((1, 1, PAGE_SIZE, aw), lambda b, p, pt: (0, pt[b, p], 0, 0)),
            pl.BlockSpec((1, 1, 1, PAGE_SIZE), lambda b, p, pt: (b, p, 0, 0)),
            pl.BlockSpec((1, 1, 1, PAGE_SIZE), lambda b, p, pt: (b, pt.shape[1], 0, 0)),
            pl.BlockSpec((1, 1, aw), lambda b, p, pt: (b, 0, 0)),
            pl.BlockSpec((1, 1, aw), lambda b, p, pt: (b, 0, 0)),
        ],
        out_specs=pl.BlockSpec((1, 1, aw), lambda b, p, pt: (b, 0, 0)),
        scratch_shapes=[
            pltpu.VMEM((N_HEADS, 1), F32),
            pltpu.VMEM((N_HEADS, 1), F32),
            pltpu.VMEM((N_HEADS, aw), F32),
        ],
    )
    out = pl.pallas_call(
        _sattn_kernel,
        grid_spec=grid_spec,
        out_shape=jax.ShapeDtypeStruct((db, 1, aw), BF16),
        compiler_params=_params("parallel", "arbitrary"),
        name="sample_attend",
    )(page_table, q.reshape(db, 1, aw), cache_k.reshape(1, -1, PAGE_SIZE, aw),
      cache_v.reshape(1, -1, PAGE_SIZE, aw), mask4, mask4,
      k_new.reshape(db, 1, aw), v_new.reshape(db, 1, aw))
    return out.reshape(db, aw)


def _merge_kernel(x_ref, g_ref, ya_ref, yb_ref, ym_ref, woa_ref, wob_ref, wom_ref, wout_ref,
                  gpm_ref, gpf_ref, x1_ref, h2_ref):
    d = x_ref.shape[1]
    merged = (g_ref[:, 0:d].astype(F32) * jnp.dot(ya_ref[...], woa_ref[...], preferred_element_type=F32)
              + g_ref[:, d:2 * d].astype(F32) * jnp.dot(yb_ref[...], wob_ref[...], preferred_element_type=F32)
              + g_ref[:, 2 * d:3 * d].astype(F32) * jnp.dot(ym_ref[...], wom_ref[...], preferred_element_type=F32))
    o = jnp.dot(merged.astype(BF16), wout_ref[...], preferred_element_type=F32)
    x1 = x_ref[...] + _rmsnorm_rows(o, gpm_ref[...])
    x1_ref[...] = x1
    h2_ref[...] = _rmsnorm_rows(x1, gpf_ref[...]).astype(h2_ref.dtype)


def _merge(x, gates, y_a, y_b, y_m, w_oa, w_ob, w_om, w_out, g_post_mix, g_pre_ffn):
    m, d = x.shape
    tm = _tile(m, 256)
    once = pl.Buffered(1)

    def rows(width):
        return pl.BlockSpec((tm, width), lambda i: (i, 0))

    def whole(a):
        return pl.BlockSpec(a.shape, lambda i: (0, 0), pipeline_mode=once)

    return pl.pallas_call(
        _merge_kernel,
        grid=(m // tm,),
        in_specs=[rows(d), rows(N_BRANCH * d), rows(y_a.shape[1]), rows(y_b.shape[1]), rows(y_m.shape[1]),
                  whole(w_oa), whole(w_ob), whole(w_om), whole(w_out),
                  pl.BlockSpec((1, d), lambda i: (0, 0)), pl.BlockSpec((1, d), lambda i: (0, 0))],
        out_specs=[rows(d), rows(d)],
        out_shape=[jax.ShapeDtypeStruct((m, d), F32), jax.ShapeDtypeStruct((m, d), BF16)],
        compiler_params=_params("parallel"),
        name="merge",
    )(x, gates, y_a, y_b, y_m, w_oa, w_ob, w_om, w_out, g_post_mix.reshape(1, d), g_pre_ffn.reshape(1, d))


def _ffn_kernel(h_ref, wg_ref, wu_ref, wd_ref, x1_ref, gpf_ref, o_ref, acc_ref):
    j = pl.program_id(1)

    @pl.when(j == 0)
    def _():
        acc_ref[...] = jnp.zeros(acc_ref.shape, F32)

    h = h_ref[...]
    a = jnp.dot(h, wg_ref[...], preferred_element_type=F32)
    c = jnp.dot(h, wu_ref[...], preferred_element_type=F32)
    act = (jax.nn.silu(a) * c).astype(BF16)
    acc_ref[...] += jnp.dot(act, wd_ref[...], preferred_element_type=F32)

    @pl.when(j == pl.num_programs(1) - 1)
    def _():
        o_ref[...] = x1_ref[...] + _rmsnorm_rows(acc_ref[...], gpf_ref[...])


def _ffn(h2, x1, w_gu, w_down, g_post_ffn):
    m, d = h2.shape
    d_ff = w_down.shape[0]
    tm = _tile(m, 512)
    tf = _tile(d_ff, 512)
    nf = d_ff // tf
    return pl.pallas_call(
        _ffn_kernel,
        grid=(m // tm, nf),
        in_specs=[
            pl.BlockSpec((tm, d), lambda i, j: (i, 0)),
            pl.BlockSpec((d, tf), lambda i, j: (0, j)),
            pl.BlockSpec((d, tf), lambda i, j: (0, j + nf)),
            pl.BlockSpec((tf, d), lambda i, j: (j, 0)),
            pl.BlockSpec((tm, d), lambda i, j: (i, 0)),
            pl.BlockSpec((1, d), lambda i, j: (0, 0)),
        ],
        out_specs=pl.BlockSpec((tm, d), lambda i, j: (i, 0)),
        out_shape=jax.ShapeDtypeStruct((m, d), F32),
        scratch_shapes=[pltpu.VMEM((tm, d), F32)],
        compiler_params=_params("parallel", "arbitrary"),
        name="ffn",
    )(h2, w_gu, w_gu, w_down, x1, g_post_ffn.reshape(1, d))


def _split_w_in(w_in, d):
    gw = d // 2
    aw = N_HEADS * HEAD_DIM
    iw = IDX_HEADS * IDX_DIM
    mw = MEM_HEADS * MEM_HEAD_DIM
    sizes = (2 * gw, aw, aw, aw, iw, IDX_DIM + IDX_HEADS, mw, N_BRANCH * d)
    parts, start = [], 0
    for s in sizes:
        parts.append(w_in[:, start:start + s].astype(BF16))
        start += s
    w_uv, w_q, w_k, w_v, w_qi, w_kw, w_qm, w_g = parts
    w_kw = jnp.pad(w_kw, ((0, 0), (0, 2 * LANES - w_kw.shape[1])))
    return w_uv, w_q, w_k, w_v, w_qi, w_kw, w_qm, w_g


def _project_common(x2d, g_pre_mix, ws, ln_g, ln_b, uv_dtype):
    w_uv, w_q, w_k, w_v, w_qi, w_kw, w_qm, w_g = ws
    h = _rmsnorm(x2d, g_pre_mix, BF16)
    uv = _project_uv(h, w_uv, ln_g, ln_b, uv_dtype)
    (q,) = _matmul(h, w_q, (BF16,), name="proj_q")
    k, k_bf = _matmul(h, w_k, (F32, BF16), name="proj_k")
    v, v_bf = _matmul(h, w_v, (F32, BF16), name="proj_v")
    (qi,) = _matmul(h, w_qi, (BF16,), name="proj_qi")
    (kw,) = _matmul(h, w_kw, (F32,), name="proj_kidx")
    (qm,) = _matmul(h, w_qm, (BF16,), name="proj_qm")
    (gates,) = _matmul(h, w_g, (BF16,), act="sigmoid", name="proj_gates")
    kidx = kw[:, :IDX_DIM]
    wi = kw[:, IDX_DIM:IDX_DIM + IDX_HEADS]
    return uv, q, k, k_bf, v, v_bf, qi, kidx, wi, qm, gates


def kernel(x_prompt, x_sample, mem_prompt, cache_k, cache_v, cache_kidx, cache_mem_k, cache_mem_v, page_table, g_pre_mix, g_post_mix, g_pre_ffn, g_post_ffn, g_mem, ln_v_g, ln_v_b, w_in, w_s, b_s, w_oa, w_ob, w_om, w_out, w_mem_kv, w_gu, w_down):
    depth = w_in.shape[0]
    b, t, d = x_prompt.shape
    db, dt, _ = x_sample.shape
    assert dt == 1, "sample path handles one new token per sequence"
    n_mem = mem_prompt.shape[1]
    aw = N_HEADS * HEAD_DIM
    mw = MEM_HEADS * MEM_HEAD_DIM
    gw = d // 2
    past = page_table.shape[1] * PAGE_SIZE

    yp = x_prompt.reshape(b * t, d)
    ys = x_sample.reshape(db * dt, d)
    outs = [[] for _ in range(9)]
    for l in range(depth):
        ws = _split_w_in(w_in[l], d)
        woa, wob, wom, wout = (w.astype(BF16) for w in (w_oa[l], w_ob[l], w_om[l], w_out[l]))
        wgu, wdown, wmem = w_gu[l].astype(BF16), w_down[l].astype(BF16), w_mem_kv[l].astype(BF16)

        uv, q, k, k_bf, v, v_bf, qi, kidx, wi, qm, gates = _project_common(
            yp, g_pre_mix[l], ws, ln_v_g[l], ln_v_b[l], BF16)
        y_a = _gmlp_spatial(uv, w_s[l], b_s[l])
        y_b = _dsa_prompt(
            q.reshape(b, t, aw), qi.reshape(b, t, -1), jnp.swapaxes(wi.reshape(b, t, IDX_HEADS), 1, 2),
            k_bf.reshape(b, t, aw), kidx.astype(BF16).reshape(b, t, IDX_DIM),
            jnp.swapaxes(v_bf.reshape(b, t, aw), 1, 2)).reshape(b * t, aw)
        hm = _rmsnorm(mem_prompt.reshape(b * n_mem, d), g_mem[l], BF16)
        (mkv,) = _matmul(hm, wmem, (F32,), name="proj_mem_kv")
        mk = mkv[:, :mw].reshape(b, n_mem, mw)
        mv = mkv[:, mw:].reshape(b, n_mem, mw)
        y_m = _mem_attend(qm.reshape(b, t, mw), mk, mv).reshape(b * t, mw)
        x1, h2 = _merge(yp, gates, y_a, y_b, y_m, woa, wob, wom, wout, g_post_mix[l], g_pre_ffn[l])
        yp = _ffn(h2, x1, wgu, wdown, g_post_ffn[l])
        outs[0].append(k.reshape(b, t, N_HEADS, HEAD_DIM))
        outs[1].append(v.reshape(b, t, N_HEADS, HEAD_DIM))
        outs[2].append(kidx.reshape(b, t, IDX_DIM))
        outs[3].append(mk.reshape(b, n_mem, MEM_HEADS, MEM_HEAD_DIM))
        outs[4].append(mv.reshape(b, n_mem, MEM_HEADS, MEM_HEAD_DIM))

        uv, q, k, k_bf, v, v_bf, qi, kidx, wi, qm, gates = _project_common(
            ys, g_pre_mix[l], ws, ln_v_g[l], ln_v_b[l], F32)
        y_a = _gmlp_first_row(uv, w_s[l], b_s[l])
        scores = _sample_scores(qi, wi, kidx, cache_kidx[l], page_table)
        mask = _sample_select(scores, past)
        y_b = _sample_attend(q, k, v, mask, cache_k[l], cache_v[l], page_table)
        qm_pad = jnp.pad(qm.reshape(db, 1, mw), ((0, 0), (0, 2 * SUBLANES - 1), (0, 0)))
        y_m = _mem_attend(qm_pad, cache_mem_k[l].reshape(db, n_mem, mw),
                          cache_mem_v[l].reshape(db, n_mem, mw))[:, 0, :]
        x1, h2 = _merge(ys, gates, y_a, y_b, y_m, woa, wob, wom, wout, g_post_mix[l], g_pre_ffn[l])
        ys = _ffn(h2, x1, wgu, wdown, g_post_ffn[l])
        outs[5].append(k.reshape(db, dt, N_HEADS, HEAD_DIM))
        outs[6].append(v.reshape(db, dt, N_HEADS, HEAD_DIM))
        outs[7].append(kidx.reshape(db, dt, IDX_DIM))
        outs[8].append(uv[:, gw:].reshape(db, dt, gw))

    return (yp.reshape(b, t, d), ys.reshape(db, dt, d)) + tuple(jnp.stack(o) for o in outs)
```

```python
import functools

import jax
import jax.numpy as jnp
from jax import lax
from jax.experimental import pallas as pl
from jax.experimental.pallas import tpu as pltpu

EPS = 1e-6
CHUNK = 128
GM_GROUPS = 8
N_HEADS = 8
HEAD_DIM = 128
IDX_HEADS = 16
IDX_DIM = 128
TOPK_MAX = 256
MEM_HEADS = 4
MEM_HEAD_DIM = 128
N_BRANCH = 3
PAGE_SIZE = 128

LANES = 128
SUBLANES = 8
VMEM_LIMIT_BYTES = 56 * 2**20
INT_MIN = -2**31
KEY_LOWEST = INT_MIN + 0x800000
KEY_INF = 0x7F800000
NEG = -1e30
LOG2_E = 1.4426950408889634
ONES_ROWS = 16

F32 = jnp.float32
BF16 = jnp.bfloat16
NT_DIMS = (((1,), (1,)), ((), ()))


def _params(*sem):
    return pltpu.CompilerParams(dimension_semantics=sem, vmem_limit_bytes=VMEM_LIMIT_BYTES)


def _tile(n, pref):
    if n <= pref:
        return n
    t = pref
    while n % t:
        t //= 2
    return t


def _key_to_float(key):
    key = jnp.clip(key, KEY_LOWEST, KEY_INF)
    return pltpu.bitcast(key ^ ((key >> 31) & 0x7FFFFFFF), F32)


def _kth_largest_key(count_ge, topk, like):
    base = jnp.where(count_ge(jnp.zeros_like(like)) >= topk, 0, INT_MIN).astype(jnp.int32)

    def bit_body(b, t):
        cand = t | jnp.left_shift(jnp.int32(1), 30 - b)
        return jnp.where(count_ge(cand) >= topk, cand, t)

    return lax.fori_loop(0, 31, bit_body, base)


def _rmsnorm_rows(x, g):
    return x * lax.rsqrt(jnp.mean(x * x, axis=-1, keepdims=True) + EPS) * g


def _rmsnorm_kernel(x_ref, g_ref, o_ref):
    o_ref[...] = _rmsnorm_rows(x_ref[...], g_ref[...]).astype(o_ref.dtype)


def _rmsnorm(x, g, out_dtype):
    m, d = x.shape
    tm = _tile(m, 512)
    return pl.pallas_call(
        _rmsnorm_kernel,
        grid=(m // tm,),
        in_specs=[pl.BlockSpec((tm, d), lambda i: (i, 0)), pl.BlockSpec((1, d), lambda i: (0, 0))],
        out_specs=pl.BlockSpec((tm, d), lambda i: (i, 0)),
        out_shape=jax.ShapeDtypeStruct((m, d), out_dtype),
        compiler_params=_params("parallel"),
        name="rmsnorm",
    )(x, g.reshape(1, d))


def _mm_kernel(x_ref, w_ref, *o_refs, act):
    acc = jnp.dot(x_ref[...], w_ref[...], preferred_element_type=F32)
    if act == "sigmoid":
        acc = jax.nn.sigmoid(acc)
    for o_ref in o_refs:
        o_ref[...] = acc.astype(o_ref.dtype)


def _matmul(x, w, out_dtypes, act=None, name="matmul"):
    m, k = x.shape
    n = w.shape[1]
    tm = _tile(m, 1024)
    tn = _tile(n, 512)
    outs = pl.pallas_call(
        functools.partial(_mm_kernel, act=act),
        grid=(m // tm, n // tn),
        in_specs=[pl.BlockSpec((tm, k), lambda i, j: (i, 0)), pl.BlockSpec((k, tn), lambda i, j: (0, j))],
        out_specs=[pl.BlockSpec((tm, tn), lambda i, j: (i, j)) for _ in out_dtypes],
        out_shape=[jax.ShapeDtypeStruct((m, n), dt) for dt in out_dtypes],
        compiler_params=_params("parallel", "parallel"),
        name=name,
    )(x, w)
    return outs


def _uv_kernel(x_ref, w_ref, lng_ref, lnb_ref, o_ref):
    y = jax.nn.gelu(jnp.dot(x_ref[...], w_ref[...], preferred_element_type=F32))

    @pl.when(pl.program_id(1) == 0)
    def _():
        o_ref[...] = y.astype(o_ref.dtype)

    @pl.when(pl.program_id(1) == 1)
    def _():
        yc = y - jnp.mean(y, axis=-1, keepdims=True)
        yn = yc * lax.rsqrt(jnp.mean(yc * yc, axis=-1, keepdims=True) + EPS)
        o_ref[...] = (yn * lng_ref[...] + lnb_ref[...]).astype(o_ref.dtype)


def _project_uv(h, w_uv, ln_g, ln_b, out_dtype):
    m, k = h.shape
    gw = w_uv.shape[1] // 2
    tm = _tile(m, 512)
    return pl.pallas_call(
        _uv_kernel,
        grid=(m // tm, 2),
        in_specs=[
            pl.BlockSpec((tm, k), lambda i, j: (i, 0)),
            pl.BlockSpec((k, gw), lambda i, j: (0, j)),
            pl.BlockSpec((1, gw), lambda i, j: (0, 0)),
            pl.BlockSpec((1, gw), lambda i, j: (0, 0)),
        ],
        out_specs=pl.BlockSpec((tm, gw), lambda i, j: (i, j)),
        out_shape=jax.ShapeDtypeStruct((m, 2 * gw), out_dtype),
        compiler_params=_params("parallel", "arbitrary"),
        name="project_uv",
    )(h, w_uv, ln_g.reshape(1, gw), ln_b.reshape(1, gw))


def _gmlp_kernel(u_ref, v_ref, ws_ref, bt_ref, o_ref, *, n_chunks):
    c = CHUNK
    gd = u_ref.shape[1] // GM_GROUPS
    causal = lax.broadcasted_iota(jnp.int32, (c, c), 1) <= lax.broadcasted_iota(jnp.int32, (c, c), 0)
    for g in range(GM_GROUPS):
        w = jnp.where(causal, ws_ref[g], 0.0).astype(v_ref.dtype)
        bcol = bt_ref[:, g:g + 1]
        for ci in range(n_chunks):
            rows = slice(ci * c, (ci + 1) * c)
            cols = slice(g * gd, (g + 1) * gd)
            mixed = jnp.dot(w, v_ref[rows, cols], preferred_element_type=F32) + bcol
            o_ref[rows, cols] = (u_ref[rows, cols].astype(F32) * mixed).astype(o_ref.dtype)


def _gmlp_spatial(uv, w_s, b_s):
    m = uv.shape[0]
    gw = uv.shape[1] // 2
    tm = _tile(m, 4 * CHUNK)
    return pl.pallas_call(
        functools.partial(_gmlp_kernel, n_chunks=tm // CHUNK),
        grid=(m // tm,),
        in_specs=[
            pl.BlockSpec((tm, gw), lambda i: (i, 0)),
            pl.BlockSpec((tm, gw), lambda i: (i, 1)),
            pl.BlockSpec((GM_GROUPS, CHUNK, CHUNK), lambda i: (0, 0, 0)),
            pl.BlockSpec((CHUNK, GM_GROUPS), lambda i: (0, 0)),
        ],
        out_specs=pl.BlockSpec((tm, gw), lambda i: (i, 0)),
        out_shape=jax.ShapeDtypeStruct((m, gw), BF16),
        compiler_params=_params("parallel"),
        name="gmlp_spatial",
    )(uv, uv, w_s, b_s.T)


def _gmlp_first_kernel(u_ref, v_ref, w_ref, b_ref, o_ref):
    o_ref[...] = (u_ref[...] * (w_ref[...] * v_ref[...] + b_ref[...])).astype(o_ref.dtype)


def _gmlp_first_row(uv, w_s, b_s):
    m = uv.shape[0]
    gw = uv.shape[1] // 2
    gd = gw // GM_GROUPS
    w_row = jnp.repeat(w_s[:, 0, 0], gd).reshape(1, gw)
    b_row = jnp.repeat(b_s[:, 0], gd).reshape(1, gw)
    return pl.pallas_call(
        _gmlp_first_kernel,
        grid=(1,),
        in_specs=[
            pl.BlockSpec((m, gw), lambda i: (0, 0)),
            pl.BlockSpec((m, gw), lambda i: (0, 1)),
            pl.BlockSpec((1, gw), lambda i: (0, 0)),
            pl.BlockSpec((1, gw), lambda i: (0, 0)),
        ],
        out_specs=pl.BlockSpec((m, gw), lambda i: (0, 0)),
        out_shape=jax.ShapeDtypeStruct((m, gw), BF16),
        compiler_params=_params("arbitrary"),
        name="gmlp_first_row",
    )(uv, uv, w_row, b_row)


def _dsa_kernel(q_ref, qi_ref, wit_ref, k_ref, kidx_ref, vt_ref, o_ref,
                sc_ref, m_ref, acc_ref, s_ref, p_ref, *, tq, tk, topk, wscale):
    i = pl.program_id(1)
    n_chunks = ((i + 1) * tq + tk - 1) // tk
    qpos = i * tq + lax.broadcasted_iota(jnp.int32, (1, tq), 1)
    vrows = vt_ref.shape[1] // N_HEADS

    def idx_body(c, carry):
        k0 = pl.multiple_of(c * tk, tk)
        kc = kidx_ref[0, pl.ds(k0, tk), :]
        acc = jnp.zeros((tk, tq), F32)
        for h in range(IDX_HEADS):
            s = lax.dot_general(kc, qi_ref[0, :, h * IDX_DIM:(h + 1) * IDX_DIM], NT_DIMS,
                                preferred_element_type=F32)
            acc = acc + jnp.maximum(s, 0.0) * (wit_ref[0, h:h + 1, :] * wscale)
        kpos = k0 + lax.broadcasted_iota(jnp.int32, (tk, 1), 0)
        sc_ref[pl.ds(k0, tk), :] = jnp.where(kpos <= qpos, acc, -jnp.inf)
        return carry

    lax.fori_loop(0, n_chunks, idx_body, 0)

    def count_ge(cand):
        cand_f = _key_to_float(cand)

        def body(c, cnt):
            k0 = pl.multiple_of(c * tk, tk)
            hit = jnp.where(sc_ref[pl.ds(k0, tk), :] >= cand_f, 1.0, 0.0)
            return cnt + jnp.sum(hit.reshape(tk // SUBLANES, SUBLANES, tq), axis=0)

        cnt = lax.fori_loop(0, n_chunks, body, jnp.zeros((SUBLANES, tq), F32))
        return jnp.sum(cnt, axis=0, keepdims=True)

    thr = _key_to_float(_kth_largest_key(count_ge, topk, qpos))

    m_ref[...] = jnp.full(m_ref.shape, NEG, F32)
    acc_ref[...] = jnp.zeros(acc_ref.shape, F32)
    c_exp = HEAD_DIM ** -0.5 * LOG2_E

    def att_body(c, carry):
        k0 = pl.multiple_of(c * tk, tk)
        bias = jnp.where(sc_ref[pl.ds(k0, tk), :] >= thr, 0.0, NEG)
        m_old, m_new = [], []
        for h in range(N_HEADS):
            hs = slice(h * HEAD_DIM, (h + 1) * HEAD_DIM)
            s = lax.dot_general(k_ref[0, pl.ds(k0, tk), hs], q_ref[0, :, hs], NT_DIMS,
                                preferred_element_type=F32) * c_exp + bias
            s_ref[h] = s
            m_old.append(m_ref[h])
            m_new.append(jnp.maximum(m_old[h], jnp.max(s, axis=0, keepdims=True)))
            m_ref[h] = m_new[h]
        for h in range(N_HEADS):
            p_ref[h] = jnp.exp2(s_ref[h] - m_new[h][0:1, :]).astype(BF16)
        for h in range(N_HEADS):
            pv = jnp.dot(vt_ref[0, h * vrows:(h + 1) * vrows, pl.ds(k0, tk)], p_ref[h],
                         preferred_element_type=F32)
            acc_ref[h] = jnp.exp2(m_old[h] - m_new[h])[0:1, :] * acc_ref[h] + pv
        return carry

    lax.fori_loop(0, n_chunks, att_body, 0)

    for h in range(N_HEADS):
        o = acc_ref[h, 0:HEAD_DIM, :] / acc_ref[h, HEAD_DIM:HEAD_DIM + 1, :]
        o_ref[0, :, h * HEAD_DIM:(h + 1) * HEAD_DIM] = o.T.astype(o_ref.dtype)


def _dsa_prompt(q, qi, wi_t, k, kidx, v, tq=256, tk=256):
    b, t, aw = q.shape
    ones = jnp.ones((b, N_HEADS, ONES_ROWS, t), v.dtype)
    v_t = jnp.concatenate([jnp.transpose(v.reshape(b, t, N_HEADS, HEAD_DIM), (0, 2, 3, 1)), ones], axis=2)
    v_t = v_t.reshape(b, N_HEADS * (HEAD_DIM + ONES_ROWS), t)
    tq = _tile(t, tq)
    tk = _tile(tq, tk)
    topk = min(TOPK_MAX, t // 4)
    once = pl.Buffered(1)
    return pl.pallas_call(
        functools.partial(_dsa_kernel, tq=tq, tk=tk, topk=topk,
                          wscale=IDX_HEADS ** -0.5 * IDX_DIM ** -0.5),
        grid=(b, t // tq),
        in_specs=[
            pl.BlockSpec((1, tq, aw), lambda bi, i: (bi, i, 0)),
            pl.BlockSpec((1, tq, IDX_HEADS * IDX_DIM), lambda bi, i: (bi, i, 0)),
            pl.BlockSpec((1, IDX_HEADS, tq), lambda bi, i: (bi, 0, i)),
            pl.BlockSpec((1, t, aw), lambda bi, i: (bi, 0, 0), pipeline_mode=once),
            pl.BlockSpec((1, t, IDX_DIM), lambda bi, i: (bi, 0, 0), pipeline_mode=once),
            pl.BlockSpec((1, v_t.shape[1], t), lambda bi, i: (bi, 0, 0), pipeline_mode=once),
        ],
        out_specs=pl.BlockSpec((1, tq, aw), lambda bi, i: (bi, i, 0)),
        out_shape=jax.ShapeDtypeStruct((b, t, aw), BF16),
        scratch_shapes=[
            pltpu.VMEM((t, tq), F32),
            pltpu.VMEM((N_HEADS, SUBLANES, tq), F32),
            pltpu.VMEM((N_HEADS, HEAD_DIM + ONES_ROWS, tq), F32),
            pltpu.VMEM((N_HEADS, tk, tq), F32),
            pltpu.VMEM((N_HEADS, tk, tq), BF16),
        ],
        compiler_params=_params("parallel", "arbitrary"),
        name="dsa_prompt",
    )(q, qi, wi_t, k, kidx, v_t)


def _memattn_kernel(qm_ref, mk_ref, mv_ref, o_ref):
    scale = MEM_HEAD_DIM ** -0.5
    for h in range(MEM_HEADS):
        hs = slice(h * MEM_HEAD_DIM, (h + 1) * MEM_HEAD_DIM)
        s = lax.dot_general(qm_ref[0, :, hs], mk_ref[0, :, hs].astype(BF16), NT_DIMS,
                            preferred_element_type=F32) * scale
        e = jnp.exp(s - jnp.max(s, axis=-1, keepdims=True))
        p = e / jnp.sum(e, axis=-1, keepdims=True)
        o = jnp.dot(p.astype(BF16), mv_ref[0, :, hs].astype(BF16), preferred_element_type=F32)
        o_ref[0, :, hs] = o.astype(o_ref.dtype)


def _mem_attend(qm, mk, mv):
    b, t, mw = qm.shape
    n_mem = mk.shape[1]
    tm = _tile(t, 512)
    return pl.pallas_call(
        _memattn_kernel,
        grid=(b, t // tm),
        in_specs=[
            pl.BlockSpec((1, tm, mw), lambda bi, i: (bi, i, 0)),
            pl.BlockSpec((1, n_mem, mw), lambda bi, i: (bi, 0, 0)),
            pl.BlockSpec((1, n_mem, mw), lambda bi, i: (bi, 0, 0)),
        ],
        out_specs=pl.BlockSpec((1, tm, mw), lambda bi, i: (bi, i, 0)),
        out_shape=jax.ShapeDtypeStruct((b, t, mw), BF16),
        compiler_params=_params("parallel", "parallel"),
        name="mem_attend",
    )(qm, mk, mv)


def _sscore_kernel(pt_ref, qi_ref, wi_ref, kidx_ref, knew_ref, o_ref, *, wscale):
    p = pl.program_id(1)
    is_new = p == pl.num_programs(1) - 1
    page = kidx_ref[0, 0]
    kc = jnp.where(is_new, jnp.broadcast_to(knew_ref[0], page.shape), page).astype(BF16)
    s = lax.dot_general(qi_ref[0], kc, NT_DIMS, preferred_element_type=F32)
    r = jnp.maximum(s, 0.0) * (wi_ref[0] * wscale)
    o_ref[0, 0] = jnp.sum(r, axis=0, keepdims=True)


def _sample_scores(qi, wi, kidx_new, cache_kidx, page_table):
    db, n_pages = page_table.shape
    grid_spec = pltpu.PrefetchScalarGridSpec(
        num_scalar_prefetch=1,
        grid=(db, n_pages + 1),
        in_specs=[
            pl.BlockSpec((1, IDX_HEADS, IDX_DIM), lambda b, p, pt: (b, 0, 0)),
            pl.BlockSpec((1, IDX_HEADS, 1), lambda b, p, pt: (b, 0, 0)),
            pl.BlockSpec((1, 1, PAGE_SIZE, IDX_DIM),
                         lambda b, p, pt: (0, pt[b, jnp.minimum(p, pt.shape[1] - 1)], 0, 0)),
            pl.BlockSpec((1, 1, IDX_DIM), lambda b, p, pt: (b, 0, 0)),
        ],
        out_specs=pl.BlockSpec((1, 1, 1, PAGE_SIZE), lambda b, p, pt: (b, p, 0, 0)),
    )
    out = pl.pallas_call(
        functools.partial(_sscore_kernel, wscale=IDX_HEADS ** -0.5 * IDX_DIM ** -0.5),
        grid_spec=grid_spec,
        out_shape=jax.ShapeDtypeStruct((db, n_pages + 1, 1, PAGE_SIZE), F32),
        compiler_params=_params("parallel", "arbitrary"),
        name="sample_scores",
    )(page_table, qi.reshape(db, IDX_HEADS, IDX_DIM), wi.reshape(db, IDX_HEADS, 1),
      cache_kidx.reshape(1, -1, PAGE_SIZE, IDX_DIM), kidx_new.reshape(db, 1, IDX_DIM))
    return out.reshape(db, n_pages + 1, PAGE_SIZE)


def _sselect_kernel(s_ref, mask_ref, *, past, topk):
    s = s_ref[0]
    pos = (lax.broadcasted_iota(jnp.int32, s.shape, 0) * PAGE_SIZE
           + lax.broadcasted_iota(jnp.int32, s.shape, 1))
    s = jnp.where(pos <= past, s, -jnp.inf)

    def count_ge(cand):
        hit = jnp.where(s >= _key_to_float(cand), 1.0, 0.0)
        return jnp.sum(jnp.sum(hit, axis=1, keepdims=True), axis=0, keepdims=True)

    thr = _key_to_float(_kth_largest_key(count_ge, topk, jnp.zeros((1, 1), jnp.int32)))
    mask_ref[0] = jnp.where(s >= thr, 1.0, 0.0)


def _sample_select(scores, past):
    db, rows, _ = scores.shape
    topk = min(TOPK_MAX, (past + 1) // 4)
    return pl.pallas_call(
        functools.partial(_sselect_kernel, past=past, topk=topk),
        grid=(db,),
        in_specs=[pl.BlockSpec((1, rows, PAGE_SIZE), lambda b: (b, 0, 0))],
        out_specs=pl.BlockSpec((1, rows, PAGE_SIZE), lambda b: (b, 0, 0)),
        out_shape=jax.ShapeDtypeStruct(scores.shape, F32),
        compiler_params=_params("parallel"),
        name="sample_select",
    )(scores)


def _sattn_kernel(pt_ref, q_ref, k_ref, v_ref, mask_ref, mnew_ref, knew_ref, vnew_ref, o_ref,
                  m_ref, l_ref, acc_ref):
    p = pl.program_id(1)
    aw = q_ref.shape[2]
    scale = HEAD_DIM ** -0.5
    own = (lax.broadcasted_iota(jnp.int32, (N_HEADS, aw), 1) // HEAD_DIM
           == lax.broadcasted_iota(jnp.int32, (N_HEADS, aw), 0))
    qbd = jnp.where(own, q_ref[0].astype(F32), 0.0).astype(BF16)

    @pl.when(p == 0)
    def _():
        m_ref[...] = jnp.full(m_ref.shape, NEG, F32)
        l_ref[...] = jnp.zeros(l_ref.shape, F32)
        acc_ref[...] = jnp.zeros(acc_ref.shape, F32)

    def update(s, sel, pv_fn):
        m_old = m_ref[...]
        m_new = jnp.maximum(m_old, jnp.max(jnp.where(sel, s, NEG), axis=1, keepdims=True))
        pr = jnp.where(sel, jnp.exp(s - m_new), 0.0)
        alpha = jnp.exp(m_old - m_new)
        l_ref[...] = alpha * l_ref[...] + jnp.sum(pr, axis=1, keepdims=True)
        acc_ref[...] = alpha * acc_ref[...] + pv_fn(pr.astype(BF16))
        m_ref[...] = m_new

    s = lax.dot_general(qbd, k_ref[0, 0].astype(BF16), NT_DIMS, preferred_element_type=F32) * scale
    update(s, mask_ref[0, 0] > 0.5,
           lambda pr: jnp.dot(pr, v_ref[0, 0].astype(BF16), preferred_element_type=F32))

    @pl.when(p == pl.num_programs(1) - 1)
    def _():
        kn = knew_ref[0].astype(BF16).astype(F32)
        vn = vnew_ref[0].astype(BF16).astype(F32)
        s_new = jnp.sum(qbd.astype(F32) * kn, axis=1, keepdims=True) * scale
        update(s_new, mnew_ref[0, 0][:, 0:1] > 0.5, lambda pr: pr.astype(F32) * vn)
        o = jnp.where(own, acc_ref[...] / l_ref[...], 0.0)
        o_ref[0] = jnp.sum(o, axis=0, keepdims=True).astype(o_ref.dtype)


def _sample_attend(q, k_new, v_new, mask, cache_k, cache_v, page_table):
    db, n_pages = page_table.shape
    aw = q.shape[1]
    mask4 = mask.reshape(db, n_pages + 1, 1, PAGE_SIZE)
    grid_spec = pltpu.PrefetchScalarGridSpec(
        num_scalar_prefetch=1,
        grid=(db, n_pages),
        in_specs=[
            pl.BlockSpec((1, 1, aw), lambda b, p, pt: (b, 0, 0)),
            pl.BlockSpec((1, 1, PAGE_SIZE, aw), lambda b, p, pt: (0, pt[b, p], 0, 0)),
            pl.BlockSpec((1, 1, PAGE_SIZE, aw), lambda b, p, pt: (0, pt[b, p], 0, 0)),
            pl.BlockSpec((1, 1, 1, PAGE_SIZE), lambda b, p, pt: (b, p, 0, 0)),
            pl.BlockSpec((1, 1, 1, PAGE_SIZE), lambda b, p, pt: (b, pt.shape[1], 0, 0)),
            pl.BlockSpec((1, 1, aw), lambda b, p, pt: (b, 0, 0)),
            pl.BlockSpec((1, 1, aw), lambda b, p, pt: (b, 0, 0)),
        ],
        out_specs=pl.BlockSpec((1, 1, aw), lambda b, p, pt: (b, 0, 0)),
        scratch_shapes=[
            pltpu.VMEM((N_HEADS, 1), F32),
            pltpu.VMEM((N_HEADS, 1), F32),
            pltpu.VMEM((N_HEADS, aw), F32),
        ],
    )
    out = pl.pallas_call(
        _sattn_kernel,
        grid_spec=grid_spec,
        out_shape=jax.ShapeDtypeStruct((db, 1, aw), BF16),
        compiler_params=_params("parallel", "arbitrary"),
        name="sample_attend",
    )(page_table, q.reshape(db, 1, aw), cache_k.reshape(1, -1, PAGE_SIZE, aw),
      cache_v.reshape(1, -1, PAGE_SIZE, aw), mask4, mask4,
      k_new.reshape(db, 1, aw), v_new.reshape(db, 1, aw))
    return out.reshape(db, aw)


def _merge_kernel(x_ref, g_ref, ya_ref, yb_ref, ym_ref, woa_ref, wob_ref, wom_ref, wout_ref,
                  gpm_ref, gpf_ref, x1_ref, h2_ref):
    d = x_ref.shape[1]
    merged = (g_ref[:, 0:d].astype(F32) * jnp.dot(ya_ref[...], woa_ref[...], preferred_element_type=F32)
              + g_ref[:, d:2 * d].astype(F32) * jnp.dot(yb_ref[...], wob_ref[...], preferred_element_type=F32)
              + g_ref[:, 2 * d:3 * d].astype(F32) * jnp.dot(ym_ref[...], wom_ref[...], preferred_element_type=F32))
    o = jnp.dot(merged.astype(BF16), wout_ref[...], preferred_element_type=F32)
    x1 = x_ref[...] + _rmsnorm_rows(o, gpm_ref[...])
    x1_ref[...] = x1
    h2_ref[...] = _rmsnorm_rows(x1, gpf_ref[...]).astype(h2_ref.dtype)


def _merge(x, gates, y_a, y_b, y_m, w_oa, w_ob, w_om, w_out, g_post_mix, g_pre_ffn):
    m, d = x.shape
    tm = _tile(m, 256)
    once = pl.Buffered(1)

    def rows(width):
        return pl.BlockSpec((tm, width), lambda i: (i, 0))

    def whole(a):
        return pl.BlockSpec(a.shape, lambda i: (0, 0), pipeline_mode=once)

    return pl.pallas_call(
        _merge_kernel,
        grid=(m // tm,),
        in_specs=[rows(d), rows(N_BRANCH * d), rows(y_a.shape[1]), rows(y_b.shape[1]), rows(y_m.shape[1]),
                  whole(w_oa), whole(w_ob), whole(w_om), whole(w_out),
                  pl.BlockSpec((1, d), lambda i: (0, 0)), pl.BlockSpec((1, d), lambda i: (0, 0))],
        out_specs=[rows(d), rows(d)],
        out_shape=[jax.ShapeDtypeStruct((m, d), F32), jax.ShapeDtypeStruct((m, d), BF16)],
        compiler_params=_params("parallel"),
        name="merge",
    )(x, gates, y_a, y_b, y_m, w_oa, w_ob, w_om, w_out, g_post_mix.reshape(1, d), g_pre_ffn.reshape(1, d))


def _ffn_kernel(h_ref, wg_ref, wu_ref, wd_ref, x1_ref, gpf_ref, o_ref, acc_ref):
    j = pl.program_id(1)

    @pl.when(j == 0)
    def _():
        acc_ref[...] = jnp.zeros(acc_ref.shape, F32)

    h = h_ref[...]
    a = jnp.dot(h, wg_ref[...], preferred_element_type=F32)
    c = jnp.dot(h, wu_ref[...], preferred_element_type=F32)
    act = (jax.nn.silu(a) * c).astype(BF16)
    acc_ref[...] += jnp.dot(act, wd_ref[...], preferred_element_type=F32)

    @pl.when(j == pl.num_programs(1) - 1)
    def _():
        o_ref[...] = x1_ref[...] + _rmsnorm_rows(acc_ref[...], gpf_ref[...])


def _ffn(h2, x1, w_gu, w_down, g_post_ffn):
    m, d = h2.shape
    d_ff = w_down.shape[0]
    tm = _tile(m, 512)
    tf = _tile(d_ff, 512)
    nf = d_ff // tf
    return pl.pallas_call(
        _ffn_kernel,
        grid=(m // tm, nf),
        in_specs=[
            pl.BlockSpec((tm, d), lambda i, j: (i, 0)),
            pl.BlockSpec((d, tf), lambda i, j: (0, j)),
            pl.BlockSpec((d, tf), lambda i, j: (0, j + nf)),
            pl.BlockSpec((tf, d), lambda i, j: (j, 0)),
            pl.BlockSpec((tm, d), lambda i, j: (i, 0)),
            pl.BlockSpec((1, d), lambda i, j: (0, 0)),
        ],
        out_specs=pl.BlockSpec((tm, d), lambda i, j: (i, 0)),
        out_shape=jax.ShapeDtypeStruct((m, d), F32),
        scratch_shapes=[pltpu.VMEM((tm, d), F32)],
        compiler_params=_params("parallel", "arbitrary"),
        name="ffn",
    )(h2, w_gu, w_gu, w_down, x1, g_post_ffn.reshape(1, d))


def _split_w_in(w_in, d):
    gw = d // 2
    aw = N_HEADS * HEAD_DIM
    iw = IDX_HEADS * IDX_DIM
    mw = MEM_HEADS * MEM_HEAD_DIM
    sizes = (2 * gw, aw, aw, aw, iw, IDX_DIM + IDX_HEADS, mw, N_BRANCH * d)
    parts, start = [], 0
    for s in sizes:
        parts.append(w_in[:, start:start + s].astype(BF16))
        start += s
    w_uv, w_q, w_k, w_v, w_qi, w_kw, w_qm, w_g = parts
    w_kw = jnp.pad(w_kw, ((0, 0), (0, 2 * LANES - w_kw.shape[1])))
    return w_uv, w_q, w_k, w_v, w_qi, w_kw, w_qm, w_g


def _project_common(x2d, g_pre_mix, ws, ln_g, ln_b, uv_dtype):
    w_uv, w_q, w_k, w_v, w_qi, w_kw, w_qm, w_g = ws
    h = _rmsnorm(x2d, g_pre_mix, BF16)
    uv = _project_uv(h, w_uv, ln_g, ln_b, uv_dtype)
    (q,) = _matmul(h, w_q, (BF16,), name="proj_q")
    k, k_bf = _matmul(h, w_k, (F32, BF16), name="proj_k")
    v, v_bf = _matmul(h, w_v, (F32, BF16), name="proj_v")
    (qi,) = _matmul(h, w_qi, (BF16,), name="proj_qi")
    (kw,) = _matmul(h, w_kw, (F32,), name="proj_kidx")
    (qm,) = _matmul(h, w_qm, (BF16,), name="proj_qm")
    (gates,) = _matmul(h, w_g, (BF16,), act="sigmoid", name="proj_gates")
    kidx = kw[:, :IDX_DIM]
    wi = kw[:, IDX_DIM:IDX_DIM + IDX_HEADS]
    return uv, q, k, k_bf, v, v_bf, qi, kidx, wi, qm, gates


def kernel(x_prompt, x_sample, mem_prompt, cache_k, cache_v, cache_kidx, cache_mem_k, cache_mem_v, page_table, g_pre_mix, g_post_mix, g_pre_ffn, g_post_ffn, g_mem, ln_v_g, ln_v_b, w_in, w_s, b_s, w_oa, w_ob, w_om, w_out, w_mem_kv, w_gu, w_down):
    depth = w_in.shape[0]
    b, t, d = x_prompt.shape
    db, dt, _ = x_sample.shape
    assert dt == 1, "sample path handles one new token per sequence"
    n_mem = mem_prompt.shape[1]
    aw = N_HEADS * HEAD_DIM
    mw = MEM_HEADS * MEM_HEAD_DIM
    gw = d // 2
    past = page_table.shape[1] * PAGE_SIZE

    yp = x_prompt.reshape(b * t, d)
    ys = x_sample.reshape(db * dt, d)
    outs = [[] for _ in range(9)]
    for l in range(depth):
        ws = _split_w_in(w_in[l], d)
        woa, wob, wom, wout = (w.astype(BF16) for w in (w_oa[l], w_ob[l], w_om[l], w_out[l]))
        wgu, wdown, wmem = w_gu[l].astype(BF16), w_down[l].astype(BF16), w_mem_kv[l].astype(BF16)

        uv, q, k, k_bf, v, v_bf, qi, kidx, wi, qm, gates = _project_common(
            yp, g_pre_mix[l], ws, ln_v_g[l], ln_v_b[l], BF16)
        y_a = _gmlp_spatial(uv, w_s[l], b_s[l])
        y_b = _dsa_prompt(
            q.reshape(b, t, aw), qi.reshape(b, t, -1), jnp.swapaxes(wi.reshape(b, t, IDX_HEADS), 1, 2),
            k_bf.reshape(b, t, aw), kidx.astype(BF16).reshape(b, t, IDX_DIM),
            v_bf.reshape(b, t, aw)).reshape(b * t, aw)
        hm = _rmsnorm(mem_prompt.reshape(b * n_mem, d), g_mem[l], BF16)
        (mkv,) = _matmul(hm, wmem, (F32,), name="proj_mem_kv")
        mk = mkv[:, :mw].reshape(b, n_mem, mw)
        mv = mkv[:, mw:].reshape(b, n_mem, mw)
        y_m = _mem_attend(qm.reshape(b, t, mw), mk, mv).reshape(b * t, mw)
        x1, h2 = _merge(yp, gates, y_a, y_b, y_m, woa, wob, wom, wout, g_post_mix[l], g_pre_ffn[l])
        yp = _ffn(h2, x1, wgu, wdown, g_post_ffn[l])
        outs[0].append(k.reshape(b, t, N_HEADS, HEAD_DIM))
        outs[1].append(v.reshape(b, t, N_HEADS, HEAD_DIM))
        outs[2].append(kidx.reshape(b, t, IDX_DIM))
        outs[3].append(mk.reshape(b, n_mem, MEM_HEADS, MEM_HEAD_DIM))
        outs[4].append(mv.reshape(b, n_mem, MEM_HEADS, MEM_HEAD_DIM))

        uv, q, k, k_bf, v, v_bf, qi, kidx, wi, qm, gates = _project_common(
            ys, g_pre_mix[l], ws, ln_v_g[l], ln_v_b[l], F32)
        y_a = _gmlp_first_row(uv, w_s[l], b_s[l])
        pages = page_table + l * cache_k.shape[1]
        scores = _sample_scores(qi, wi, kidx, cache_kidx, pages)
        mask = _sample_select(scores, past)
        y_b = _sample_attend(q, k, v, mask, cache_k, cache_v, pages)
        qm_pad = jnp.pad(qm.reshape(db, 1, mw), ((0, 0), (0, 2 * SUBLANES - 1), (0, 0)))
        y_m = _mem_attend(qm_pad, cache_mem_k[l].reshape(db, n_mem, mw),
                          cache_mem_v[l].reshape(db, n_mem, mw))[:, 0, :]
        x1, h2 = _merge(ys, gates, y_a, y_b, y_m, woa, wob, wom, wout, g_post_mix[l], g_pre_ffn[l])
        ys = _ffn(h2, x1, wgu, wdown, g_post_ffn[l])
        outs[5].append(k.reshape(db, dt, N_HEADS, HEAD_DIM))
        outs[6].append(v.reshape(db, dt, N_HEADS, HEAD_DIM))
        outs[7].append(kidx.reshape(db, dt, IDX_DIM))
        outs[8].append(uv[:, gw:].reshape(db, dt, gw))

    return (yp.reshape(b, t, d), ys.reshape(db, dt, d)) + tuple(jnp.stack(o) for o in outs)
```

```python
import functools

import jax
import jax.numpy as jnp
from jax import lax
from jax.experimental import pallas as pl
from jax.experimental.pallas import tpu as pltpu

EPS = 1e-6
CHUNK = 128
GM_GROUPS = 8
N_HEADS = 8
HEAD_DIM = 128
IDX_HEADS = 16
IDX_DIM = 128
TOPK_MAX = 256
MEM_HEADS = 4
MEM_HEAD_DIM = 128
N_BRANCH = 3
PAGE_SIZE = 128

LANES = 128
SUBLANES = 8
VMEM_LIMIT_BYTES = 56 * 2**20
INT_MIN = -2**31
KEY_LOWEST = INT_MIN + 0x800000
KEY_INF = 0x7F800000
NEG = -1e30
LOG2_E = 1.4426950408889634
ONES_ROWS = 16

F32 = jnp.float32
BF16 = jnp.bfloat16
NT_DIMS = (((1,), (1,)), ((), ()))


def _params(*sem):
    return pltpu.CompilerParams(dimension_semantics=sem, vmem_limit_bytes=VMEM_LIMIT_BYTES)


def _tile(n, pref):
    if n <= pref:
        return n
    t = pref
    while n % t:
        t //= 2
    return t


def _key_to_float(key):
    key = jnp.clip(key, KEY_LOWEST, KEY_INF)
    return pltpu.bitcast(key ^ ((key >> 31) & 0x7FFFFFFF), F32)


def _kth_largest_key(count_ge, topk, like):
    base = jnp.where(count_ge(jnp.zeros_like(like)) >= topk, 0, INT_MIN).astype(jnp.int32)

    def bit_body(b, t):
        cand = t | jnp.left_shift(jnp.int32(1), 30 - b)
        return jnp.where(count_ge(cand) >= topk, cand, t)

    return lax.fori_loop(0, 31, bit_body, base)


def _rmsnorm_rows(x, g):
    return x * lax.rsqrt(jnp.mean(x * x, axis=-1, keepdims=True) + EPS) * g


def _rmsnorm_kernel(x_ref, g_ref, o_ref):
    o_ref[...] = _rmsnorm_rows(x_ref[...], g_ref[...]).astype(o_ref.dtype)


def _rmsnorm(x, g, out_dtype):
    m, d = x.shape
    tm = _tile(m, 512)
    return pl.pallas_call(
        _rmsnorm_kernel,
        grid=(m // tm,),
        in_specs=[pl.BlockSpec((tm, d), lambda i: (i, 0)), pl.BlockSpec((1, d), lambda i: (0, 0))],
        out_specs=pl.BlockSpec((tm, d), lambda i: (i, 0)),
        out_shape=jax.ShapeDtypeStruct((m, d), out_dtype),
        compiler_params=_params("parallel"),
        name="rmsnorm",
    )(x, g.reshape(1, d))


def _mm_kernel(x_ref, w_ref, *o_refs, act):
    acc = jnp.dot(x_ref[...], w_ref[...], preferred_element_type=F32)
    if act == "sigmoid":
        acc = jax.nn.sigmoid(acc)
    for o_ref in o_refs:
        o_ref[...] = acc.astype(o_ref.dtype)


def _matmul(x, w, out_dtypes, act=None, name="matmul"):
    m, k = x.shape
    n = w.shape[1]
    tm = _tile(m, 1024)
    tn = _tile(n, 512)
    outs = pl.pallas_call(
        functools.partial(_mm_kernel, act=act),
        grid=(m // tm, n // tn),
        in_specs=[pl.BlockSpec((tm, k), lambda i, j: (i, 0)), pl.BlockSpec((k, tn), lambda i, j: (0, j))],
        out_specs=[pl.BlockSpec((tm, tn), lambda i, j: (i, j)) for _ in out_dtypes],
        out_shape=[jax.ShapeDtypeStruct((m, n), dt) for dt in out_dtypes],
        compiler_params=_params("parallel", "parallel"),
        name=name,
    )(x, w)
    return outs


def _uv_kernel(x_ref, w_ref, lng_ref, lnb_ref, o_ref):
    y = jax.nn.gelu(jnp.dot(x_ref[...], w_ref[...], preferred_element_type=F32))

    @pl.when(pl.program_id(1) == 0)
    def _():
        o_ref[...] = y.astype(o_ref.dtype)

    @pl.when(pl.program_id(1) == 1)
    def _():
        yc = y - jnp.mean(y, axis=-1, keepdims=True)
        yn = yc * lax.rsqrt(jnp.mean(yc * yc, axis=-1, keepdims=True) + EPS)
        o_ref[...] = (yn * lng_ref[...] + lnb_ref[...]).astype(o_ref.dtype)


def _project_uv(h, w_uv, ln_g, ln_b, out_dtype):
    m, k = h.shape
    gw = w_uv.shape[1] // 2
    tm = _tile(m, 512)
    return pl.pallas_call(
        _uv_kernel,
        grid=(m // tm, 2),
        in_specs=[
            pl.BlockSpec((tm, k), lambda i, j: (i, 0)),
            pl.BlockSpec((k, gw), lambda i, j: (0, j)),
            pl.BlockSpec((1, gw), lambda i, j: (0, 0)),
            pl.BlockSpec((1, gw), lambda i, j: (0, 0)),
        ],
        out_specs=pl.BlockSpec((tm, gw), lambda i, j: (i, j)),
        out_shape=jax.ShapeDtypeStruct((m, 2 * gw), out_dtype),
        compiler_params=_params("parallel", "arbitrary"),
        name="project_uv",
    )(h, w_uv, ln_g.reshape(1, gw), ln_b.reshape(1, gw))


def _gmlp_kernel(u_ref, v_ref, ws_ref, bt_ref, o_ref, *, n_chunks):
    c = CHUNK
    gd = u_ref.shape[1] // GM_GROUPS
    causal = lax.broadcasted_iota(jnp.int32, (c, c), 1) <= lax.broadcasted_iota(jnp.int32, (c, c), 0)
    for g in range(GM_GROUPS):
        w = jnp.where(causal, ws_ref[g], 0.0).astype(v_ref.dtype)
        bcol = bt_ref[:, g:g + 1]
        for ci in range(n_chunks):
            rows = slice(ci * c, (ci + 1) * c)
            cols = slice(g * gd, (g + 1) * gd)
            mixed = jnp.dot(w, v_ref[rows, cols], preferred_element_type=F32) + bcol
            o_ref[rows, cols] = (u_ref[rows, cols].astype(F32) * mixed).astype(o_ref.dtype)


def _gmlp_spatial(uv, w_s, b_s):
    m = uv.shape[0]
    gw = uv.shape[1] // 2
    tm = _tile(m, 4 * CHUNK)
    return pl.pallas_call(
        functools.partial(_gmlp_kernel, n_chunks=tm // CHUNK),
        grid=(m // tm,),
        in_specs=[
            pl.BlockSpec((tm, gw), lambda i: (i, 0)),
            pl.BlockSpec((tm, gw), lambda i: (i, 1)),
            pl.BlockSpec((GM_GROUPS, CHUNK, CHUNK), lambda i: (0, 0, 0)),
            pl.BlockSpec((CHUNK, GM_GROUPS), lambda i: (0, 0)),
        ],
        out_specs=pl.BlockSpec((tm, gw), lambda i: (i, 0)),
        out_shape=jax.ShapeDtypeStruct((m, gw), BF16),
        compiler_params=_params("parallel"),
        name="gmlp_spatial",
    )(uv, uv, w_s, b_s.T)


def _gmlp_first_kernel(u_ref, v_ref, w_ref, b_ref, o_ref):
    o_ref[...] = (u_ref[...] * (w_ref[...] * v_ref[...] + b_ref[...])).astype(o_ref.dtype)


def _gmlp_first_row(uv, w_s, b_s):
    m = uv.shape[0]
    gw = uv.shape[1] // 2
    gd = gw // GM_GROUPS
    w_row = jnp.repeat(w_s[:, 0, 0], gd).reshape(1, gw)
    b_row = jnp.repeat(b_s[:, 0], gd).reshape(1, gw)
    return pl.pallas_call(
        _gmlp_first_kernel,
        grid=(1,),
        in_specs=[
            pl.BlockSpec((m, gw), lambda i: (0, 0)),
            pl.BlockSpec((m, gw), lambda i: (0, 1)),
            pl.BlockSpec((1, gw), lambda i: (0, 0)),
            pl.BlockSpec((1, gw), lambda i: (0, 0)),
        ],
        out_specs=pl.BlockSpec((m, gw), lambda i: (0, 0)),
        out_shape=jax.ShapeDtypeStruct((m, gw), BF16),
        compiler_params=_params("arbitrary"),
        name="gmlp_first_row",
    )(uv, uv, w_row, b_row)


def _dsa_kernel(q_ref, qi_ref, wit_ref, k_ref, kidx_ref, vt_ref, o_ref,
                sc_ref, m_ref, acc_ref, s_ref, p_ref, *, tq, tk, topk, wscale):
    i = pl.program_id(1)
    n_chunks = ((i + 1) * tq + tk - 1) // tk
    qpos = i * tq + lax.broadcasted_iota(jnp.int32, (1, tq), 1)
    vrows = vt_ref.shape[1] // N_HEADS

    def idx_body(c, carry):
        k0 = pl.multiple_of(c * tk, tk)
        kc = kidx_ref[0, pl.ds(k0, tk), :]
        acc = jnp.zeros((tk, tq), F32)
        for h in range(IDX_HEADS):
            s = lax.dot_general(kc, qi_ref[0, :, h * IDX_DIM:(h + 1) * IDX_DIM], NT_DIMS,
                                preferred_element_type=F32)
            acc = acc + jnp.maximum(s, 0.0) * (wit_ref[0, h:h + 1, :] * wscale)
        kpos = k0 + lax.broadcasted_iota(jnp.int32, (tk, 1), 0)
        sc_ref[pl.ds(k0, tk), :] = jnp.where(kpos <= qpos, acc, -jnp.inf)
        return carry

    lax.fori_loop(0, n_chunks, idx_body, 0)

    def count_ge(cand):
        cand_f = _key_to_float(cand)

        def body(c, cnt):
            k0 = pl.multiple_of(c * tk, tk)
            hit = jnp.where(sc_ref[pl.ds(k0, tk), :] >= cand_f, 1.0, 0.0)
            return cnt + jnp.sum(hit.reshape(tk // SUBLANES, SUBLANES, tq), axis=0)

        cnt = lax.fori_loop(0, n_chunks, body, jnp.zeros((SUBLANES, tq), F32))
        return jnp.sum(cnt, axis=0, keepdims=True)

    thr = _key_to_float(_kth_largest_key(count_ge, topk, qpos))

    m_ref[...] = jnp.full(m_ref.shape, NEG, F32)
    acc_ref[...] = jnp.zeros(acc_ref.shape, F32)
    c_exp = HEAD_DIM ** -0.5 * LOG2_E

    def att_body(c, carry):
        k0 = pl.multiple_of(c * tk, tk)
        bias = jnp.where(sc_ref[pl.ds(k0, tk), :] >= thr, 0.0, NEG)
        m_old, m_new = [], []
        for h in range(N_HEADS):
            hs = slice(h * HEAD_DIM, (h + 1) * HEAD_DIM)
            s = lax.dot_general(k_ref[0, pl.ds(k0, tk), hs], q_ref[0, :, hs], NT_DIMS,
                                preferred_element_type=F32) * c_exp + bias
            s_ref[h] = s
            m_old.append(m_ref[h])
            m_new.append(jnp.maximum(m_old[h], jnp.max(s, axis=0, keepdims=True)))
            m_ref[h] = m_new[h]
        for h in range(N_HEADS):
            p_ref[h] = jnp.exp2(s_ref[h] - m_new[h][0:1, :]).astype(BF16)
        for h in range(N_HEADS):
            pv = jnp.dot(vt_ref[0, h * vrows:(h + 1) * vrows, pl.ds(k0, tk)], p_ref[h],
                         preferred_element_type=F32)
            acc_ref[h] = jnp.exp2(m_old[h] - m_new[h])[0:1, :] * acc_ref[h] + pv
        return carry

    lax.fori_loop(0, n_chunks, att_body, 0)

    for h in range(N_HEADS):
        o = acc_ref[h, 0:HEAD_DIM, :] / acc_ref[h, HEAD_DIM:HEAD_DIM + 1, :]
        o_ref[0, :, h * HEAD_DIM:(h + 1) * HEAD_DIM] = o.T.astype(o_ref.dtype)


def _dsa_prompt(q, qi, wi_t, k, kidx, v, tq=256, tk=256):
    b, t, aw = q.shape
    ones = jnp.ones((b, N_HEADS, ONES_ROWS, t), v.dtype)
    v_t = jnp.concatenate([jnp.transpose(v.reshape(b, t, N_HEADS, HEAD_DIM), (0, 2, 3, 1)), ones], axis=2)
    v_t = v_t.reshape(b, N_HEADS * (HEAD_DIM + ONES_ROWS), t)
    tq = _tile(t, tq)
    tk = _tile(tq, tk)
    topk = min(TOPK_MAX, t // 4)
    once = pl.Buffered(1)
    return pl.pallas_call(
        functools.partial(_dsa_kernel, tq=tq, tk=tk, topk=topk,
                          wscale=IDX_HEADS ** -0.5 * IDX_DIM ** -0.5),
        grid=(b, t // tq),
        in_specs=[
            pl.BlockSpec((1, tq, aw), lambda bi, i: (bi, i, 0)),
            pl.BlockSpec((1, tq, IDX_HEADS * IDX_DIM), lambda bi, i: (bi, i, 0)),
            pl.BlockSpec((1, IDX_HEADS, tq), lambda bi, i: (bi, 0, i)),
            pl.BlockSpec((1, t, aw), lambda bi, i: (bi, 0, 0), pipeline_mode=once),
            pl.BlockSpec((1, t, IDX_DIM), lambda bi, i: (bi, 0, 0), pipeline_mode=once),
            pl.BlockSpec((1, v_t.shape[1], t), lambda bi, i: (bi, 0, 0), pipeline_mode=once),
        ],
        out_specs=pl.BlockSpec((1, tq, aw), lambda bi, i: (bi, i, 0)),
        out_shape=jax.ShapeDtypeStruct((b, t, aw), BF16),
        scratch_shapes=[
            pltpu.VMEM((t, tq), F32),
            pltpu.VMEM((N_HEADS, SUBLANES, tq), F32),
            pltpu.VMEM((N_HEADS, HEAD_DIM + ONES_ROWS, tq), F32),
            pltpu.VMEM((N_HEADS, tk, tq), F32),
            pltpu.VMEM((N_HEADS, tk, tq), BF16),
        ],
        compiler_params=_params("parallel", "arbitrary"),
        name="dsa_prompt",
    )(q, qi, wi_t, k, kidx, v_t)


def _memattn_kernel(qm_ref, mk_ref, mv_ref, o_ref):
    scale = MEM_HEAD_DIM ** -0.5
    for h in range(MEM_HEADS):
        hs = slice(h * MEM_HEAD_DIM, (h + 1) * MEM_HEAD_DIM)
        s = lax.dot_general(qm_ref[0, :, hs], mk_ref[0, :, hs].astype(BF16), NT_DIMS,
                            preferred_element_type=F32) * scale
        e = jnp.exp(s - jnp.max(s, axis=-1, keepdims=True))
        p = e / jnp.sum(e, axis=-1, keepdims=True)
        o = jnp.dot(p.astype(BF16), mv_ref[0, :, hs].astype(BF16), preferred_element_type=F32)
        o_ref[0, :, hs] = o.astype(o_ref.dtype)


def _mem_attend(qm, mk, mv):
    b, t, mw = qm.shape
    n_mem = mk.shape[1]
    tm = _tile(t, 512)
    return pl.pallas_call(
        _memattn_kernel,
        grid=(b, t // tm),
        in_specs=[
            pl.BlockSpec((1, tm, mw), lambda bi, i: (bi, i, 0)),
            pl.BlockSpec((1, n_mem, mw), lambda bi, i: (bi, 0, 0)),
            pl.BlockSpec((1, n_mem, mw), lambda bi, i: (bi, 0, 0)),
        ],
        out_specs=pl.BlockSpec((1, tm, mw), lambda bi, i: (bi, i, 0)),
        out_shape=jax.ShapeDtypeStruct((b, t, mw), BF16),
        compiler_params=_params("parallel", "parallel"),
        name="mem_attend",
    )(qm, mk, mv)


def _strict_triangle(n, lower):
    r = lax.broadcasted_iota(jnp.int32, (n, n), 0)
    c = lax.broadcasted_iota(jnp.int32, (n, n), 1)
    return jnp.where((c < r) if lower else (r < c), 1.0, 0.0).astype(BF16)


def _index_order_rank(flag):
    rows = flag.shape[0]
    within = jnp.dot(flag.astype(BF16), _strict_triangle(LANES, False), preferred_element_type=F32)
    per_row = jnp.broadcast_to(jnp.sum(flag, axis=1, keepdims=True), flag.shape)
    before = jnp.dot(_strict_triangle(rows, True), per_row.astype(BF16), preferred_element_type=F32)
    return before + within


def _sindex_kernel(pg_ref, qi_ref, wi_ref, knew_ref, cache_ref, idx_ref, kbuf, sem, sc_ref, pos_ref,
                   *, n_pages, topk, wscale, group):
    b = pl.program_id(0)
    nb = pl.num_programs(0)
    slot = lax.rem(b, 2)

    def page_copy(bb, p, sl):
        return pltpu.make_async_copy(cache_ref.at[pg_ref[bb, p]], kbuf.at[sl, p], sem.at[sl])

    def start_all(bb, sl):
        lax.fori_loop(0, n_pages, lambda p, c: (page_copy(bb, p, sl).start(), c)[1], 0)

    @pl.when(b == 0)
    def _():
        start_all(b, slot)

    @pl.when(b + 1 < nb)
    def _():
        start_all(b + 1, 1 - slot)

    lax.fori_loop(0, n_pages, lambda p, c: (page_copy(b, p, slot).wait(), c)[1], 0)

    qi = qi_ref[0]
    w = wi_ref[0] * wscale
    for g in range(n_pages // group):
        kc = kbuf[slot, g * group:(g + 1) * group].reshape(group * PAGE_SIZE, IDX_DIM).astype(BF16)
        s = lax.dot_general(qi, kc, NT_DIMS, preferred_element_type=F32)
        r = jnp.sum(jnp.maximum(s, 0.0) * w, axis=0, keepdims=True)
        for j in range(group):
            sc_ref[g * group + j:g * group + j + 1, :] = r[:, j * PAGE_SIZE:(j + 1) * PAGE_SIZE]
    s_new = jnp.sum(qi.astype(F32) * knew_ref[0].astype(BF16).astype(F32), axis=1, keepdims=True)
    s_new = jnp.sum(jnp.maximum(s_new, 0.0) * w, axis=0, keepdims=True)
    pad_rows = sc_ref.shape[0] - n_pages
    first = (lax.broadcasted_iota(jnp.int32, (pad_rows, LANES), 0) == 0) & (
        lax.broadcasted_iota(jnp.int32, (pad_rows, LANES), 1) == 0)
    sc_ref[n_pages:, :] = jnp.where(first, s_new, -jnp.inf)

    s = sc_ref[...]

    def count(hit):
        return jnp.sum(jnp.sum(jnp.where(hit, 1.0, 0.0), axis=1, keepdims=True), axis=0, keepdims=True)

    thr = _key_to_float(_kth_largest_key(lambda cand: count(s >= _key_to_float(cand)), topk,
                                         jnp.zeros((1, 1), jnp.int32)))
    above = s > thr
    tied = jnp.where(s == thr, 1.0, 0.0)
    keep_tied = (tied > 0.5) & (_index_order_rank(tied) < topk - count(above))
    chosen = jnp.where(above | keep_tied, 1.0, 0.0)
    pos_ref[...] = jnp.where(chosen > 0.5, _index_order_rank(chosen), -1.0)

    slot_id = lax.broadcasted_iota(jnp.int32, (topk, LANES), 0).astype(F32)
    lane = lax.broadcasted_iota(jnp.int32, (1, LANES), 1)

    def place(p, acc):
        key_pos = (p * PAGE_SIZE + lane).astype(F32)
        return acc + jnp.where(pos_ref[pl.ds(p, 1), :] == slot_id, key_pos, 0.0)

    acc = lax.fori_loop(0, n_pages + 1, place, jnp.zeros((topk, LANES), F32))
    idx_ref[0] = jnp.sum(acc, axis=1, keepdims=True).astype(jnp.int32)


def _sample_indices(qi, wi, kidx_new, cache_kidx, pages):
    db, n_pages = pages.shape
    past = n_pages * PAGE_SIZE
    topk = min(TOPK_MAX, (past + 1) // 4)
    assert topk % SUBLANES == 0
    group = 16 if n_pages % 16 == 0 else 1
    rows = (n_pages + 1 + SUBLANES - 1) // SUBLANES * SUBLANES
    grid_spec = pltpu.PrefetchScalarGridSpec(
        num_scalar_prefetch=1,
        grid=(db,),
        in_specs=[
            pl.BlockSpec((1, IDX_HEADS, IDX_DIM), lambda b, pg: (b, 0, 0)),
            pl.BlockSpec((1, IDX_HEADS, 1), lambda b, pg: (b, 0, 0)),
            pl.BlockSpec((1, 1, IDX_DIM), lambda b, pg: (b, 0, 0)),
            pl.BlockSpec(memory_space=pl.ANY),
        ],
        out_specs=pl.BlockSpec((1, topk, 1), lambda b, pg: (b, 0, 0)),
        scratch_shapes=[
            pltpu.VMEM((2, n_pages, PAGE_SIZE, IDX_DIM), F32),
            pltpu.SemaphoreType.DMA((2,)),
            pltpu.VMEM((rows, LANES), F32),
            pltpu.VMEM((rows, LANES), F32),
        ],
    )
    idx = pl.pallas_call(
        functools.partial(_sindex_kernel, n_pages=n_pages, topk=topk,
                          wscale=IDX_HEADS ** -0.5 * IDX_DIM ** -0.5, group=group),
        grid_spec=grid_spec,
        out_shape=jax.ShapeDtypeStruct((db, topk, 1), jnp.int32),
        compiler_params=_params("arbitrary"),
        name="sample_indices",
    )(pages, qi.reshape(db, IDX_HEADS, IDX_DIM), wi.reshape(db, IDX_HEADS, 1),
      kidx_new.reshape(db, 1, IDX_DIM), cache_kidx.reshape(-1, PAGE_SIZE, IDX_DIM))
    return idx.reshape(db, topk)


def _sgather_kernel(idx_ref, pg_ref, q_ref, knew_ref, vnew_ref, ck_ref, cv_ref, o_ref, kbuf, vbuf, sem,
                    *, past, topk):
    b = pl.program_id(0)
    nb = pl.num_programs(0)
    slot = lax.rem(b, 2)

    def start_all(bb, sl):
        def body(j, c):
            i = idx_ref[bb, j]
            ip = jnp.minimum(i, past - 1)
            page = pg_ref[bb, ip // PAGE_SIZE]
            off = lax.rem(ip, PAGE_SIZE)

            @pl.when(i >= past)
            def _():
                pltpu.make_async_copy(knew_ref.at[bb], kbuf.at[sl, j], sem.at[0, sl]).start()
                pltpu.make_async_copy(vnew_ref.at[bb], vbuf.at[sl, j], sem.at[1, sl]).start()

            @pl.when(i < past)
            def _():
                pltpu.make_async_copy(ck_ref.at[page, off], kbuf.at[sl, j], sem.at[0, sl]).start()
                pltpu.make_async_copy(cv_ref.at[page, off], vbuf.at[sl, j], sem.at[1, sl]).start()

            return c

        lax.fori_loop(0, topk, body, 0)

    @pl.when(b == 0)
    def _():
        start_all(b, slot)

    @pl.when(b + 1 < nb)
    def _():
        start_all(b + 1, 1 - slot)

    def wait_body(j, c):
        pltpu.make_async_copy(knew_ref.at[b], kbuf.at[slot, j], sem.at[0, slot]).wait()
        pltpu.make_async_copy(vnew_ref.at[b], vbuf.at[slot, j], sem.at[1, slot]).wait()
        return c

    lax.fori_loop(0, topk, wait_body, 0)

    k = kbuf[slot]
    v = vbuf[slot]
    s = jnp.sum(k * q_ref[...], axis=2, keepdims=True) * HEAD_DIM ** -0.5
    e = jnp.exp(s - jnp.max(s, axis=0, keepdims=True))
    p = e / jnp.sum(e, axis=0, keepdims=True)
    o_ref[...] = jnp.sum(p * v, axis=0, keepdims=True).astype(o_ref.dtype)


def _sample_attend(q, k_new, v_new, idx, cache_k, cache_v, pages):
    db, n_pages = pages.shape
    topk = idx.shape[1]
    tile = (N_HEADS, HEAD_DIM)
    grid_spec = pltpu.PrefetchScalarGridSpec(
        num_scalar_prefetch=2,
        grid=(db,),
        in_specs=[
            pl.BlockSpec((1,) + tile, lambda b, ix, pg: (b, 0, 0)),
            pl.BlockSpec(memory_space=pl.ANY),
            pl.BlockSpec(memory_space=pl.ANY),
            pl.BlockSpec(memory_space=pl.ANY),
            pl.BlockSpec(memory_space=pl.ANY),
        ],
        out_specs=pl.BlockSpec((1,) + tile, lambda b, ix, pg: (b, 0, 0)),
        scratch_shapes=[
            pltpu.VMEM((2, topk) + tile, F32),
            pltpu.VMEM((2, topk) + tile, F32),
            pltpu.SemaphoreType.DMA((2, 2)),
        ],
    )
    out = pl.pallas_call(
        functools.partial(_sgather_kernel, past=n_pages * PAGE_SIZE, topk=topk),
        grid_spec=grid_spec,
        out_shape=jax.ShapeDtypeStruct((db,) + tile, BF16),
        compiler_params=_params("arbitrary"),
        name="sample_attend",
    )(idx, pages, q.reshape((db,) + tile), k_new.reshape((db,) + tile), v_new.reshape((db,) + tile),
      cache_k.reshape((-1, PAGE_SIZE) + tile), cache_v.reshape((-1, PAGE_SIZE) + tile))
    return out.reshape(db, N_HEADS * HEAD_DIM)


def _merge_kernel(x_ref, g_ref, ya_ref, yb_ref, ym_ref, woa_ref, wob_ref, wom_ref, wout_ref,
                  gpm_ref, gpf_ref, x1_ref, h2_ref):
    d = x_ref.shape[1]
    merged = (g_ref[:, 0:d].astype(F32) * jnp.dot(ya_ref[...], woa_ref[...], preferred_element_type=F32)
              + g_ref[:, d:2 * d].astype(F32) * jnp.dot(yb_ref[...], wob_ref[...], preferred_element_type=F32)
              + g_ref[:, 2 * d:3 * d].astype(F32) * jnp.dot(ym_ref[...], wom_ref[...], preferred_element_type=F32))
    o = jnp.dot(merged.astype(BF16), wout_ref[...], preferred_element_type=F32)
    x1 = x_ref[...] + _rmsnorm_rows(o, gpm_ref[...])
    x1_ref[...] = x1
    h2_ref[...] = _rmsnorm_rows(x1, gpf_ref[...]).astype(h2_ref.dtype)


def _merge(x, gates, y_a, y_b, y_m, w_oa, w_ob, w_om, w_out, g_post_mix, g_pre_ffn):
    m, d = x.shape
    tm = _tile(m, 256)
    once = pl.Buffered(1)

    def rows(width):
        return pl.BlockSpec((tm, width), lambda i: (i, 0))

    def whole(a):
        return pl.BlockSpec(a.shape, lambda i: (0, 0), pipeline_mode=once)

    return pl.pallas_call(
        _merge_kernel,
        grid=(m // tm,),
        in_specs=[rows(d), rows(N_BRANCH * d), rows(y_a.shape[1]), rows(y_b.shape[1]), rows(y_m.shape[1]),
                  whole(w_oa), whole(w_ob), whole(w_om), whole(w_out),
                  pl.BlockSpec((1, d), lambda i: (0, 0)), pl.BlockSpec((1, d), lambda i: (0, 0))],
        out_specs=[rows(d), rows(d)],
        out_shape=[jax.ShapeDtypeStruct((m, d), F32), jax.ShapeDtypeStruct((m, d), BF16)],
        compiler_params=_params("parallel"),
        name="merge",
    )(x, gates, y_a, y_b, y_m, w_oa, w_ob, w_om, w_out, g_post_mix.reshape(1, d), g_pre_ffn.reshape(1, d))


def _ffn_kernel(h_ref, wg_ref, wu_ref, wd_ref, x1_ref, gpf_ref, o_ref, acc_ref):
    j = pl.program_id(1)

    @pl.when(j == 0)
    def _():
        acc_ref[...] = jnp.zeros(acc_ref.shape, F32)

    h = h_ref[...]
    a = jnp.dot(h, wg_ref[...], preferred_element_type=F32)
    c = jnp.dot(h, wu_ref[...], preferred_element_type=F32)
    act = (jax.nn.silu(a) * c).astype(BF16)
    acc_ref[...] += jnp.dot(act, wd_ref[...], preferred_element_type=F32)

    @pl.when(j == pl.num_programs(1) - 1)
    def _():
        o_ref[...] = x1_ref[...] + _rmsnorm_rows(acc_ref[...], gpf_ref[...])


def _ffn(h2, x1, w_gu, w_down, g_post_ffn):
    m, d = h2.shape
    d_ff = w_down.shape[0]
    tm = _tile(m, 512)
    tf = _tile(d_ff, 512)
    nf = d_ff // tf
    return pl.pallas_call(
        _ffn_kernel,
        grid=(m // tm, nf),
        in_specs=[
            pl.BlockSpec((tm, d), lambda i, j: (i, 0)),
            pl.BlockSpec((d, tf), lambda i, j: (0, j)),
            pl.BlockSpec((d, tf), lambda i, j: (0, j + nf)),
            pl.BlockSpec((tf, d), lambda i, j: (j, 0)),
            pl.BlockSpec((tm, d), lambda i, j: (i, 0)),
            pl.BlockSpec((1, d), lambda i, j: (0, 0)),
        ],
        out_specs=pl.BlockSpec((tm, d), lambda i, j: (i, 0)),
        out_shape=jax.ShapeDtypeStruct((m, d), F32),
        scratch_shapes=[pltpu.VMEM((tm, d), F32)],
        compiler_params=_params("parallel", "arbitrary"),
        name="ffn",
    )(h2, w_gu, w_gu, w_down, x1, g_post_ffn.reshape(1, d))


def _split_w_in(w_in, d):
    gw = d // 2
    aw = N_HEADS * HEAD_DIM
    iw = IDX_HEADS * IDX_DIM
    mw = MEM_HEADS * MEM_HEAD_DIM
    sizes = (2 * gw, aw, aw, aw, iw, IDX_DIM + IDX_HEADS, mw, N_BRANCH * d)
    parts, start = [], 0
    for s in sizes:
        parts.append(w_in[:, start:start + s].astype(BF16))
        start += s
    w_uv, w_q, w_k, w_v, w_qi, w_kw, w_qm, w_g = parts
    w_kw = jnp.pad(w_kw, ((0, 0), (0, 2 * LANES - w_kw.shape[1])))
    return w_uv, w_q, w_k, w_v, w_qi, w_kw, w_qm, w_g


def _project_common(x2d, g_pre_mix, ws, ln_g, ln_b, uv_dtype, q_dtype):
    w_uv, w_q, w_k, w_v, w_qi, w_kw, w_qm, w_g = ws
    h = _rmsnorm(x2d, g_pre_mix, BF16)
    uv = _project_uv(h, w_uv, ln_g, ln_b, uv_dtype)
    (q,) = _matmul(h, w_q, (q_dtype,), name="proj_q")
    k, k_bf = _matmul(h, w_k, (F32, BF16), name="proj_k")
    v, v_bf = _matmul(h, w_v, (F32, BF16), name="proj_v")
    (qi,) = _matmul(h, w_qi, (BF16,), name="proj_qi")
    (kw,) = _matmul(h, w_kw, (F32,), name="proj_kidx")
    (qm,) = _matmul(h, w_qm, (BF16,), name="proj_qm")
    (gates,) = _matmul(h, w_g, (BF16,), act="sigmoid", name="proj_gates")
    kidx = kw[:, :IDX_DIM]
    wi = kw[:, IDX_DIM:IDX_DIM + IDX_HEADS]
    return uv, q, k, k_bf, v, v_bf, qi, kidx, wi, qm, gates


def kernel(x_prompt, x_sample, mem_prompt, cache_k, cache_v, cache_kidx, cache_mem_k, cache_mem_v, page_table, g_pre_mix, g_post_mix, g_pre_ffn, g_post_ffn, g_mem, ln_v_g, ln_v_b, w_in, w_s, b_s, w_oa, w_ob, w_om, w_out, w_mem_kv, w_gu, w_down):
    depth = w_in.shape[0]
    b, t, d = x_prompt.shape
    db, dt, _ = x_sample.shape
    assert dt == 1, "sample path handles one new token per sequence"
    n_mem = mem_prompt.shape[1]
    aw = N_HEADS * HEAD_DIM
    mw = MEM_HEADS * MEM_HEAD_DIM
    gw = d // 2
    past = page_table.shape[1] * PAGE_SIZE

    yp = x_prompt.reshape(b * t, d)
    ys = x_sample.reshape(db * dt, d)
    outs = [[] for _ in range(9)]
    for l in range(depth):
        ws = _split_w_in(w_in[l], d)
        woa, wob, wom, wout = (w.astype(BF16) for w in (w_oa[l], w_ob[l], w_om[l], w_out[l]))
        wgu, wdown, wmem = w_gu[l].astype(BF16), w_down[l].astype(BF16), w_mem_kv[l].astype(BF16)

        uv, q, k, k_bf, v, v_bf, qi, kidx, wi, qm, gates = _project_common(
            yp, g_pre_mix[l], ws, ln_v_g[l], ln_v_b[l], BF16, BF16)
        y_a = _gmlp_spatial(uv, w_s[l], b_s[l])
        y_b = _dsa_prompt(
            q.reshape(b, t, aw), qi.reshape(b, t, -1), jnp.swapaxes(wi.reshape(b, t, IDX_HEADS), 1, 2),
            k_bf.reshape(b, t, aw), kidx.astype(BF16).reshape(b, t, IDX_DIM),
            v_bf.reshape(b, t, aw)).reshape(b * t, aw)
        hm = _rmsnorm(mem_prompt.reshape(b * n_mem, d), g_mem[l], BF16)
        (mkv,) = _matmul(hm, wmem, (F32,), name="proj_mem_kv")
        mk = mkv[:, :mw].reshape(b, n_mem, mw)
        mv = mkv[:, mw:].reshape(b, n_mem, mw)
        y_m = _mem_attend(qm.reshape(b, t, mw), mk, mv).reshape(b * t, mw)
        x1, h2 = _merge(yp, gates, y_a, y_b, y_m, woa, wob, wom, wout, g_post_mix[l], g_pre_ffn[l])
        yp = _ffn(h2, x1, wgu, wdown, g_post_ffn[l])
        outs[0].append(k.reshape(b, t, N_HEADS, HEAD_DIM))
        outs[1].append(v.reshape(b, t, N_HEADS, HEAD_DIM))
        outs[2].append(kidx.reshape(b, t, IDX_DIM))
        outs[3].append(mk.reshape(b, n_mem, MEM_HEADS, MEM_HEAD_DIM))
        outs[4].append(mv.reshape(b, n_mem, MEM_HEADS, MEM_HEAD_DIM))

        uv, q, k, k_bf, v, v_bf, qi, kidx, wi, qm, gates = _project_common(
            ys, g_pre_mix[l], ws, ln_v_g[l], ln_v_b[l], F32, F32)
        y_a = _gmlp_first_row(uv, w_s[l], b_s[l])
        pages = page_table + l * cache_k.shape[1]
        idx = _sample_indices(qi, wi, kidx, cache_kidx, pages)
        y_b = _sample_attend(q, k, v, idx, cache_k, cache_v, pages)
        qm_pad = jnp.pad(qm.reshape(db, 1, mw), ((0, 0), (0, 2 * SUBLANES - 1), (0, 0)))
        y_m = _mem_attend(qm_pad, cache_mem_k[l].reshape(db, n_mem, mw),
                          cache_mem_v[l].reshape(db, n_mem, mw))[:, 0, :]
        x1, h2 = _merge(ys, gates, y_a, y_b, y_m, woa, wob, wom, wout, g_post_mix[l], g_pre_ffn[l])
        ys = _ffn(h2, x1, wgu, wdown, g_post_ffn[l])
        outs[5].append(k.reshape(db, dt, N_HEADS, HEAD_DIM))
        outs[6].append(v.reshape(db, dt, N_HEADS, HEAD_DIM))
        outs[7].append(kidx.reshape(db, dt, IDX_DIM))
        outs[8].append(uv[:, gw:].reshape(db, dt, gw))

    return (yp.reshape(b, t, d), ys.reshape(db, dt, d)) + tuple(jnp.stack(o) for o in outs)
```

```python
import functools

import jax
import jax.numpy as jnp
from jax import lax
from jax.experimental import pallas as pl
from jax.experimental.pallas import tpu as pltpu

EPS = 1e-6
CHUNK = 128
GM_GROUPS = 8
N_HEADS = 8
HEAD_DIM = 128
IDX_HEADS = 16
IDX_DIM = 128
TOPK_MAX = 256
MEM_HEADS = 4
MEM_HEAD_DIM = 128
N_BRANCH = 3
PAGE_SIZE = 128

LANES = 128
SUBLANES = 8
VMEM_LIMIT_BYTES = 56 * 2**20
INT_MIN = -2**31
KEY_LOWEST = INT_MIN + 0x800000
KEY_INF = 0x7F800000
NEG = -1e30
LOG2_E = 1.4426950408889634
DSA_LOG2_SCALE = HEAD_DIM ** -0.5 * LOG2_E
ONES_ROWS = 16
COUNT_ROWS = 64

F32 = jnp.float32
BF16 = jnp.bfloat16
NT_DIMS = (((1,), (1,)), ((), ()))


def _params(*sem):
    return pltpu.CompilerParams(dimension_semantics=sem, vmem_limit_bytes=VMEM_LIMIT_BYTES)


def _tile(n, pref):
    if n <= pref:
        return n
    t = pref
    while n % t:
        t //= 2
    return t


def _key_to_float(key):
    key = jnp.clip(key, KEY_LOWEST, KEY_INF)
    return pltpu.bitcast(key ^ ((key >> 31) & 0x7FFFFFFF), F32)


def _kth_largest_key(count_ge, topk, like):
    base = jnp.where(count_ge(jnp.zeros_like(like)) >= topk, 0, INT_MIN).astype(jnp.int32)

    def bit_body(b, t):
        cand = t | jnp.left_shift(jnp.int32(1), 30 - b)
        return jnp.where(count_ge(cand) >= topk, cand, t)

    return lax.fori_loop(0, 31, bit_body, base)


def _rmsnorm_rows(x, g):
    return x * lax.rsqrt(jnp.mean(x * x, axis=-1, keepdims=True) + EPS) * g


def _rmsnorm_kernel(x_ref, g_ref, o_ref):
    o_ref[...] = _rmsnorm_rows(x_ref[...], g_ref[...]).astype(o_ref.dtype)


def _rmsnorm(x, g, out_dtype):
    m, d = x.shape
    tm = _tile(m, 512)
    return pl.pallas_call(
        _rmsnorm_kernel,
        grid=(m // tm,),
        in_specs=[pl.BlockSpec((tm, d), lambda i: (i, 0)), pl.BlockSpec((1, d), lambda i: (0, 0))],
        out_specs=pl.BlockSpec((tm, d), lambda i: (i, 0)),
        out_shape=jax.ShapeDtypeStruct((m, d), out_dtype),
        compiler_params=_params("parallel"),
        name="rmsnorm",
    )(x, g.reshape(1, d))


def _mm_kernel(x_ref, w_ref, *o_refs, act, scale):
    acc = lax.dot_general(x_ref[...], w_ref[...], NT_DIMS, preferred_element_type=F32)
    if act == "sigmoid":
        acc = jax.nn.sigmoid(acc)
    if scale is not None:
        acc = acc * scale
    for o_ref in o_refs:
        o_ref[...] = acc.astype(o_ref.dtype)


def _matmul(x, w_t, out_dtypes, act=None, scale=None, name="matmul"):
    m, k = x.shape
    n = w_t.shape[0]
    tm = _tile(m, 1024)
    tn = _tile(n, 512)
    outs = pl.pallas_call(
        functools.partial(_mm_kernel, act=act, scale=scale),
        grid=(m // tm, n // tn),
        in_specs=[pl.BlockSpec((tm, k), lambda i, j: (i, 0)), pl.BlockSpec((tn, k), lambda i, j: (j, 0))],
        out_specs=[pl.BlockSpec((tm, tn), lambda i, j: (i, j)) for _ in out_dtypes],
        out_shape=[jax.ShapeDtypeStruct((m, n), dt) for dt in out_dtypes],
        compiler_params=_params("parallel", "parallel"),
        name=name,
    )(x, w_t)
    return outs


def _uv_kernel(x_ref, w_ref, lng_ref, lnb_ref, o_ref):
    y = jax.nn.gelu(lax.dot_general(x_ref[...], w_ref[...], NT_DIMS, preferred_element_type=F32))

    @pl.when(pl.program_id(1) == 0)
    def _():
        o_ref[...] = y.astype(o_ref.dtype)

    @pl.when(pl.program_id(1) == 1)
    def _():
        yc = y - jnp.mean(y, axis=-1, keepdims=True)
        yn = yc * lax.rsqrt(jnp.mean(yc * yc, axis=-1, keepdims=True) + EPS)
        o_ref[...] = (yn * lng_ref[...] + lnb_ref[...]).astype(o_ref.dtype)


def _project_uv(h, w_uv_t, ln_g, ln_b, out_dtype):
    m, k = h.shape
    gw = w_uv_t.shape[0] // 2
    tm = _tile(m, 512)
    return pl.pallas_call(
        _uv_kernel,
        grid=(m // tm, 2),
        in_specs=[
            pl.BlockSpec((tm, k), lambda i, j: (i, 0)),
            pl.BlockSpec((gw, k), lambda i, j: (j, 0)),
            pl.BlockSpec((1, gw), lambda i, j: (0, 0)),
            pl.BlockSpec((1, gw), lambda i, j: (0, 0)),
        ],
        out_specs=pl.BlockSpec((tm, gw), lambda i, j: (i, j)),
        out_shape=jax.ShapeDtypeStruct((m, 2 * gw), out_dtype),
        compiler_params=_params("parallel", "arbitrary"),
        name="project_uv",
    )(h, w_uv_t, ln_g.reshape(1, gw), ln_b.reshape(1, gw))


def _gmlp_kernel(u_ref, v_ref, ws_ref, bt_ref, o_ref, *, n_chunks):
    c = CHUNK
    gd = u_ref.shape[1] // GM_GROUPS
    causal = lax.broadcasted_iota(jnp.int32, (c, c), 1) <= lax.broadcasted_iota(jnp.int32, (c, c), 0)
    for g in range(GM_GROUPS):
        w = jnp.where(causal, ws_ref[g], 0.0).astype(v_ref.dtype)
        bcol = bt_ref[:, g:g + 1]
        for ci in range(n_chunks):
            rows = slice(ci * c, (ci + 1) * c)
            cols = slice(g * gd, (g + 1) * gd)
            mixed = jnp.dot(w, v_ref[rows, cols], preferred_element_type=F32) + bcol
            o_ref[rows, cols] = (u_ref[rows, cols].astype(F32) * mixed).astype(o_ref.dtype)


def _gmlp_spatial(uv, w_s, b_s):
    m = uv.shape[0]
    gw = uv.shape[1] // 2
    tm = _tile(m, 4 * CHUNK)
    return pl.pallas_call(
        functools.partial(_gmlp_kernel, n_chunks=tm // CHUNK),
        grid=(m // tm,),
        in_specs=[
            pl.BlockSpec((tm, gw), lambda i: (i, 0)),
            pl.BlockSpec((tm, gw), lambda i: (i, 1)),
            pl.BlockSpec((GM_GROUPS, CHUNK, CHUNK), lambda i: (0, 0, 0)),
            pl.BlockSpec((CHUNK, GM_GROUPS), lambda i: (0, 0)),
        ],
        out_specs=pl.BlockSpec((tm, gw), lambda i: (i, 0)),
        out_shape=jax.ShapeDtypeStruct((m, gw), BF16),
        compiler_params=_params("parallel"),
        name="gmlp_spatial",
    )(uv, uv, w_s, b_s.T)


def _gmlp_first_kernel(u_ref, v_ref, w_ref, b_ref, o_ref):
    o_ref[...] = (u_ref[...] * (w_ref[...] * v_ref[...] + b_ref[...])).astype(o_ref.dtype)


def _gmlp_first_row(uv, w_s, b_s):
    m = uv.shape[0]
    gw = uv.shape[1] // 2
    gd = gw // GM_GROUPS
    w_row = jnp.repeat(w_s[:, 0, 0], gd).reshape(1, gw)
    b_row = jnp.repeat(b_s[:, 0], gd).reshape(1, gw)
    return pl.pallas_call(
        _gmlp_first_kernel,
        grid=(1,),
        in_specs=[
            pl.BlockSpec((m, gw), lambda i: (0, 0)),
            pl.BlockSpec((m, gw), lambda i: (0, 1)),
            pl.BlockSpec((1, gw), lambda i: (0, 0)),
            pl.BlockSpec((1, gw), lambda i: (0, 0)),
        ],
        out_specs=pl.BlockSpec((m, gw), lambda i: (0, 0)),
        out_shape=jax.ShapeDtypeStruct((m, gw), BF16),
        compiler_params=_params("arbitrary"),
        name="gmlp_first_row",
    )(uv, uv, w_row, b_row)


def _dsa_kernel(q_ref, qi_ref, wit_ref, k_ref, kidx_ref, vt_ref, o_ref,
                sc_ref, m_ref, acc_ref, s_ref, p_ref, *, tq, tk, topk, wscale):
    i = pl.program_id(1)
    n_chunks = ((i + 1) * tq + tk - 1) // tk
    qpos = i * tq + lax.broadcasted_iota(jnp.int32, (1, tq), 1)
    vrows = vt_ref.shape[1] // N_HEADS

    def idx_body(c, carry):
        k0 = pl.multiple_of(c * tk, tk)
        kc = kidx_ref[0, pl.ds(k0, tk), :]
        acc = jnp.zeros((tk, tq), F32)
        for h in range(IDX_HEADS):
            s = lax.dot_general(kc, qi_ref[0, :, h * IDX_DIM:(h + 1) * IDX_DIM], NT_DIMS,
                                preferred_element_type=F32)
            acc = acc + jnp.maximum(s, 0.0) * (wit_ref[0, h:h + 1, :] * wscale)
        kpos = k0 + lax.broadcasted_iota(jnp.int32, (tk, 1), 0)
        sc_ref[pl.ds(k0, tk), :] = jnp.where(kpos <= qpos, acc, -jnp.inf)
        return carry

    lax.fori_loop(0, n_chunks, idx_body, 0)

    @pl.when(n_chunks % 2 == 1)
    def _():
        sc_ref[pl.ds(pl.multiple_of(n_chunks * tk, tk), tk), :] = jnp.full((tk, tq), -jnp.inf, F32)

    def count_ge(cand):
        cand_f = _key_to_float(cand)

        def body(c, cnt):
            k0 = pl.multiple_of(c * (2 * tk), 2 * tk)
            hit = jnp.where(sc_ref[pl.ds(k0, 2 * tk), :] >= cand_f, 1.0, 0.0)
            return cnt + jnp.sum(hit.reshape(2 * tk // COUNT_ROWS, COUNT_ROWS, tq), axis=0)

        cnt = lax.fori_loop(0, (n_chunks + 1) // 2, body, jnp.zeros((COUNT_ROWS, tq), F32))
        return jnp.sum(cnt, axis=0, keepdims=True)

    thr = _key_to_float(_kth_largest_key(count_ge, topk, qpos))

    m_ref[...] = jnp.full(m_ref.shape, NEG, F32)
    acc_ref[...] = jnp.zeros(acc_ref.shape, F32)

    def att_body(c, carry):
        k0 = pl.multiple_of(c * tk, tk)
        bias = jnp.where(sc_ref[pl.ds(k0, tk), :] >= thr, 0.0, NEG)
        m_old, m_new = [], []
        for h in range(N_HEADS):
            hs = slice(h * HEAD_DIM, (h + 1) * HEAD_DIM)
            s = lax.dot_general(k_ref[0, pl.ds(k0, tk), hs], q_ref[0, :, hs], NT_DIMS,
                                preferred_element_type=F32) + bias
            s_ref[h] = s
            m_old.append(m_ref[h])
            m_new.append(jnp.maximum(m_old[h], jnp.max(s, axis=0, keepdims=True)))
            m_ref[h] = m_new[h]
        for h in range(N_HEADS):
            p_ref[h] = jnp.exp2(s_ref[h] - m_new[h][0:1, :]).astype(BF16)
        for h in range(N_HEADS):
            pv = jnp.dot(vt_ref[0, h * vrows:(h + 1) * vrows, pl.ds(k0, tk)], p_ref[h],
                         preferred_element_type=F32)
            acc_ref[h] = jnp.exp2(m_old[h] - m_new[h])[0:1, :] * acc_ref[h] + pv
        return carry

    lax.fori_loop(0, n_chunks, att_body, 0)

    for h in range(N_HEADS):
        o = acc_ref[h, 0:HEAD_DIM, :] / acc_ref[h, HEAD_DIM:HEAD_DIM + 1, :]
        o_ref[0, :, h * HEAD_DIM:(h + 1) * HEAD_DIM] = o.T.astype(o_ref.dtype)


def _dsa_prompt(q, qi, wi_t, k, kidx, v, tq=256, tk=256):
    b, t, aw = q.shape
    ones = jnp.ones((b, N_HEADS, ONES_ROWS, t), v.dtype)
    v_t = jnp.concatenate([jnp.transpose(v.reshape(b, t, N_HEADS, HEAD_DIM), (0, 2, 3, 1)), ones], axis=2)
    v_t = v_t.reshape(b, N_HEADS * (HEAD_DIM + ONES_ROWS), t)
    tq = _tile(t, tq)
    tk = _tile(tq, tk)
    topk = min(TOPK_MAX, t // 4)
    once = pl.Buffered(1)
    return pl.pallas_call(
        functools.partial(_dsa_kernel, tq=tq, tk=tk, topk=topk,
                          wscale=IDX_HEADS ** -0.5 * IDX_DIM ** -0.5),
        grid=(b, t // tq),
        in_specs=[
            pl.BlockSpec((1, tq, aw), lambda bi, i: (bi, i, 0)),
            pl.BlockSpec((1, tq, IDX_HEADS * IDX_DIM), lambda bi, i: (bi, i, 0)),
            pl.BlockSpec((1, IDX_HEADS, tq), lambda bi, i: (bi, 0, i)),
            pl.BlockSpec((1, t, aw), lambda bi, i: (bi, 0, 0), pipeline_mode=once),
            pl.BlockSpec((1, t, IDX_DIM), lambda bi, i: (bi, 0, 0), pipeline_mode=once),
            pl.BlockSpec((1, v_t.shape[1], t), lambda bi, i: (bi, 0, 0), pipeline_mode=once),
        ],
        out_specs=pl.BlockSpec((1, tq, aw), lambda bi, i: (bi, i, 0)),
        out_shape=jax.ShapeDtypeStruct((b, t, aw), BF16),
        scratch_shapes=[
            pltpu.VMEM((t + tk, tq), F32),
            pltpu.VMEM((N_HEADS, SUBLANES, tq), F32),
            pltpu.VMEM((N_HEADS, HEAD_DIM + ONES_ROWS, tq), F32),
            pltpu.VMEM((N_HEADS, tk, tq), F32),
            pltpu.VMEM((N_HEADS, tk, tq), BF16),
        ],
        compiler_params=_params("parallel", "arbitrary"),
        name="dsa_prompt",
    )(q, qi, wi_t, k, kidx, v_t)


def _memattn_kernel(qm_ref, mk_ref, mv_ref, o_ref):
    scale = MEM_HEAD_DIM ** -0.5
    for h in range(MEM_HEADS):
        hs = slice(h * MEM_HEAD_DIM, (h + 1) * MEM_HEAD_DIM)
        s = lax.dot_general(qm_ref[0, :, hs], mk_ref[0, :, hs].astype(BF16), NT_DIMS,
                            preferred_element_type=F32) * scale
        e = jnp.exp(s - jnp.max(s, axis=-1, keepdims=True))
        p = e / jnp.sum(e, axis=-1, keepdims=True)
        o = jnp.dot(p.astype(BF16), mv_ref[0, :, hs].astype(BF16), preferred_element_type=F32)
        o_ref[0, :, hs] = o.astype(o_ref.dtype)


def _mem_attend(qm, mk, mv):
    b, t, mw = qm.shape
    n_mem = mk.shape[1]
    tm = _tile(t, 512)
    return pl.pallas_call(
        _memattn_kernel,
        grid=(b, t // tm),
        in_specs=[
            pl.BlockSpec((1, tm, mw), lambda bi, i: (bi, i, 0)),
            pl.BlockSpec((1, n_mem, mw), lambda bi, i: (bi, 0, 0)),
            pl.BlockSpec((1, n_mem, mw), lambda bi, i: (bi, 0, 0)),
        ],
        out_specs=pl.BlockSpec((1, tm, mw), lambda bi, i: (bi, i, 0)),
        out_shape=jax.ShapeDtypeStruct((b, t, mw), BF16),
        compiler_params=_params("parallel", "parallel"),
        name="mem_attend",
    )(qm, mk, mv)


def _strict_triangle(n, lower):
    r = lax.broadcasted_iota(jnp.int32, (n, n), 0)
    c = lax.broadcasted_iota(jnp.int32, (n, n), 1)
    return jnp.where((c < r) if lower else (r < c), 1.0, 0.0).astype(BF16)


def _index_order_rank(flag):
    rows = flag.shape[0]
    within = jnp.dot(flag.astype(BF16), _strict_triangle(LANES, False), preferred_element_type=F32)
    per_row = jnp.broadcast_to(jnp.sum(flag, axis=1, keepdims=True), flag.shape)
    before = jnp.dot(_strict_triangle(rows, True), per_row.astype(BF16), preferred_element_type=F32)
    return before + within


def _sindex_kernel(pg_ref, qi_ref, wi_ref, knew_ref, cache_ref, idx_ref, kbuf, sem, sc_ref, pos_ref,
                   *, n_pages, topk, wscale, group):
    b = pl.program_id(0)
    nb = pl.num_programs(0)
    slot = lax.rem(b, 2)

    def page_copy(bb, p, sl):
        return pltpu.make_async_copy(cache_ref.at[pg_ref[bb, p]], kbuf.at[sl, p], sem.at[sl])

    def start_all(bb, sl):
        lax.fori_loop(0, n_pages, lambda p, c: (page_copy(bb, p, sl).start(), c)[1], 0)

    @pl.when(b == 0)
    def _():
        start_all(b, slot)

    @pl.when(b + 1 < nb)
    def _():
        start_all(b + 1, 1 - slot)

    lax.fori_loop(0, n_pages, lambda p, c: (page_copy(b, p, slot).wait(), c)[1], 0)

    qi = qi_ref[0]
    w = wi_ref[0] * wscale
    for g in range(n_pages // group):
        kc = kbuf[slot, g * group:(g + 1) * group].reshape(group * PAGE_SIZE, IDX_DIM).astype(BF16)
        s = lax.dot_general(qi, kc, NT_DIMS, preferred_element_type=F32)
        r = jnp.sum(jnp.maximum(s, 0.0) * w, axis=0, keepdims=True)
        for j in range(group):
            sc_ref[g * group + j:g * group + j + 1, :] = r[:, j * PAGE_SIZE:(j + 1) * PAGE_SIZE]
    s_new = jnp.sum(qi.astype(F32) * knew_ref[0].astype(BF16).astype(F32), axis=1, keepdims=True)
    s_new = jnp.sum(jnp.maximum(s_new, 0.0) * w, axis=0, keepdims=True)
    pad_rows = sc_ref.shape[0] - n_pages
    first = (lax.broadcasted_iota(jnp.int32, (pad_rows, LANES), 0) == 0) & (
        lax.broadcasted_iota(jnp.int32, (pad_rows, LANES), 1) == 0)
    sc_ref[n_pages:, :] = jnp.where(first, s_new, -jnp.inf)

    s = sc_ref[...]

    def count(hit):
        return jnp.sum(jnp.sum(jnp.where(hit, 1.0, 0.0), axis=1, keepdims=True), axis=0, keepdims=True)

    thr = _key_to_float(_kth_largest_key(lambda cand: count(s >= _key_to_float(cand)), topk,
                                         jnp.zeros((1, 1), jnp.int32)))
    above = s > thr
    tied = jnp.where(s == thr, 1.0, 0.0)
    keep_tied = (tied > 0.5) & (_index_order_rank(tied) < topk - count(above))
    chosen = jnp.where(above | keep_tied, 1.0, 0.0)
    pos_ref[...] = jnp.where(chosen > 0.5, _index_order_rank(chosen), -1.0)

    slot_id = lax.broadcasted_iota(jnp.int32, (topk, LANES), 0).astype(F32)
    lane = lax.broadcasted_iota(jnp.int32, (1, LANES), 1)

    def place(p, acc):
        key_pos = (p * PAGE_SIZE + lane).astype(F32)
        return acc + jnp.where(pos_ref[pl.ds(p, 1), :] == slot_id, key_pos, 0.0)

    acc = lax.fori_loop(0, n_pages + 1, place, jnp.zeros((topk, LANES), F32))
    idx_ref[0] = jnp.sum(acc, axis=1, keepdims=True).astype(jnp.int32)


def _sample_indices(qi, wi, kidx_new, cache_kidx, pages):
    db, n_pages = pages.shape
    past = n_pages * PAGE_SIZE
    topk = min(TOPK_MAX, (past + 1) // 4)
    assert topk % SUBLANES == 0
    group = 16 if n_pages % 16 == 0 else 1
    rows = (n_pages + 1 + SUBLANES - 1) // SUBLANES * SUBLANES
    grid_spec = pltpu.PrefetchScalarGridSpec(
        num_scalar_prefetch=1,
        grid=(db,),
        in_specs=[
            pl.BlockSpec((1, IDX_HEADS, IDX_DIM), lambda b, pg: (b, 0, 0)),
            pl.BlockSpec((1, IDX_HEADS, 1), lambda b, pg: (b, 0, 0)),
            pl.BlockSpec((1, 1, IDX_DIM), lambda b, pg: (b, 0, 0)),
            pl.BlockSpec(memory_space=pl.ANY),
        ],
        out_specs=pl.BlockSpec((1, topk, 1), lambda b, pg: (b, 0, 0)),
        scratch_shapes=[
            pltpu.VMEM((2, n_pages, PAGE_SIZE, IDX_DIM), F32),
            pltpu.SemaphoreType.DMA((2,)),
            pltpu.VMEM((rows, LANES), F32),
            pltpu.VMEM((rows, LANES), F32),
        ],
    )
    idx = pl.pallas_call(
        functools.partial(_sindex_kernel, n_pages=n_pages, topk=topk,
                          wscale=IDX_HEADS ** -0.5 * IDX_DIM ** -0.5, group=group),
        grid_spec=grid_spec,
        out_shape=jax.ShapeDtypeStruct((db, topk, 1), jnp.int32),
        compiler_params=_params("arbitrary"),
        name="sample_indices",
    )(pages, qi.reshape(db, IDX_HEADS, IDX_DIM), wi.reshape(db, IDX_HEADS, 1),
      kidx_new.reshape(db, 1, IDX_DIM), cache_kidx.reshape(-1, PAGE_SIZE, IDX_DIM))
    return idx.reshape(db, topk)


def _sgather_kernel(idx_ref, pg_ref, q_ref, knew_ref, vnew_ref, ck_ref, cv_ref, o_ref, kbuf, vbuf, sem,
                    *, past, topk):
    b = pl.program_id(0)
    nb = pl.num_programs(0)
    slot = lax.rem(b, 2)

    def start_all(bb, sl):
        def body(j, c):
            i = idx_ref[bb, j]
            ip = jnp.minimum(i, past - 1)
            page = pg_ref[bb, ip // PAGE_SIZE]
            off = lax.rem(ip, PAGE_SIZE)

            @pl.when(i >= past)
            def _():
                pltpu.make_async_copy(knew_ref.at[bb], kbuf.at[sl, j], sem.at[0, sl]).start()
                pltpu.make_async_copy(vnew_ref.at[bb], vbuf.at[sl, j], sem.at[1, sl]).start()

            @pl.when(i < past)
            def _():
                pltpu.make_async_copy(ck_ref.at[page, off], kbuf.at[sl, j], sem.at[0, sl]).start()
                pltpu.make_async_copy(cv_ref.at[page, off], vbuf.at[sl, j], sem.at[1, sl]).start()

            return c

        lax.fori_loop(0, topk, body, 0)

    @pl.when(b == 0)
    def _():
        start_all(b, slot)

    @pl.when(b + 1 < nb)
    def _():
        start_all(b + 1, 1 - slot)

    def wait_body(j, c):
        pltpu.make_async_copy(knew_ref.at[b], kbuf.at[slot, j], sem.at[0, slot]).wait()
        pltpu.make_async_copy(vnew_ref.at[b], vbuf.at[slot, j], sem.at[1, slot]).wait()
        return c

    lax.fori_loop(0, topk, wait_body, 0)

    k = kbuf[slot]
    v = vbuf[slot]
    s = jnp.sum(k * q_ref[...], axis=2, keepdims=True) * HEAD_DIM ** -0.5
    e = jnp.exp(s - jnp.max(s, axis=0, keepdims=True))
    p = e / jnp.sum(e, axis=0, keepdims=True)
    o_ref[...] = jnp.sum(p * v, axis=0, keepdims=True).astype(o_ref.dtype)


def _sample_attend(q, k_new, v_new, idx, cache_k, cache_v, pages):
    db, n_pages = pages.shape
    topk = idx.shape[1]
    tile = (N_HEADS, HEAD_DIM)
    grid_spec = pltpu.PrefetchScalarGridSpec(
        num_scalar_prefetch=2,
        grid=(db,),
        in_specs=[
            pl.BlockSpec((1,) + tile, lambda b, ix, pg: (b, 0, 0)),
            pl.BlockSpec(memory_space=pl.ANY),
            pl.BlockSpec(memory_space=pl.ANY),
            pl.BlockSpec(memory_space=pl.ANY),
            pl.BlockSpec(memory_space=pl.ANY),
        ],
        out_specs=pl.BlockSpec((1,) + tile, lambda b, ix, pg: (b, 0, 0)),
        scratch_shapes=[
            pltpu.VMEM((2, topk) + tile, F32),
            pltpu.VMEM((2, topk) + tile, F32),
            pltpu.SemaphoreType.DMA((2, 2)),
        ],
    )
    out = pl.pallas_call(
        functools.partial(_sgather_kernel, past=n_pages * PAGE_SIZE, topk=topk),
        grid_spec=grid_spec,
        out_shape=jax.ShapeDtypeStruct((db,) + tile, BF16),
        compiler_params=_params("arbitrary"),
        name="sample_attend",
    )(idx, pages, q.reshape((db,) + tile), k_new.reshape((db,) + tile), v_new.reshape((db,) + tile),
      cache_k.reshape((-1, PAGE_SIZE) + tile), cache_v.reshape((-1, PAGE_SIZE) + tile))
    return out.reshape(db, N_HEADS * HEAD_DIM)


def _merge_kernel(x_ref, g_ref, ya_ref, yb_ref, ym_ref, woa_ref, wob_ref, wom_ref, wout_ref,
                  gpm_ref, gpf_ref, x1_ref, h2_ref):
    d = x_ref.shape[1]
    merged = (g_ref[:, 0:d].astype(F32) * jnp.dot(ya_ref[...], woa_ref[...], preferred_element_type=F32)
              + g_ref[:, d:2 * d].astype(F32) * jnp.dot(yb_ref[...], wob_ref[...], preferred_element_type=F32)
              + g_ref[:, 2 * d:3 * d].astype(F32) * jnp.dot(ym_ref[...], wom_ref[...], preferred_element_type=F32))
    o = jnp.dot(merged.astype(BF16), wout_ref[...], preferred_element_type=F32)
    x1 = x_ref[...] + _rmsnorm_rows(o, gpm_ref[...])
    x1_ref[...] = x1
    h2_ref[...] = _rmsnorm_rows(x1, gpf_ref[...]).astype(h2_ref.dtype)


def _merge(x, gates, y_a, y_b, y_m, w_oa, w_ob, w_om, w_out, g_post_mix, g_pre_ffn):
    m, d = x.shape
    tm = _tile(m, 256)
    once = pl.Buffered(1)

    def rows(width):
        return pl.BlockSpec((tm, width), lambda i: (i, 0))

    def whole(a):
        return pl.BlockSpec(a.shape, lambda i: (0, 0), pipeline_mode=once)

    return pl.pallas_call(
        _merge_kernel,
        grid=(m // tm,),
        in_specs=[rows(d), rows(N_BRANCH * d), rows(y_a.shape[1]), rows(y_b.shape[1]), rows(y_m.shape[1]),
                  whole(w_oa), whole(w_ob), whole(w_om), whole(w_out),
                  pl.BlockSpec((1, d), lambda i: (0, 0)), pl.BlockSpec((1, d), lambda i: (0, 0))],
        out_specs=[rows(d), rows(d)],
        out_shape=[jax.ShapeDtypeStruct((m, d), F32), jax.ShapeDtypeStruct((m, d), BF16)],
        compiler_params=_params("parallel"),
        name="merge",
    )(x, gates, y_a, y_b, y_m, w_oa, w_ob, w_om, w_out, g_post_mix.reshape(1, d), g_pre_ffn.reshape(1, d))


def _ffn_kernel(h_ref, wg_ref, wu_ref, wd_ref, x1_ref, gpf_ref, o_ref, acc_ref):
    j = pl.program_id(1)

    @pl.when(j == 0)
    def _():
        acc_ref[...] = jnp.zeros(acc_ref.shape, F32)

    h = h_ref[...]
    a = jnp.dot(h, wg_ref[...], preferred_element_type=F32)
    c = jnp.dot(h, wu_ref[...], preferred_element_type=F32)
    act = (jax.nn.silu(a) * c).astype(BF16)
    acc_ref[...] += jnp.dot(act, wd_ref[...], preferred_element_type=F32)

    @pl.when(j == pl.num_programs(1) - 1)
    def _():
        o_ref[...] = x1_ref[...] + _rmsnorm_rows(acc_ref[...], gpf_ref[...])


def _ffn(h2, x1, w_gu, w_down, g_post_ffn):
    m, d = h2.shape
    d_ff = w_down.shape[0]
    tm = _tile(m, 512)
    tf = _tile(d_ff, 512)
    nf = d_ff // tf
    return pl.pallas_call(
        _ffn_kernel,
        grid=(m // tm, nf),
        in_specs=[
            pl.BlockSpec((tm, d), lambda i, j: (i, 0)),
            pl.BlockSpec((d, tf), lambda i, j: (0, j)),
            pl.BlockSpec((d, tf), lambda i, j: (0, j + nf)),
            pl.BlockSpec((tf, d), lambda i, j: (j, 0)),
            pl.BlockSpec((tm, d), lambda i, j: (i, 0)),
            pl.BlockSpec((1, d), lambda i, j: (0, 0)),
        ],
        out_specs=pl.BlockSpec((tm, d), lambda i, j: (i, 0)),
        out_shape=jax.ShapeDtypeStruct((m, d), F32),
        scratch_shapes=[pltpu.VMEM((tm, d), F32)],
        compiler_params=_params("parallel", "arbitrary"),
        name="ffn",
    )(h2, w_gu, w_gu, w_down, x1, g_post_ffn.reshape(1, d))


def _split_w_in(w_in, d):
    gw = d // 2
    aw = N_HEADS * HEAD_DIM
    iw = IDX_HEADS * IDX_DIM
    mw = MEM_HEADS * MEM_HEAD_DIM
    sizes = (2 * gw, aw, aw, aw, iw, IDX_DIM + IDX_HEADS, mw, N_BRANCH * d)
    w_t = w_in.T
    parts, start = [], 0
    for s in sizes:
        parts.append(w_t[start:start + s].astype(BF16))
        start += s
    w_uv, w_q, w_k, w_v, w_qi, w_kw, w_qm, w_g = parts
    w_kw = jnp.pad(w_kw, ((0, 2 * LANES - w_kw.shape[0]), (0, 0)))
    return w_uv, w_q, w_k, w_v, w_qi, w_kw, w_qm, w_g


def _project_common(x2d, g_pre_mix, ws, ln_g, ln_b, uv_dtype, q_dtype, q_scale):
    w_uv, w_q, w_k, w_v, w_qi, w_kw, w_qm, w_g = ws
    h = _rmsnorm(x2d, g_pre_mix, BF16)
    uv = _project_uv(h, w_uv, ln_g, ln_b, uv_dtype)
    (q,) = _matmul(h, w_q, (q_dtype,), scale=q_scale, name="proj_q")
    k, k_bf = _matmul(h, w_k, (F32, BF16), name="proj_k")
    v, v_bf = _matmul(h, w_v, (F32, BF16), name="proj_v")
    (qi,) = _matmul(h, w_qi, (BF16,), name="proj_qi")
    (kw,) = _matmul(h, w_kw, (F32,), name="proj_kidx")
    (qm,) = _matmul(h, w_qm, (BF16,), name="proj_qm")
    (gates,) = _matmul(h, w_g, (BF16,), act="sigmoid", name="proj_gates")
    kidx = kw[:, :IDX_DIM]
    wi = kw[:, IDX_DIM:IDX_DIM + IDX_HEADS]
    return uv, q, k, k_bf, v, v_bf, qi, kidx, wi, qm, gates


def kernel(x_prompt, x_sample, mem_prompt, cache_k, cache_v, cache_kidx, cache_mem_k, cache_mem_v, page_table, g_pre_mix, g_post_mix, g_pre_ffn, g_post_ffn, g_mem, ln_v_g, ln_v_b, w_in, w_s, b_s, w_oa, w_ob, w_om, w_out, w_mem_kv, w_gu, w_down):
    depth = w_in.shape[0]
    b, t, d = x_prompt.shape
    db, dt, _ = x_sample.shape
    assert dt == 1, "sample path handles one new token per sequence"
    n_mem = mem_prompt.shape[1]
    aw = N_HEADS * HEAD_DIM
    mw = MEM_HEADS * MEM_HEAD_DIM
    gw = d // 2
    past = page_table.shape[1] * PAGE_SIZE

    yp = x_prompt.reshape(b * t, d)
    ys = x_sample.reshape(db * dt, d)
    outs = [[] for _ in range(9)]
    for l in range(depth):
        ws = _split_w_in(w_in[l], d)
        woa, wob, wom, wout = (w.astype(BF16) for w in (w_oa[l], w_ob[l], w_om[l], w_out[l]))
        wgu, wdown, wmem = w_gu[l].astype(BF16), w_down[l].astype(BF16), w_mem_kv[l].T.astype(BF16)

        uv, q, k, k_bf, v, v_bf, qi, kidx, wi, qm, gates = _project_common(
            yp, g_pre_mix[l], ws, ln_v_g[l], ln_v_b[l], BF16, BF16, DSA_LOG2_SCALE)
        y_a = _gmlp_spatial(uv, w_s[l], b_s[l])
        y_b = _dsa_prompt(
            q.reshape(b, t, aw), qi.reshape(b, t, -1), jnp.swapaxes(wi.reshape(b, t, IDX_HEADS), 1, 2),
            k_bf.reshape(b, t, aw), kidx.astype(BF16).reshape(b, t, IDX_DIM),
            v_bf.reshape(b, t, aw)).reshape(b * t, aw)
        hm = _rmsnorm(mem_prompt.reshape(b * n_mem, d), g_mem[l], BF16)
        (mkv,) = _matmul(hm, wmem, (F32,), name="proj_mem_kv")
        mk = mkv[:, :mw].reshape(b, n_mem, mw)
        mv = mkv[:, mw:].reshape(b, n_mem, mw)
        y_m = _mem_attend(qm.reshape(b, t, mw), mk, mv).reshape(b * t, mw)
        x1, h2 = _merge(yp, gates, y_a, y_b, y_m, woa, wob, wom, wout, g_post_mix[l], g_pre_ffn[l])
        yp = _ffn(h2, x1, wgu, wdown, g_post_ffn[l])
        outs[0].append(k.reshape(b, t, N_HEADS, HEAD_DIM))
        outs[1].append(v.reshape(b, t, N_HEADS, HEAD_DIM))
        outs[2].append(kidx.reshape(b, t, IDX_DIM))
        outs[3].append(mk.reshape(b, n_mem, MEM_HEADS, MEM_HEAD_DIM))
        outs[4].append(mv.reshape(b, n_mem, MEM_HEADS, MEM_HEAD_DIM))

        uv, q, k, k_bf, v, v_bf, qi, kidx, wi, qm, gates = _project_common(
            ys, g_pre_mix[l], ws, ln_v_g[l], ln_v_b[l], F32, F32, None)
        y_a = _gmlp_first_row(uv, w_s[l], b_s[l])
        pages = page_table + l * cache_k.shape[1]
        idx = _sample_indices(qi, wi, kidx, cache_kidx, pages)
        y_b = _sample_attend(q, k, v, idx, cache_k, cache_v, pages)
        qm_pad = jnp.pad(qm.reshape(db, 1, mw), ((0, 0), (0, 2 * SUBLANES - 1), (0, 0)))
        y_m = _mem_attend(qm_pad, cache_mem_k[l].reshape(db, n_mem, mw),
                          cache_mem_v[l].reshape(db, n_mem, mw))[:, 0, :]
        x1, h2 = _merge(ys, gates, y_a, y_b, y_m, woa, wob, wom, wout, g_post_mix[l], g_pre_ffn[l])
        ys = _ffn(h2, x1, wgu, wdown, g_post_ffn[l])
        outs[5].append(k.reshape(db, dt, N_HEADS, HEAD_DIM))
        outs[6].append(v.reshape(db, dt, N_HEADS, HEAD_DIM))
        outs[7].append(kidx.reshape(db, dt, IDX_DIM))
        outs[8].append(uv[:, gw:].reshape(db, dt, gw))

    return (yp.reshape(b, t, d), ys.reshape(db, dt, d)) + tuple(jnp.stack(o) for o in outs)
```

```python
import functools
from typing import NamedTuple, Optional

import jax
import jax.numpy as jnp
from jax import lax
from jax.experimental import pallas as pl
from jax.experimental.pallas import tpu as pltpu

EPS = 1e-6
CHUNK = 128
GM_GROUPS = 8
N_HEADS = 8
HEAD_DIM = 128
IDX_HEADS = 16
IDX_DIM = 128
TOPK_MAX = 256
MEM_HEADS = 4
MEM_HEAD_DIM = 128
N_BRANCH = 3
PAGE_SIZE = 128

LANES = 128
SUBLANES = 8
VMEM_LIMIT_BYTES = 56 * 2**20
INT_MIN = -2**31
KEY_LOWEST = INT_MIN + 0x800000
KEY_INF = 0x7F800000
NEG = -1e30
LOG2_E = 1.4426950408889634
DSA_LOG2_SCALE = HEAD_DIM ** -0.5 * LOG2_E
ONES_ROWS = 16
COUNT_ROWS = 64
PROJ_TN = 512

F32 = jnp.float32
BF16 = jnp.bfloat16
NT_DIMS = (((1,), (1,)), ((), ()))


def _params(*sem):
    return pltpu.CompilerParams(dimension_semantics=sem, vmem_limit_bytes=VMEM_LIMIT_BYTES)


def _tile(n, pref):
    if n <= pref:
        return n
    t = pref
    while n % t:
        t //= 2
    return t


def _key_to_float(key):
    key = jnp.clip(key, KEY_LOWEST, KEY_INF)
    return pltpu.bitcast(key ^ ((key >> 31) & 0x7FFFFFFF), F32)


def _kth_largest_key(count_ge, topk, like):
    base = jnp.where(count_ge(jnp.zeros_like(like)) >= topk, 0, INT_MIN).astype(jnp.int32)

    def bit_body(b, t):
        cand = t | jnp.left_shift(jnp.int32(1), 30 - b)
        return jnp.where(count_ge(cand) >= topk, cand, t)

    return lax.fori_loop(0, 31, bit_body, base)


def _rmsnorm_rows(x, g):
    return x * lax.rsqrt(jnp.mean(x * x, axis=-1, keepdims=True) + EPS) * g


class _Group(NamedTuple):
    tiles: int
    dtypes: tuple
    act: Optional[str] = None
    scale: Optional[float] = None
    width: int = PROJ_TN


def _proj_kernel(x_ref, g_ref, w_ref, *rest, groups):
    o_refs, h_ref = rest[:-1], rest[-1]
    j = pl.program_id(1)

    @pl.when(j == 0)
    def _():
        h_ref[...] = _rmsnorm_rows(x_ref[...], g_ref[...]).astype(h_ref.dtype)

    acc = lax.dot_general(h_ref[...], w_ref[...], NT_DIMS, preferred_element_type=F32)
    start = n_out = 0
    for grp in groups:
        refs = o_refs[n_out:n_out + len(grp.dtypes)]

        def emit(grp=grp, refs=refs):
            y = acc[:, :grp.width]
            if grp.act == "gelu":
                y = jax.nn.gelu(y)
            elif grp.act == "sigmoid":
                y = jax.nn.sigmoid(y)
            if grp.scale is not None:
                y = y * grp.scale
            for r in refs:
                r[...] = y.astype(r.dtype)

        pl.when((j >= start) & (j < start + grp.tiles))(emit)
        start += grp.tiles
        n_out += len(grp.dtypes)


def _project(x, g, w_t, groups, name):
    m, k = x.shape
    tm = _tile(m, 1024)
    n_tiles = w_t.shape[0] // PROJ_TN
    assert n_tiles == sum(grp.tiles for grp in groups)
    out_specs, out_shapes, start = [], [], 0
    for grp in groups:
        def index(i, j, start=start, last=grp.tiles - 1):
            return i, jnp.clip(j - start, 0, last)

        for dt in grp.dtypes:
            out_specs.append(pl.BlockSpec((tm, grp.width), index))
            out_shapes.append(jax.ShapeDtypeStruct((m, grp.tiles * grp.width), dt))
        start += grp.tiles
    return pl.pallas_call(
        functools.partial(_proj_kernel, groups=groups),
        grid=(m // tm, n_tiles),
        in_specs=[
            pl.BlockSpec((tm, k), lambda i, j: (i, 0)),
            pl.BlockSpec((1, k), lambda i, j: (0, 0)),
            pl.BlockSpec((PROJ_TN, k), lambda i, j: (j, 0)),
        ],
        out_specs=out_specs,
        out_shape=out_shapes,
        scratch_shapes=[pltpu.VMEM((tm, k), BF16)],
        compiler_params=_params("parallel", "arbitrary"),
        name=name,
    )(x, g.reshape(1, k), w_t)


def _layernorm_rows(y, g, b):
    yc = y - jnp.mean(y, axis=-1, keepdims=True)
    return yc * lax.rsqrt(jnp.mean(yc * yc, axis=-1, keepdims=True) + EPS) * g + b


def _gmlp_kernel(u_ref, v_ref, lng_ref, lnb_ref, ws_ref, bt_ref, o_ref, *, n_chunks):
    c = CHUNK
    gd = u_ref.shape[1] // GM_GROUPS
    vn = _layernorm_rows(v_ref[...].astype(F32), lng_ref[...], lnb_ref[...]).astype(BF16)
    causal = lax.broadcasted_iota(jnp.int32, (c, c), 1) <= lax.broadcasted_iota(jnp.int32, (c, c), 0)
    for g in range(GM_GROUPS):
        w = jnp.where(causal, ws_ref[g], 0.0).astype(BF16)
        bcol = bt_ref[:, g:g + 1]
        for ci in range(n_chunks):
            rows = slice(ci * c, (ci + 1) * c)
            cols = slice(g * gd, (g + 1) * gd)
            mixed = jnp.dot(w, vn[rows, cols], preferred_element_type=F32) + bcol
            o_ref[rows, cols] = (u_ref[rows, cols].astype(F32) * mixed).astype(o_ref.dtype)


def _gmlp_spatial(uv, ln_g, ln_b, w_s, b_s):
    m = uv.shape[0]
    gw = uv.shape[1] // 2
    tm = _tile(m, 4 * CHUNK)
    return pl.pallas_call(
        functools.partial(_gmlp_kernel, n_chunks=tm // CHUNK),
        grid=(m // tm,),
        in_specs=[
            pl.BlockSpec((tm, gw), lambda i: (i, 0)),
            pl.BlockSpec((tm, gw), lambda i: (i, 1)),
            pl.BlockSpec((1, gw), lambda i: (0, 0)),
            pl.BlockSpec((1, gw), lambda i: (0, 0)),
            pl.BlockSpec((GM_GROUPS, CHUNK, CHUNK), lambda i: (0, 0, 0)),
            pl.BlockSpec((CHUNK, GM_GROUPS), lambda i: (0, 0)),
        ],
        out_specs=pl.BlockSpec((tm, gw), lambda i: (i, 0)),
        out_shape=jax.ShapeDtypeStruct((m, gw), BF16),
        compiler_params=_params("parallel"),
        name="gmlp_spatial",
    )(uv, uv, ln_g.reshape(1, gw), ln_b.reshape(1, gw), w_s, b_s.T)


def _gmlp_first_kernel(u_ref, v_ref, lng_ref, lnb_ref, w_ref, b_ref, o_ref, vn_ref):
    vn = _layernorm_rows(v_ref[...], lng_ref[...], lnb_ref[...])
    vn_ref[...] = vn
    o_ref[...] = (u_ref[...] * (w_ref[...] * vn + b_ref[...])).astype(o_ref.dtype)


def _gmlp_first_row(uv, ln_g, ln_b, w_s, b_s):
    m = uv.shape[0]
    gw = uv.shape[1] // 2
    gd = gw // GM_GROUPS
    w_row = jnp.repeat(w_s[:, 0, 0], gd).reshape(1, gw)
    b_row = jnp.repeat(b_s[:, 0], gd).reshape(1, gw)
    row = pl.BlockSpec((1, gw), lambda i: (0, 0))
    return pl.pallas_call(
        _gmlp_first_kernel,
        grid=(1,),
        in_specs=[pl.BlockSpec((m, gw), lambda i: (0, 0)), pl.BlockSpec((m, gw), lambda i: (0, 1)),
                  row, row, row, row],
        out_specs=[pl.BlockSpec((m, gw), lambda i: (0, 0)), pl.BlockSpec((m, gw), lambda i: (0, 0))],
        out_shape=[jax.ShapeDtypeStruct((m, gw), BF16), jax.ShapeDtypeStruct((m, gw), F32)],
        compiler_params=_params("arbitrary"),
        name="gmlp_first_row",
    )(uv, uv, ln_g.reshape(1, gw), ln_b.reshape(1, gw), w_row, b_row)


def _dsa_kernel(q_ref, qi_ref, wit_ref, k_ref, kidx_ref, vt_ref, o_ref,
                sc_ref, m_ref, acc_ref, s_ref, p_ref, *, tq, tk, topk, wscale):
    i = pl.program_id(1)
    n_chunks = ((i + 1) * tq + tk - 1) // tk
    qpos = i * tq + lax.broadcasted_iota(jnp.int32, (1, tq), 1)
    vrows = vt_ref.shape[1] // N_HEADS

    def idx_body(c, carry):
        k0 = pl.multiple_of(c * tk, tk)
        kc = kidx_ref[0, pl.ds(k0, tk), :]
        acc = jnp.zeros((tk, tq), F32)
        for h in range(IDX_HEADS):
            s = lax.dot_general(kc, qi_ref[0, :, h * IDX_DIM:(h + 1) * IDX_DIM], NT_DIMS,
                                preferred_element_type=F32)
            acc = acc + jnp.maximum(s, 0.0) * (wit_ref[0, h:h + 1, :] * wscale)
        kpos = k0 + lax.broadcasted_iota(jnp.int32, (tk, 1), 0)
        sc_ref[pl.ds(k0, tk), :] = jnp.where(kpos <= qpos, acc, -jnp.inf)
        return carry

    lax.fori_loop(0, n_chunks, idx_body, 0)

    @pl.when(n_chunks % 2 == 1)
    def _():
        sc_ref[pl.ds(pl.multiple_of(n_chunks * tk, tk), tk), :] = jnp.full((tk, tq), -jnp.inf, F32)

    def count_ge(cand):
        cand_f = _key_to_float(cand)

        def body(c, cnt):
            k0 = pl.multiple_of(c * (2 * tk), 2 * tk)
            hit = jnp.where(sc_ref[pl.ds(k0, 2 * tk), :] >= cand_f, 1.0, 0.0)
            return cnt + jnp.sum(hit.reshape(2 * tk // COUNT_ROWS, COUNT_ROWS, tq), axis=0)

        cnt = lax.fori_loop(0, (n_chunks + 1) // 2, body, jnp.zeros((COUNT_ROWS, tq), F32))
        return jnp.sum(cnt, axis=0, keepdims=True)

    thr = _key_to_float(_kth_largest_key(count_ge, topk, qpos))

    m_ref[...] = jnp.full(m_ref.shape, NEG, F32)
    acc_ref[...] = jnp.zeros(acc_ref.shape, F32)

    def att_body(c, carry):
        k0 = pl.multiple_of(c * tk, tk)
        bias = jnp.where(sc_ref[pl.ds(k0, tk), :] >= thr, 0.0, NEG)
        m_old, m_new = [], []
        for h in range(N_HEADS):
            hs = slice(h * HEAD_DIM, (h + 1) * HEAD_DIM)
            s = lax.dot_general(k_ref[0, pl.ds(k0, tk), hs], q_ref[0, :, hs], NT_DIMS,
                                preferred_element_type=F32) + bias
            s_ref[h] = s
            m_old.append(m_ref[h])
            m_new.append(jnp.maximum(m_old[h], jnp.max(s, axis=0, keepdims=True)))
            m_ref[h] = m_new[h]
        for h in range(N_HEADS):
            p_ref[h] = jnp.exp2(s_ref[h] - m_new[h][0:1, :]).astype(BF16)
        for h in range(N_HEADS):
            pv = jnp.dot(vt_ref[0, h * vrows:(h + 1) * vrows, pl.ds(k0, tk)], p_ref[h],
                         preferred_element_type=F32)
            acc_ref[h] = jnp.exp2(m_old[h] - m_new[h])[0:1, :] * acc_ref[h] + pv
        return carry

    lax.fori_loop(0, n_chunks, att_body, 0)

    for h in range(N_HEADS):
        o = acc_ref[h, 0:HEAD_DIM, :] / acc_ref[h, HEAD_DIM:HEAD_DIM + 1, :]
        o_ref[0, :, h * HEAD_DIM:(h + 1) * HEAD_DIM] = o.T.astype(o_ref.dtype)


def _dsa_prompt(q, qi, wi_t, k, kidx, v, tq=256, tk=256):
    b, t, aw = q.shape
    ones = jnp.ones((b, N_HEADS, ONES_ROWS, t), v.dtype)
    v_t = jnp.concatenate([jnp.transpose(v.reshape(b, t, N_HEADS, HEAD_DIM), (0, 2, 3, 1)), ones], axis=2)
    v_t = v_t.reshape(b, N_HEADS * (HEAD_DIM + ONES_ROWS), t)
    tq = _tile(t, tq)
    tk = _tile(tq, tk)
    topk = min(TOPK_MAX, t // 4)
    once = pl.Buffered(1)
    return pl.pallas_call(
        functools.partial(_dsa_kernel, tq=tq, tk=tk, topk=topk,
                          wscale=IDX_HEADS ** -0.5 * IDX_DIM ** -0.5),
        grid=(b, t // tq),
        in_specs=[
            pl.BlockSpec((1, tq, aw), lambda bi, i: (bi, i, 0)),
            pl.BlockSpec((1, tq, IDX_HEADS * IDX_DIM), lambda bi, i: (bi, i, 0)),
            pl.BlockSpec((1, IDX_HEADS, tq), lambda bi, i: (bi, 0, i)),
            pl.BlockSpec((1, t, aw), lambda bi, i: (bi, 0, 0), pipeline_mode=once),
            pl.BlockSpec((1, t, IDX_DIM), lambda bi, i: (bi, 0, 0), pipeline_mode=once),
            pl.BlockSpec((1, v_t.shape[1], t), lambda bi, i: (bi, 0, 0), pipeline_mode=once),
        ],
        out_specs=pl.BlockSpec((1, tq, aw), lambda bi, i: (bi, i, 0)),
        out_shape=jax.ShapeDtypeStruct((b, t, aw), BF16),
        scratch_shapes=[
            pltpu.VMEM((t + tk, tq), F32),
            pltpu.VMEM((N_HEADS, SUBLANES, tq), F32),
            pltpu.VMEM((N_HEADS, HEAD_DIM + ONES_ROWS, tq), F32),
            pltpu.VMEM((N_HEADS, tk, tq), F32),
            pltpu.VMEM((N_HEADS, tk, tq), BF16),
        ],
        compiler_params=_params("parallel", "arbitrary"),
        name="dsa_prompt",
    )(q, qi, wi_t, k, kidx, v_t)


def _memattn_kernel(qm_ref, mk_ref, mv_ref, o_ref):
    scale = MEM_HEAD_DIM ** -0.5
    for h in range(MEM_HEADS):
        hs = slice(h * MEM_HEAD_DIM, (h + 1) * MEM_HEAD_DIM)
        s = lax.dot_general(qm_ref[0, :, hs], mk_ref[0, :, hs].astype(BF16), NT_DIMS,
                            preferred_element_type=F32) * scale
        e = jnp.exp(s - jnp.max(s, axis=-1, keepdims=True))
        p = e / jnp.sum(e, axis=-1, keepdims=True)
        o = jnp.dot(p.astype(BF16), mv_ref[0, :, hs].astype(BF16), preferred_element_type=F32)
        o_ref[0, :, hs] = o.astype(o_ref.dtype)


def _mem_attend(qm, mk, mv):
    b, t, mw = qm.shape
    n_mem = mk.shape[1]
    tm = _tile(t, 512)
    return pl.pallas_call(
        _memattn_kernel,
        grid=(b, t // tm),
        in_specs=[
            pl.BlockSpec((1, tm, mw), lambda bi, i: (bi, i, 0)),
            pl.BlockSpec((1, n_mem, mw), lambda bi, i: (bi, 0, 0)),
            pl.BlockSpec((1, n_mem, mw), lambda bi, i: (bi, 0, 0)),
        ],
        out_specs=pl.BlockSpec((1, tm, mw), lambda bi, i: (bi, i, 0)),
        out_shape=jax.ShapeDtypeStruct((b, t, mw), BF16),
        compiler_params=_params("parallel", "parallel"),
        name="mem_attend",
    )(qm, mk, mv)


def _strict_triangle(n, lower):
    r = lax.broadcasted_iota(jnp.int32, (n, n), 0)
    c = lax.broadcasted_iota(jnp.int32, (n, n), 1)
    return jnp.where((c < r) if lower else (r < c), 1.0, 0.0).astype(BF16)


def _index_order_rank(flag):
    rows = flag.shape[0]
    within = jnp.dot(flag.astype(BF16), _strict_triangle(LANES, False), preferred_element_type=F32)
    per_row = jnp.broadcast_to(jnp.sum(flag, axis=1, keepdims=True), flag.shape)
    before = jnp.dot(_strict_triangle(rows, True), per_row.astype(BF16), preferred_element_type=F32)
    return before + within


def _sindex_kernel(pg_ref, qi_ref, wi_ref, knew_ref, cache_ref, idx_ref, kbuf, sem, sc_ref, pos_ref,
                   *, n_pages, topk, wscale, group):
    b = pl.program_id(0)
    nb = pl.num_programs(0)
    slot = lax.rem(b, 2)

    def page_copy(bb, p, sl):
        return pltpu.make_async_copy(cache_ref.at[pg_ref[bb, p]], kbuf.at[sl, p], sem.at[sl])

    def start_all(bb, sl):
        lax.fori_loop(0, n_pages, lambda p, c: (page_copy(bb, p, sl).start(), c)[1], 0)

    @pl.when(b == 0)
    def _():
        start_all(b, slot)

    @pl.when(b + 1 < nb)
    def _():
        start_all(b + 1, 1 - slot)

    lax.fori_loop(0, n_pages, lambda p, c: (page_copy(b, p, slot).wait(), c)[1], 0)

    qi = qi_ref[0]
    w = wi_ref[0] * wscale
    for g in range(n_pages // group):
        kc = kbuf[slot, g * group:(g + 1) * group].reshape(group * PAGE_SIZE, IDX_DIM).astype(BF16)
        s = lax.dot_general(qi, kc, NT_DIMS, preferred_element_type=F32)
        r = jnp.sum(jnp.maximum(s, 0.0) * w, axis=0, keepdims=True)
        for j in range(group):
            sc_ref[g * group + j:g * group + j + 1, :] = r[:, j * PAGE_SIZE:(j + 1) * PAGE_SIZE]
    s_new = jnp.sum(qi.astype(F32) * knew_ref[0].astype(BF16).astype(F32), axis=1, keepdims=True)
    s_new = jnp.sum(jnp.maximum(s_new, 0.0) * w, axis=0, keepdims=True)
    pad_rows = sc_ref.shape[0] - n_pages
    first = (lax.broadcasted_iota(jnp.int32, (pad_rows, LANES), 0) == 0) & (
        lax.broadcasted_iota(jnp.int32, (pad_rows, LANES), 1) == 0)
    sc_ref[n_pages:, :] = jnp.where(first, s_new, -jnp.inf)

    s = sc_ref[...]

    def count(hit):
        return jnp.sum(jnp.sum(jnp.where(hit, 1.0, 0.0), axis=1, keepdims=True), axis=0, keepdims=True)

    thr = _key_to_float(_kth_largest_key(lambda cand: count(s >= _key_to_float(cand)), topk,
                                         jnp.zeros((1, 1), jnp.int32)))
    above = s > thr
    tied = jnp.where(s == thr, 1.0, 0.0)
    keep_tied = (tied > 0.5) & (_index_order_rank(tied) < topk - count(above))
    chosen = jnp.where(above | keep_tied, 1.0, 0.0)
    pos_ref[...] = jnp.where(chosen > 0.5, _index_order_rank(chosen), -1.0)

    slot_id = lax.broadcasted_iota(jnp.int32, (topk, LANES), 0).astype(F32)
    lane = lax.broadcasted_iota(jnp.int32, (1, LANES), 1)

    def place(p, acc):
        key_pos = (p * PAGE_SIZE + lane).astype(F32)
        return acc + jnp.where(pos_ref[pl.ds(p, 1), :] == slot_id, key_pos, 0.0)

    acc = lax.fori_loop(0, n_pages + 1, place, jnp.zeros((topk, LANES), F32))
    idx_ref[0] = jnp.sum(acc, axis=1, keepdims=True).astype(jnp.int32)


def _sample_indices(qi, wi, kidx_new, cache_kidx, pages):
    db, n_pages = pages.shape
    past = n_pages * PAGE_SIZE
    topk = min(TOPK_MAX, (past + 1) // 4)
    assert topk % SUBLANES == 0
    group = 16 if n_pages % 16 == 0 else 1
    rows = (n_pages + 1 + SUBLANES - 1) // SUBLANES * SUBLANES
    grid_spec = pltpu.PrefetchScalarGridSpec(
        num_scalar_prefetch=1,
        grid=(db,),
        in_specs=[
            pl.BlockSpec((1, IDX_HEADS, IDX_DIM), lambda b, pg: (b, 0, 0)),
            pl.BlockSpec((1, IDX_HEADS, 1), lambda b, pg: (b, 0, 0)),
            pl.BlockSpec((1, 1, IDX_DIM), lambda b, pg: (b, 0, 0)),
            pl.BlockSpec(memory_space=pl.ANY),
        ],
        out_specs=pl.BlockSpec((1, topk, 1), lambda b, pg: (b, 0, 0)),
        scratch_shapes=[
            pltpu.VMEM((2, n_pages, PAGE_SIZE, IDX_DIM), F32),
            pltpu.SemaphoreType.DMA((2,)),
            pltpu.VMEM((rows, LANES), F32),
            pltpu.VMEM((rows, LANES), F32),
        ],
    )
    idx = pl.pallas_call(
        functools.partial(_sindex_kernel, n_pages=n_pages, topk=topk,
                          wscale=IDX_HEADS ** -0.5 * IDX_DIM ** -0.5, group=group),
        grid_spec=grid_spec,
        out_shape=jax.ShapeDtypeStruct((db, topk, 1), jnp.int32),
        compiler_params=_params("arbitrary"),
        name="sample_indices",
    )(pages, qi.reshape(db, IDX_HEADS, IDX_DIM), wi.reshape(db, IDX_HEADS, 1),
      kidx_new.reshape(db, 1, IDX_DIM), cache_kidx.reshape(-1, PAGE_SIZE, IDX_DIM))
    return idx.reshape(db, topk)


def _sgather_kernel(idx_ref, pg_ref, q_ref, knew_ref, vnew_ref, ck_ref, cv_ref, o_ref, kbuf, vbuf, sem,
                    *, past, topk):
    b = pl.program_id(0)
    nb = pl.num_programs(0)
    slot = lax.rem(b, 2)

    def start_all(bb, sl):
        def body(j, c):
            i = idx_ref[bb, j]
            ip = jnp.minimum(i, past - 1)
            page = pg_ref[bb, ip // PAGE_SIZE]
            off = lax.rem(ip, PAGE_SIZE)

            @pl.when(i >= past)
            def _():
                pltpu.make_async_copy(knew_ref.at[bb], kbuf.at[sl, j], sem.at[0, sl]).start()
                pltpu.make_async_copy(vnew_ref.at[bb], vbuf.at[sl, j], sem.at[1, sl]).start()

            @pl.when(i < past)
            def _():
                pltpu.make_async_copy(ck_ref.at[page, off], kbuf.at[sl, j], sem.at[0, sl]).start()
                pltpu.make_async_copy(cv_ref.at[page, off], vbuf.at[sl, j], sem.at[1, sl]).start()

            return c

        lax.fori_loop(0, topk, body, 0)

    @pl.when(b == 0)
    def _():
        start_all(b, slot)

    @pl.when(b + 1 < nb)
    def _():
        start_all(b + 1, 1 - slot)

    def wait_body(j, c):
        pltpu.make_async_copy(knew_ref.at[b], kbuf.at[slot, j], sem.at[0, slot]).wait()
        pltpu.make_async_copy(vnew_ref.at[b], vbuf.at[slot, j], sem.at[1, slot]).wait()
        return c

    lax.fori_loop(0, topk, wait_body, 0)

    k = kbuf[slot]
    v = vbuf[slot]
    s = jnp.sum(k * q_ref[...], axis=2, keepdims=True) * HEAD_DIM ** -0.5
    e = jnp.exp(s - jnp.max(s, axis=0, keepdims=True))
    p = e / jnp.sum(e, axis=0, keepdims=True)
    o_ref[...] = jnp.sum(p * v, axis=0, keepdims=True).astype(o_ref.dtype)


def _sample_attend(q, k_new, v_new, idx, cache_k, cache_v, pages):
    db, n_pages = pages.shape
    topk = idx.shape[1]
    tile = (N_HEADS, HEAD_DIM)
    grid_spec = pltpu.PrefetchScalarGridSpec(
        num_scalar_prefetch=2,
        grid=(db,),
        in_specs=[
            pl.BlockSpec((1,) + tile, lambda b, ix, pg: (b, 0, 0)),
            pl.BlockSpec(memory_space=pl.ANY),
            pl.BlockSpec(memory_space=pl.ANY),
            pl.BlockSpec(memory_space=pl.ANY),
            pl.BlockSpec(memory_space=pl.ANY),
        ],
        out_specs=pl.BlockSpec((1,) + tile, lambda b, ix, pg: (b, 0, 0)),
        scratch_shapes=[
            pltpu.VMEM((2, topk) + tile, F32),
            pltpu.VMEM((2, topk) + tile, F32),
            pltpu.SemaphoreType.DMA((2, 2)),
        ],
    )
    out = pl.pallas_call(
        functools.partial(_sgather_kernel, past=n_pages * PAGE_SIZE, topk=topk),
        grid_spec=grid_spec,
        out_shape=jax.ShapeDtypeStruct((db,) + tile, BF16),
        compiler_params=_params("arbitrary"),
        name="sample_attend",
    )(idx, pages, q.reshape((db,) + tile), k_new.reshape((db,) + tile), v_new.reshape((db,) + tile),
      cache_k.reshape((-1, PAGE_SIZE) + tile), cache_v.reshape((-1, PAGE_SIZE) + tile))
    return out.reshape(db, N_HEADS * HEAD_DIM)


def _merge_kernel(x_ref, g_ref, ya_ref, yb_ref, ym_ref, woa_ref, wob_ref, wom_ref, wout_ref,
                  gpm_ref, gpf_ref, x1_ref, h2_ref):
    d = x_ref.shape[1]
    merged = (g_ref[:, 0:d].astype(F32) * jnp.dot(ya_ref[...], woa_ref[...], preferred_element_type=F32)
              + g_ref[:, d:2 * d].astype(F32) * jnp.dot(yb_ref[...], wob_ref[...], preferred_element_type=F32)
              + g_ref[:, 2 * d:3 * d].astype(F32) * jnp.dot(ym_ref[...], wom_ref[...], preferred_element_type=F32))
    o = jnp.dot(merged.astype(BF16), wout_ref[...], preferred_element_type=F32)
    x1 = x_ref[...] + _rmsnorm_rows(o, gpm_ref[...])
    x1_ref[...] = x1
    h2_ref[...] = _rmsnorm_rows(x1, gpf_ref[...]).astype(h2_ref.dtype)


def _merge(x, gates, y_a, y_b, y_m, w_oa, w_ob, w_om, w_out, g_post_mix, g_pre_ffn):
    m, d = x.shape
    tm = _tile(m, 256)
    once = pl.Buffered(1)

    def rows(width):
        return pl.BlockSpec((tm, width), lambda i: (i, 0))

    def whole(a):
        return pl.BlockSpec(a.shape, lambda i: (0, 0), pipeline_mode=once)

    return pl.pallas_call(
        _merge_kernel,
        grid=(m // tm,),
        in_specs=[rows(d), rows(N_BRANCH * d), rows(y_a.shape[1]), rows(y_b.shape[1]), rows(y_m.shape[1]),
                  whole(w_oa), whole(w_ob), whole(w_om), whole(w_out),
                  pl.BlockSpec((1, d), lambda i: (0, 0)), pl.BlockSpec((1, d), lambda i: (0, 0))],
        out_specs=[rows(d), rows(d)],
        out_shape=[jax.ShapeDtypeStruct((m, d), F32), jax.ShapeDtypeStruct((m, d), BF16)],
        compiler_params=_params("parallel"),
        name="merge",
    )(x, gates, y_a, y_b, y_m, w_oa, w_ob, w_om, w_out, g_post_mix.reshape(1, d), g_pre_ffn.reshape(1, d))


def _ffn_kernel(h_ref, wg_ref, wu_ref, wd_ref, x1_ref, gpf_ref, o_ref, acc_ref):
    j = pl.program_id(1)

    @pl.when(j == 0)
    def _():
        acc_ref[...] = jnp.zeros(acc_ref.shape, F32)

    h = h_ref[...]
    a = jnp.dot(h, wg_ref[...], preferred_element_type=F32)
    c = jnp.dot(h, wu_ref[...], preferred_element_type=F32)
    act = (jax.nn.silu(a) * c).astype(BF16)
    acc_ref[...] += jnp.dot(act, wd_ref[...], preferred_element_type=F32)

    @pl.when(j == pl.num_programs(1) - 1)
    def _():
        o_ref[...] = x1_ref[...] + _rmsnorm_rows(acc_ref[...], gpf_ref[...])


def _ffn(h2, x1, w_gu, w_down, g_post_ffn):
    m, d = h2.shape
    d_ff = w_down.shape[0]
    tm = _tile(m, 512)
    tf = _tile(d_ff, 512)
    nf = d_ff // tf
    return pl.pallas_call(
        _ffn_kernel,
        grid=(m // tm, nf),
        in_specs=[
            pl.BlockSpec((tm, d), lambda i, j: (i, 0)),
            pl.BlockSpec((d, tf), lambda i, j: (0, j)),
            pl.BlockSpec((d, tf), lambda i, j: (0, j + nf)),
            pl.BlockSpec((tf, d), lambda i, j: (j, 0)),
            pl.BlockSpec((tm, d), lambda i, j: (i, 0)),
            pl.BlockSpec((1, d), lambda i, j: (0, 0)),
        ],
        out_specs=pl.BlockSpec((tm, d), lambda i, j: (i, 0)),
        out_shape=jax.ShapeDtypeStruct((m, d), F32),
        scratch_shapes=[pltpu.VMEM((tm, d), F32)],
        compiler_params=_params("parallel", "arbitrary"),
        name="ffn",
    )(h2, w_gu, w_gu, w_down, x1, g_post_ffn.reshape(1, d))


def _input_weights(w_in, d):
    head = d + 3 * N_HEADS * HEAD_DIM + IDX_HEADS * IDX_DIM + IDX_DIM + IDX_HEADS
    w_t = w_in.T.astype(BF16)
    return jnp.concatenate([w_t[:head], jnp.zeros((-head % PROJ_TN, d), BF16), w_t[head:]], axis=0)


def _input_groups(d, uv_dtype, q_dtype, q_scale):
    aw, iw, mw = N_HEADS * HEAD_DIM, IDX_HEADS * IDX_DIM, MEM_HEADS * MEM_HEAD_DIM
    assert all(w % PROJ_TN == 0 for w in (d, aw, iw, mw)) and IDX_DIM + IDX_HEADS <= 2 * LANES
    t = PROJ_TN
    return (
        _Group(d // t, (uv_dtype,), act="gelu"),
        _Group(aw // t, (q_dtype,), scale=q_scale),
        _Group(aw // t, (F32, BF16)),
        _Group(aw // t, (F32, BF16)),
        _Group(iw // t, (BF16,)),
        _Group(1, (F32,), width=2 * LANES),
        _Group(mw // t, (BF16,)),
        _Group(N_BRANCH * d // t, (BF16,), act="sigmoid"),
    )


def _project_common(x2d, g_pre_mix, w_t, uv_dtype, q_dtype, q_scale):
    d = x2d.shape[1]
    uv, q, k, k_bf, v, v_bf, qi, kw, qm, gates = _project(
        x2d, g_pre_mix, w_t, _input_groups(d, uv_dtype, q_dtype, q_scale), "project_in")
    kidx = kw[:, :IDX_DIM]
    wi = kw[:, IDX_DIM:IDX_DIM + IDX_HEADS]
    return uv, q, k, k_bf, v, v_bf, qi, kidx, wi, qm, gates


def kernel(x_prompt, x_sample, mem_prompt, cache_k, cache_v, cache_kidx, cache_mem_k, cache_mem_v, page_table, g_pre_mix, g_post_mix, g_pre_ffn, g_post_ffn, g_mem, ln_v_g, ln_v_b, w_in, w_s, b_s, w_oa, w_ob, w_om, w_out, w_mem_kv, w_gu, w_down):
    depth = w_in.shape[0]
    b, t, d = x_prompt.shape
    db, dt, _ = x_sample.shape
    assert dt == 1, "sample path handles one new token per sequence"
    n_mem = mem_prompt.shape[1]
    aw = N_HEADS * HEAD_DIM
    mw = MEM_HEADS * MEM_HEAD_DIM
    gw = d // 2
    past = page_table.shape[1] * PAGE_SIZE

    yp = x_prompt.reshape(b * t, d)
    ys = x_sample.reshape(db * dt, d)
    outs = [[] for _ in range(9)]
    for l in range(depth):
        w_t = _input_weights(w_in[l], d)
        woa, wob, wom, wout = (w.astype(BF16) for w in (w_oa[l], w_ob[l], w_om[l], w_out[l]))
        wgu, wdown, wmem = w_gu[l].astype(BF16), w_down[l].astype(BF16), w_mem_kv[l].T.astype(BF16)

        uv, q, k, k_bf, v, v_bf, qi, kidx, wi, qm, gates = _project_common(
            yp, g_pre_mix[l], w_t, BF16, BF16, DSA_LOG2_SCALE)
        y_a = _gmlp_spatial(uv, ln_v_g[l], ln_v_b[l], w_s[l], b_s[l])
        y_b = _dsa_prompt(
            q.reshape(b, t, aw), qi.reshape(b, t, -1), jnp.swapaxes(wi.reshape(b, t, IDX_HEADS), 1, 2),
            k_bf.reshape(b, t, aw), kidx.astype(BF16).reshape(b, t, IDX_DIM),
            v_bf.reshape(b, t, aw)).reshape(b * t, aw)
        (mkv,) = _project(mem_prompt.reshape(b * n_mem, d), g_mem[l], wmem,
                          (_Group(2 * mw // PROJ_TN, (F32,)),), "project_mem_kv")
        mk = mkv[:, :mw].reshape(b, n_mem, mw)
        mv = mkv[:, mw:].reshape(b, n_mem, mw)
        y_m = _mem_attend(qm.reshape(b, t, mw), mk, mv).reshape(b * t, mw)
        x1, h2 = _merge(yp, gates, y_a, y_b, y_m, woa, wob, wom, wout, g_post_mix[l], g_pre_ffn[l])
        yp = _ffn(h2, x1, wgu, wdown, g_post_ffn[l])
        outs[0].append(k.reshape(b, t, N_HEADS, HEAD_DIM))
        outs[1].append(v.reshape(b, t, N_HEADS, HEAD_DIM))
        outs[2].append(kidx.reshape(b, t, IDX_DIM))
        outs[3].append(mk.reshape(b, n_mem, MEM_HEADS, MEM_HEAD_DIM))
        outs[4].append(mv.reshape(b, n_mem, MEM_HEADS, MEM_HEAD_DIM))

        uv, q, k, k_bf, v, v_bf, qi, kidx, wi, qm, gates = _project_common(
            ys, g_pre_mix[l], w_t, F32, F32, None)
        y_a, vn = _gmlp_first_row(uv, ln_v_g[l], ln_v_b[l], w_s[l], b_s[l])
        pages = page_table + l * cache_k.shape[1]
        idx = _sample_indices(qi, wi, kidx, cache_kidx, pages)
        y_b = _sample_attend(q, k, v, idx, cache_k, cache_v, pages)
        qm_pad = jnp.pad(qm.reshape(db, 1, mw), ((0, 0), (0, 2 * SUBLANES - 1), (0, 0)))
        y_m = _mem_attend(qm_pad, cache_mem_k[l].reshape(db, n_mem, mw),
                          cache_mem_v[l].reshape(db, n_mem, mw))[:, 0, :]
        x1, h2 = _merge(ys, gates, y_a, y_b, y_m, woa, wob, wom, wout, g_post_mix[l], g_pre_ffn[l])
        ys = _ffn(h2, x1, wgu, wdown, g_post_ffn[l])
        outs[5].append(k.reshape(db, dt, N_HEADS, HEAD_DIM))
        outs[6].append(v.reshape(db, dt, N_HEADS, HEAD_DIM))
        outs[7].append(kidx.reshape(db, dt, IDX_DIM))
        outs[8].append(vn.reshape(db, dt, gw))

    return (yp.reshape(b, t, d), ys.reshape(db, dt, d)) + tuple(jnp.stack(o) for o in outs)
```

```python
import functools

import jax
import jax.numpy as jnp
from jax import lax
from jax.experimental import pallas as pl
from jax.experimental.pallas import tpu as pltpu

EPS = 1e-6
CHUNK = 128
GM_GROUPS = 8
N_HEADS = 8
HEAD_DIM = 128
IDX_HEADS = 16
IDX_DIM = 128
TOPK_MAX = 256
MEM_HEADS = 4
MEM_HEAD_DIM = 128
N_BRANCH = 3
PAGE_SIZE = 128

LANES = 128
SUBLANES = 8
VMEM_LIMIT_BYTES = 56 * 2**20
INT_MIN = -2**31
KEY_LOWEST = INT_MIN + 0x800000
KEY_INF = 0x7F800000
NEG = -1e30
LOG2_E = 1.4426950408889634
DSA_LOG2_SCALE = HEAD_DIM ** -0.5 * LOG2_E
ONES_ROWS = 16
COUNT_ROWS = 64
MM_TM, MM_TN = 2048, 1024

F32 = jnp.float32
BF16 = jnp.bfloat16
NT_DIMS = (((1,), (1,)), ((), ()))


def _params(*sem):
    return pltpu.CompilerParams(dimension_semantics=sem, vmem_limit_bytes=VMEM_LIMIT_BYTES)


def _tile(n, pref):
    if n <= pref:
        return n
    t = pref
    while n % t:
        t //= 2
    return t


def _key_to_float(key):
    key = jnp.clip(key, KEY_LOWEST, KEY_INF)
    return pltpu.bitcast(key ^ ((key >> 31) & 0x7FFFFFFF), F32)


def _kth_largest_key(count_ge, topk, like):
    base = jnp.where(count_ge(jnp.zeros_like(like)) >= topk, 0, INT_MIN).astype(jnp.int32)

    def bit_body(b, t):
        cand = t | jnp.left_shift(jnp.int32(1), 30 - b)
        return jnp.where(count_ge(cand) >= topk, cand, t)

    return lax.fori_loop(0, 31, bit_body, base)


def _rmsnorm_rows(x, g):
    return x * lax.rsqrt(jnp.mean(x * x, axis=-1, keepdims=True) + EPS) * g


def _rmsnorm_kernel(x_ref, g_ref, o_ref):
    o_ref[...] = _rmsnorm_rows(x_ref[...], g_ref[...]).astype(o_ref.dtype)


def _rmsnorm(x, g, out_dtype):
    m, d = x.shape
    tm = _tile(m, 512)
    return pl.pallas_call(
        _rmsnorm_kernel,
        grid=(m // tm,),
        in_specs=[pl.BlockSpec((tm, d), lambda i: (i, 0)), pl.BlockSpec((1, d), lambda i: (0, 0))],
        out_specs=pl.BlockSpec((tm, d), lambda i: (i, 0)),
        out_shape=jax.ShapeDtypeStruct((m, d), out_dtype),
        compiler_params=_params("parallel"),
        name="rmsnorm",
    )(x, g.reshape(1, d))


def _mm_kernel(x_ref, w_ref, *o_refs, act, scale):
    acc = lax.dot_general(x_ref[...], w_ref[...], NT_DIMS, preferred_element_type=F32)
    if act == "sigmoid":
        acc = jax.nn.sigmoid(acc)
    if scale is not None:
        acc = acc * scale
    for o_ref in o_refs:
        o_ref[...] = acc.astype(o_ref.dtype)


def _matmul(x, w_t, out_dtypes, act=None, scale=None, name="matmul"):
    m, k = x.shape
    n = w_t.shape[0]
    tm = _tile(m, MM_TM)
    tn = _tile(n, MM_TN)
    outs = pl.pallas_call(
        functools.partial(_mm_kernel, act=act, scale=scale),
        grid=(m // tm, n // tn),
        in_specs=[pl.BlockSpec((tm, k), lambda i, j: (i, 0)), pl.BlockSpec((tn, k), lambda i, j: (j, 0))],
        out_specs=[pl.BlockSpec((tm, tn), lambda i, j: (i, j)) for _ in out_dtypes],
        out_shape=[jax.ShapeDtypeStruct((m, n), dt) for dt in out_dtypes],
        compiler_params=_params("parallel", "parallel"),
        name=name,
    )(x, w_t)
    return outs


def _uv_kernel(x_ref, w_ref, lng_ref, lnb_ref, o_ref):
    y = jax.nn.gelu(lax.dot_general(x_ref[...], w_ref[...], NT_DIMS, preferred_element_type=F32))

    @pl.when(pl.program_id(1) == 0)
    def _():
        o_ref[...] = y.astype(o_ref.dtype)

    @pl.when(pl.program_id(1) == 1)
    def _():
        yc = y - jnp.mean(y, axis=-1, keepdims=True)
        yn = yc * lax.rsqrt(jnp.mean(yc * yc, axis=-1, keepdims=True) + EPS)
        o_ref[...] = (yn * lng_ref[...] + lnb_ref[...]).astype(o_ref.dtype)


def _project_uv(h, w_uv_t, ln_g, ln_b, out_dtype):
    m, k = h.shape
    gw = w_uv_t.shape[0] // 2
    tm = _tile(m, 1024)
    return pl.pallas_call(
        _uv_kernel,
        grid=(m // tm, 2),
        in_specs=[
            pl.BlockSpec((tm, k), lambda i, j: (i, 0)),
            pl.BlockSpec((gw, k), lambda i, j: (j, 0)),
            pl.BlockSpec((1, gw), lambda i, j: (0, 0)),
            pl.BlockSpec((1, gw), lambda i, j: (0, 0)),
        ],
        out_specs=pl.BlockSpec((tm, gw), lambda i, j: (i, j)),
        out_shape=jax.ShapeDtypeStruct((m, 2 * gw), out_dtype),
        compiler_params=_params("parallel", "arbitrary"),
        name="project_uv",
    )(h, w_uv_t, ln_g.reshape(1, gw), ln_b.reshape(1, gw))


def _gmlp_kernel(u_ref, v_ref, ws_ref, bt_ref, o_ref, *, n_chunks):
    c = CHUNK
    gd = u_ref.shape[1] // GM_GROUPS
    causal = lax.broadcasted_iota(jnp.int32, (c, c), 1) <= lax.broadcasted_iota(jnp.int32, (c, c), 0)
    for g in range(GM_GROUPS):
        w = jnp.where(causal, ws_ref[g], 0.0).astype(v_ref.dtype)
        bcol = bt_ref[:, g:g + 1]
        for ci in range(n_chunks):
            rows = slice(ci * c, (ci + 1) * c)
            cols = slice(g * gd, (g + 1) * gd)
            mixed = jnp.dot(w, v_ref[rows, cols], preferred_element_type=F32) + bcol
            o_ref[rows, cols] = (u_ref[rows, cols].astype(F32) * mixed).astype(o_ref.dtype)


def _gmlp_spatial(uv, w_s, b_s):
    m = uv.shape[0]
    gw = uv.shape[1] // 2
    tm = _tile(m, 4 * CHUNK)
    return pl.pallas_call(
        functools.partial(_gmlp_kernel, n_chunks=tm // CHUNK),
        grid=(m // tm,),
        in_specs=[
            pl.BlockSpec((tm, gw), lambda i: (i, 0)),
            pl.BlockSpec((tm, gw), lambda i: (i, 1)),
            pl.BlockSpec((GM_GROUPS, CHUNK, CHUNK), lambda i: (0, 0, 0)),
            pl.BlockSpec((CHUNK, GM_GROUPS), lambda i: (0, 0)),
        ],
        out_specs=pl.BlockSpec((tm, gw), lambda i: (i, 0)),
        out_shape=jax.ShapeDtypeStruct((m, gw), BF16),
        compiler_params=_params("parallel"),
        name="gmlp_spatial",
    )(uv, uv, w_s, b_s.T)


def _gmlp_first_kernel(u_ref, v_ref, w_ref, b_ref, o_ref):
    o_ref[...] = (u_ref[...] * (w_ref[...] * v_ref[...] + b_ref[...])).astype(o_ref.dtype)


def _gmlp_first_row(uv, w_s, b_s):
    m = uv.shape[0]
    gw = uv.shape[1] // 2
    gd = gw // GM_GROUPS
    w_row = jnp.repeat(w_s[:, 0, 0], gd).reshape(1, gw)
    b_row = jnp.repeat(b_s[:, 0], gd).reshape(1, gw)
    return pl.pallas_call(
        _gmlp_first_kernel,
        grid=(1,),
        in_specs=[
            pl.BlockSpec((m, gw), lambda i: (0, 0)),
            pl.BlockSpec((m, gw), lambda i: (0, 1)),
            pl.BlockSpec((1, gw), lambda i: (0, 0)),
            pl.BlockSpec((1, gw), lambda i: (0, 0)),
        ],
        out_specs=pl.BlockSpec((m, gw), lambda i: (0, 0)),
        out_shape=jax.ShapeDtypeStruct((m, gw), BF16),
        compiler_params=_params("arbitrary"),
        name="gmlp_first_row",
    )(uv, uv, w_row, b_row)


def _dsa_kernel(q_ref, qi_ref, wit_ref, k_ref, kidx_ref, vt_ref, o_ref,
                sc_ref, m_ref, acc_ref, s_ref, p_ref, *, tq, tk, topk, wscale):
    i = pl.program_id(1)
    n_chunks = ((i + 1) * tq + tk - 1) // tk
    qpos = i * tq + lax.broadcasted_iota(jnp.int32, (1, tq), 1)
    vrows = vt_ref.shape[1] // N_HEADS

    def idx_body(c, carry):
        k0 = pl.multiple_of(c * tk, tk)
        kc = kidx_ref[0, pl.ds(k0, tk), :]
        acc = jnp.zeros((tk, tq), F32)
        for h in range(IDX_HEADS):
            s = lax.dot_general(kc, qi_ref[0, :, h * IDX_DIM:(h + 1) * IDX_DIM], NT_DIMS,
                                preferred_element_type=F32)
            acc = acc + jnp.maximum(s, 0.0) * (wit_ref[0, h:h + 1, :] * wscale)
        kpos = k0 + lax.broadcasted_iota(jnp.int32, (tk, 1), 0)
        sc_ref[pl.ds(k0, tk), :] = jnp.where(kpos <= qpos, acc, -jnp.inf)
        return carry

    lax.fori_loop(0, n_chunks, idx_body, 0)

    @pl.when(n_chunks % 2 == 1)
    def _():
        sc_ref[pl.ds(pl.multiple_of(n_chunks * tk, tk), tk), :] = jnp.full((tk, tq), -jnp.inf, F32)

    def count_ge(cand):
        cand_f = _key_to_float(cand)

        def body(c, cnt):
            k0 = pl.multiple_of(c * (2 * tk), 2 * tk)
            hit = jnp.where(sc_ref[pl.ds(k0, 2 * tk), :] >= cand_f, 1.0, 0.0)
            return cnt + jnp.sum(hit.reshape(2 * tk // COUNT_ROWS, COUNT_ROWS, tq), axis=0)

        cnt = lax.fori_loop(0, (n_chunks + 1) // 2, body, jnp.zeros((COUNT_ROWS, tq), F32))
        return jnp.sum(cnt, axis=0, keepdims=True)

    thr = _key_to_float(_kth_largest_key(count_ge, topk, qpos))

    m_ref[...] = jnp.full(m_ref.shape, NEG, F32)
    acc_ref[...] = jnp.zeros(acc_ref.shape, F32)

    def att_body(c, carry):
        k0 = pl.multiple_of(c * tk, tk)
        bias = jnp.where(sc_ref[pl.ds(k0, tk), :] >= thr, 0.0, NEG)
        m_old, m_new = [], []
        for h in range(N_HEADS):
            hs = slice(h * HEAD_DIM, (h + 1) * HEAD_DIM)
            s = lax.dot_general(k_ref[0, pl.ds(k0, tk), hs], q_ref[0, :, hs], NT_DIMS,
                                preferred_element_type=F32) + bias
            s_ref[h] = s
            m_old.append(m_ref[h])
            m_new.append(jnp.maximum(m_old[h], jnp.max(s, axis=0, keepdims=True)))
            m_ref[h] = m_new[h]
        for h in range(N_HEADS):
            p_ref[h] = jnp.exp2(s_ref[h] - m_new[h][0:1, :]).astype(BF16)
        for h in range(N_HEADS):
            pv = jnp.dot(vt_ref[0, h * vrows:(h + 1) * vrows, pl.ds(k0, tk)], p_ref[h],
                         preferred_element_type=F32)
            acc_ref[h] = jnp.exp2(m_old[h] - m_new[h])[0:1, :] * acc_ref[h] + pv
        return carry

    lax.fori_loop(0, n_chunks, att_body, 0)

    for h in range(N_HEADS):
        o = acc_ref[h, 0:HEAD_DIM, :] / acc_ref[h, HEAD_DIM:HEAD_DIM + 1, :]
        o_ref[0, :, h * HEAD_DIM:(h + 1) * HEAD_DIM] = o.T.astype(o_ref.dtype)


def _dsa_prompt(q, qi, wi_t, k, kidx, v, tq=256, tk=256):
    b, t, aw = q.shape
    ones = jnp.ones((b, N_HEADS, ONES_ROWS, t), v.dtype)
    v_t = jnp.concatenate([jnp.transpose(v.reshape(b, t, N_HEADS, HEAD_DIM), (0, 2, 3, 1)), ones], axis=2)
    v_t = v_t.reshape(b, N_HEADS * (HEAD_DIM + ONES_ROWS), t)
    tq = _tile(t, tq)
    tk = _tile(tq, tk)
    topk = min(TOPK_MAX, t // 4)
    once = pl.Buffered(1)
    return pl.pallas_call(
        functools.partial(_dsa_kernel, tq=tq, tk=tk, topk=topk,
                          wscale=IDX_HEADS ** -0.5 * IDX_DIM ** -0.5),
        grid=(b, t // tq),
        in_specs=[
            pl.BlockSpec((1, tq, aw), lambda bi, i: (bi, i, 0)),
            pl.BlockSpec((1, tq, IDX_HEADS * IDX_DIM), lambda bi, i: (bi, i, 0)),
            pl.BlockSpec((1, IDX_HEADS, tq), lambda bi, i: (bi, 0, i)),
            pl.BlockSpec((1, t, aw), lambda bi, i: (bi, 0, 0), pipeline_mode=once),
            pl.BlockSpec((1, t, IDX_DIM), lambda bi, i: (bi, 0, 0), pipeline_mode=once),
            pl.BlockSpec((1, v_t.shape[1], t), lambda bi, i: (bi, 0, 0), pipeline_mode=once),
        ],
        out_specs=pl.BlockSpec((1, tq, aw), lambda bi, i: (bi, i, 0)),
        out_shape=jax.ShapeDtypeStruct((b, t, aw), BF16),
        scratch_shapes=[
            pltpu.VMEM((t + tk, tq), F32),
            pltpu.VMEM((N_HEADS, SUBLANES, tq), F32),
            pltpu.VMEM((N_HEADS, HEAD_DIM + ONES_ROWS, tq), F32),
            pltpu.VMEM((N_HEADS, tk, tq), F32),
            pltpu.VMEM((N_HEADS, tk, tq), BF16),
        ],
        compiler_params=_params("parallel", "arbitrary"),
        name="dsa_prompt",
    )(q, qi, wi_t, k, kidx, v_t)


def _memattn_kernel(qm_ref, mk_ref, mv_ref, o_ref):
    scale = MEM_HEAD_DIM ** -0.5
    for h in range(MEM_HEADS):
        hs = slice(h * MEM_HEAD_DIM, (h + 1) * MEM_HEAD_DIM)
        s = lax.dot_general(qm_ref[0, :, hs], mk_ref[0, :, hs].astype(BF16), NT_DIMS,
                            preferred_element_type=F32) * scale
        e = jnp.exp(s - jnp.max(s, axis=-1, keepdims=True))
        p = e / jnp.sum(e, axis=-1, keepdims=True)
        o = jnp.dot(p.astype(BF16), mv_ref[0, :, hs].astype(BF16), preferred_element_type=F32)
        o_ref[0, :, hs] = o.astype(o_ref.dtype)


def _mem_attend(qm, mk, mv):
    b, t, mw = qm.shape
    n_mem = mk.shape[1]
    tm = _tile(t, 512)
    return pl.pallas_call(
        _memattn_kernel,
        grid=(b, t // tm),
        in_specs=[
            pl.BlockSpec((1, tm, mw), lambda bi, i: (bi, i, 0)),
            pl.BlockSpec((1, n_mem, mw), lambda bi, i: (bi, 0, 0)),
            pl.BlockSpec((1, n_mem, mw), lambda bi, i: (bi, 0, 0)),
        ],
        out_specs=pl.BlockSpec((1, tm, mw), lambda bi, i: (bi, i, 0)),
        out_shape=jax.ShapeDtypeStruct((b, t, mw), BF16),
        compiler_params=_params("parallel", "parallel"),
        name="mem_attend",
    )(qm, mk, mv)


def _strict_triangle(n, lower):
    r = lax.broadcasted_iota(jnp.int32, (n, n), 0)
    c = lax.broadcasted_iota(jnp.int32, (n, n), 1)
    return jnp.where((c < r) if lower else (r < c), 1.0, 0.0).astype(BF16)


def _index_order_rank(flag):
    rows = flag.shape[0]
    within = jnp.dot(flag.astype(BF16), _strict_triangle(LANES, False), preferred_element_type=F32)
    per_row = jnp.broadcast_to(jnp.sum(flag, axis=1, keepdims=True), flag.shape)
    before = jnp.dot(_strict_triangle(rows, True), per_row.astype(BF16), preferred_element_type=F32)
    return before + within


def _sindex_kernel(pg_ref, qi_ref, wi_ref, knew_ref, cache_ref, idx_ref, kbuf, sem, sc_ref, pos_ref,
                   *, n_pages, topk, wscale, group):
    b = pl.program_id(0)
    nb = pl.num_programs(0)
    slot = lax.rem(b, 2)

    def page_copy(bb, p, sl):
        return pltpu.make_async_copy(cache_ref.at[pg_ref[bb, p]], kbuf.at[sl, p], sem.at[sl])

    def start_all(bb, sl):
        lax.fori_loop(0, n_pages, lambda p, c: (page_copy(bb, p, sl).start(), c)[1], 0)

    @pl.when(b == 0)
    def _():
        start_all(b, slot)

    @pl.when(b + 1 < nb)
    def _():
        start_all(b + 1, 1 - slot)

    lax.fori_loop(0, n_pages, lambda p, c: (page_copy(b, p, slot).wait(), c)[1], 0)

    qi = qi_ref[0]
    w = wi_ref[0] * wscale
    for g in range(n_pages // group):
        kc = kbuf[slot, g * group:(g + 1) * group].reshape(group * PAGE_SIZE, IDX_DIM).astype(BF16)
        s = lax.dot_general(qi, kc, NT_DIMS, preferred_element_type=F32)
        r = jnp.sum(jnp.maximum(s, 0.0) * w, axis=0, keepdims=True)
        for j in range(group):
            sc_ref[g * group + j:g * group + j + 1, :] = r[:, j * PAGE_SIZE:(j + 1) * PAGE_SIZE]
    s_new = jnp.sum(qi.astype(F32) * knew_ref[0].astype(BF16).astype(F32), axis=1, keepdims=True)
    s_new = jnp.sum(jnp.maximum(s_new, 0.0) * w, axis=0, keepdims=True)
    pad_rows = sc_ref.shape[0] - n_pages
    first = (lax.broadcasted_iota(jnp.int32, (pad_rows, LANES), 0) == 0) & (
        lax.broadcasted_iota(jnp.int32, (pad_rows, LANES), 1) == 0)
    sc_ref[n_pages:, :] = jnp.where(first, s_new, -jnp.inf)

    s = sc_ref[...]

    def count(hit):
        return jnp.sum(jnp.sum(jnp.where(hit, 1.0, 0.0), axis=1, keepdims=True), axis=0, keepdims=True)

    thr = _key_to_float(_kth_largest_key(lambda cand: count(s >= _key_to_float(cand)), topk,
                                         jnp.zeros((1, 1), jnp.int32)))
    above = s > thr
    tied = jnp.where(s == thr, 1.0, 0.0)
    keep_tied = (tied > 0.5) & (_index_order_rank(tied) < topk - count(above))
    chosen = jnp.where(above | keep_tied, 1.0, 0.0)
    pos_ref[...] = jnp.where(chosen > 0.5, _index_order_rank(chosen), -1.0)

    slot_id = lax.broadcasted_iota(jnp.int32, (topk, LANES), 0).astype(F32)
    lane = lax.broadcasted_iota(jnp.int32, (1, LANES), 1)

    def place(p, acc):
        key_pos = (p * PAGE_SIZE + lane).astype(F32)
        return acc + jnp.where(pos_ref[pl.ds(p, 1), :] == slot_id, key_pos, 0.0)

    acc = lax.fori_loop(0, n_pages + 1, place, jnp.zeros((topk, LANES), F32))
    idx_ref[0] = jnp.sum(acc, axis=1, keepdims=True).astype(jnp.int32)


def _sample_indices(qi, wi, kidx_new, cache_kidx, pages):
    db, n_pages = pages.shape
    past = n_pages * PAGE_SIZE
    topk = min(TOPK_MAX, (past + 1) // 4)
    assert topk % SUBLANES == 0
    group = 16 if n_pages % 16 == 0 else 1
    rows = (n_pages + 1 + SUBLANES - 1) // SUBLANES * SUBLANES
    grid_spec = pltpu.PrefetchScalarGridSpec(
        num_scalar_prefetch=1,
        grid=(db,),
        in_specs=[
            pl.BlockSpec((1, IDX_HEADS, IDX_DIM), lambda b, pg: (b, 0, 0)),
            pl.BlockSpec((1, IDX_HEADS, 1), lambda b, pg: (b, 0, 0)),
            pl.BlockSpec((1, 1, IDX_DIM), lambda b, pg: (b, 0, 0)),
            pl.BlockSpec(memory_space=pl.ANY),
        ],
        out_specs=pl.BlockSpec((1, topk, 1), lambda b, pg: (b, 0, 0)),
        scratch_shapes=[
            pltpu.VMEM((2, n_pages, PAGE_SIZE, IDX_DIM), F32),
            pltpu.SemaphoreType.DMA((2,)),
            pltpu.VMEM((rows, LANES), F32),
            pltpu.VMEM((rows, LANES), F32),
        ],
    )
    idx = pl.pallas_call(
        functools.partial(_sindex_kernel, n_pages=n_pages, topk=topk,
                          wscale=IDX_HEADS ** -0.5 * IDX_DIM ** -0.5, group=group),
        grid_spec=grid_spec,
        out_shape=jax.ShapeDtypeStruct((db, topk, 1), jnp.int32),
        compiler_params=_params("arbitrary"),
        name="sample_indices",
    )(pages, qi.reshape(db, IDX_HEADS, IDX_DIM), wi.reshape(db, IDX_HEADS, 1),
      kidx_new.reshape(db, 1, IDX_DIM), cache_kidx.reshape(-1, PAGE_SIZE, IDX_DIM))
    return idx.reshape(db, topk)


def _sgather_kernel(idx_ref, pg_ref, q_ref, knew_ref, vnew_ref, ck_ref, cv_ref, o_ref, kbuf, vbuf, sem,
                    *, past, topk):
    b = pl.program_id(0)
    nb = pl.num_programs(0)
    slot = lax.rem(b, 2)

    def start_all(bb, sl):
        def body(j, c):
            i = idx_ref[bb, j]
            ip = jnp.minimum(i, past - 1)
            page = pg_ref[bb, ip // PAGE_SIZE]
            off = lax.rem(ip, PAGE_SIZE)

            @pl.when(i >= past)
            def _():
                pltpu.make_async_copy(knew_ref.at[bb], kbuf.at[sl, j], sem.at[0, sl]).start()
                pltpu.make_async_copy(vnew_ref.at[bb], vbuf.at[sl, j], sem.at[1, sl]).start()

            @pl.when(i < past)
            def _():
                pltpu.make_async_copy(ck_ref.at[page, off], kbuf.at[sl, j], sem.at[0, sl]).start()
                pltpu.make_async_copy(cv_ref.at[page, off], vbuf.at[sl, j], sem.at[1, sl]).start()

            return c

        lax.fori_loop(0, topk, body, 0)

    @pl.when(b == 0)
    def _():
        start_all(b, slot)

    @pl.when(b + 1 < nb)
    def _():
        start_all(b + 1, 1 - slot)

    def wait_body(j, c):
        pltpu.make_async_copy(knew_ref.at[b], kbuf.at[slot, j], sem.at[0, slot]).wait()
        pltpu.make_async_copy(vnew_ref.at[b], vbuf.at[slot, j], sem.at[1, slot]).wait()
        return c

    lax.fori_loop(0, topk, wait_body, 0)

    k = kbuf[slot]
    v = vbuf[slot]
    s = jnp.sum(k * q_ref[...], axis=2, keepdims=True) * HEAD_DIM ** -0.5
    e = jnp.exp(s - jnp.max(s, axis=0, keepdims=True))
    p = e / jnp.sum(e, axis=0, keepdims=True)
    o_ref[...] = jnp.sum(p * v, axis=0, keepdims=True).astype(o_ref.dtype)


def _sample_attend(q, k_new, v_new, idx, cache_k, cache_v, pages):
    db, n_pages = pages.shape
    topk = idx.shape[1]
    tile = (N_HEADS, HEAD_DIM)
    grid_spec = pltpu.PrefetchScalarGridSpec(
        num_scalar_prefetch=2,
        grid=(db,),
        in_specs=[
            pl.BlockSpec((1,) + tile, lambda b, ix, pg: (b, 0, 0)),
            pl.BlockSpec(memory_space=pl.ANY),
            pl.BlockSpec(memory_space=pl.ANY),
            pl.BlockSpec(memory_space=pl.ANY),
            pl.BlockSpec(memory_space=pl.ANY),
        ],
        out_specs=pl.BlockSpec((1,) + tile, lambda b, ix, pg: (b, 0, 0)),
        scratch_shapes=[
            pltpu.VMEM((2, topk) + tile, F32),
            pltpu.VMEM((2, topk) + tile, F32),
            pltpu.SemaphoreType.DMA((2, 2)),
        ],
    )
    out = pl.pallas_call(
        functools.partial(_sgather_kernel, past=n_pages * PAGE_SIZE, topk=topk),
        grid_spec=grid_spec,
        out_shape=jax.ShapeDtypeStruct((db,) + tile, BF16),
        compiler_params=_params("arbitrary"),
        name="sample_attend",
    )(idx, pages, q.reshape((db,) + tile), k_new.reshape((db,) + tile), v_new.reshape((db,) + tile),
      cache_k.reshape((-1, PAGE_SIZE) + tile), cache_v.reshape((-1, PAGE_SIZE) + tile))
    return out.reshape(db, N_HEADS * HEAD_DIM)


def _merge_kernel(x_ref, g_ref, ya_ref, yb_ref, ym_ref, woa_ref, wob_ref, wom_ref, wout_ref,
                  gpm_ref, gpf_ref, x1_ref, h2_ref):
    d = x_ref.shape[1]
    merged = (g_ref[:, 0:d].astype(F32) * jnp.dot(ya_ref[...], woa_ref[...], preferred_element_type=F32)
              + g_ref[:, d:2 * d].astype(F32) * jnp.dot(yb_ref[...], wob_ref[...], preferred_element_type=F32)
              + g_ref[:, 2 * d:3 * d].astype(F32) * jnp.dot(ym_ref[...], wom_ref[...], preferred_element_type=F32))
    o = jnp.dot(merged.astype(BF16), wout_ref[...], preferred_element_type=F32)
    x1 = x_ref[...] + _rmsnorm_rows(o, gpm_ref[...])
    x1_ref[...] = x1
    h2_ref[...] = _rmsnorm_rows(x1, gpf_ref[...]).astype(h2_ref.dtype)


def _merge(x, gates, y_a, y_b, y_m, w_oa, w_ob, w_om, w_out, g_post_mix, g_pre_ffn):
    m, d = x.shape
    tm = _tile(m, 256)
    once = pl.Buffered(1)

    def rows(width):
        return pl.BlockSpec((tm, width), lambda i: (i, 0))

    def whole(a):
        return pl.BlockSpec(a.shape, lambda i: (0, 0), pipeline_mode=once)

    return pl.pallas_call(
        _merge_kernel,
        grid=(m // tm,),
        in_specs=[rows(d), rows(N_BRANCH * d), rows(y_a.shape[1]), rows(y_b.shape[1]), rows(y_m.shape[1]),
                  whole(w_oa), whole(w_ob), whole(w_om), whole(w_out),
                  pl.BlockSpec((1, d), lambda i: (0, 0)), pl.BlockSpec((1, d), lambda i: (0, 0))],
        out_specs=[rows(d), rows(d)],
        out_shape=[jax.ShapeDtypeStruct((m, d), F32), jax.ShapeDtypeStruct((m, d), BF16)],
        compiler_params=_params("parallel"),
        name="merge",
    )(x, gates, y_a, y_b, y_m, w_oa, w_ob, w_om, w_out, g_post_mix.reshape(1, d), g_pre_ffn.reshape(1, d))


def _ffn_kernel(h_ref, wg_ref, wu_ref, wd_ref, x1_ref, gpf_ref, o_ref):
    j = pl.program_id(1)

    @pl.when(j == 0)
    def _():
        o_ref[...] = jnp.zeros(o_ref.shape, F32)

    h = h_ref[...]
    a = jnp.dot(h, wg_ref[...], preferred_element_type=F32)
    c = jnp.dot(h, wu_ref[...], preferred_element_type=F32)
    act = (jax.nn.silu(a) * c).astype(BF16)
    o_ref[...] += jnp.dot(act, wd_ref[...], preferred_element_type=F32)

    @pl.when(j == pl.num_programs(1) - 1)
    def _():
        o_ref[...] = x1_ref[...] + _rmsnorm_rows(o_ref[...], gpf_ref[...])


def _ffn(h2, x1, w_gu, w_down, g_post_ffn):
    m, d = h2.shape
    d_ff = w_down.shape[0]
    tm = _tile(m, 1024)
    tf = _tile(d_ff, 512)
    nf = d_ff // tf
    once = pl.Buffered(1)
    return pl.pallas_call(
        _ffn_kernel,
        grid=(m // tm, nf),
        in_specs=[
            pl.BlockSpec((tm, d), lambda i, j: (i, 0), pipeline_mode=once),
            pl.BlockSpec((d, tf), lambda i, j: (0, j)),
            pl.BlockSpec((d, tf), lambda i, j: (0, j + nf)),
            pl.BlockSpec((tf, d), lambda i, j: (j, 0)),
            pl.BlockSpec((tm, d), lambda i, j: (i, 0), pipeline_mode=once),
            pl.BlockSpec((1, d), lambda i, j: (0, 0)),
        ],
        out_specs=pl.BlockSpec((tm, d), lambda i, j: (i, 0)),
        out_shape=jax.ShapeDtypeStruct((m, d), F32),
        compiler_params=_params("parallel", "arbitrary"),
        name="ffn",
    )(h2, w_gu, w_gu, w_down, x1, g_post_ffn.reshape(1, d))


def _split_w_in(w_in, d):
    gw = d // 2
    aw = N_HEADS * HEAD_DIM
    iw = IDX_HEADS * IDX_DIM
    mw = MEM_HEADS * MEM_HEAD_DIM
    sizes = (2 * gw, aw, aw, aw, iw, IDX_DIM + IDX_HEADS, mw, N_BRANCH * d)
    w_t = w_in.T
    parts, start = [], 0
    for s in sizes:
        parts.append(w_t[start:start + s].astype(BF16))
        start += s
    w_uv, w_q, w_k, w_v, w_qi, w_kw, w_qm, w_g = parts
    w_kw = jnp.pad(w_kw, ((0, 2 * LANES - w_kw.shape[0]), (0, 0)))
    return w_uv, w_q, w_k, w_v, w_qi, w_kw, w_qm, w_g


def _project_common(x2d, g_pre_mix, ws, ln_g, ln_b, uv_dtype, q_dtype, q_scale):
    w_uv, w_q, w_k, w_v, w_qi, w_kw, w_qm, w_g = ws
    h = _rmsnorm(x2d, g_pre_mix, BF16)
    uv = _project_uv(h, w_uv, ln_g, ln_b, uv_dtype)
    (q,) = _matmul(h, w_q, (q_dtype,), scale=q_scale, name="proj_q")
    k, k_bf = _matmul(h, w_k, (F32, BF16), name="proj_k")
    v, v_bf = _matmul(h, w_v, (F32, BF16), name="proj_v")
    (qi,) = _matmul(h, w_qi, (BF16,), name="proj_qi")
    (kw,) = _matmul(h, w_kw, (F32,), name="proj_kidx")
    (qm,) = _matmul(h, w_qm, (BF16,), name="proj_qm")
    (gates,) = _matmul(h, w_g, (BF16,), act="sigmoid", name="proj_gates")
    kidx = kw[:, :IDX_DIM]
    wi = kw[:, IDX_DIM:IDX_DIM + IDX_HEADS]
    return uv, q, k, k_bf, v, v_bf, qi, kidx, wi, qm, gates


def kernel(x_prompt, x_sample, mem_prompt, cache_k, cache_v, cache_kidx, cache_mem_k, cache_mem_v, page_table, g_pre_mix, g_post_mix, g_pre_ffn, g_post_ffn, g_mem, ln_v_g, ln_v_b, w_in, w_s, b_s, w_oa, w_ob, w_om, w_out, w_mem_kv, w_gu, w_down):
    depth = w_in.shape[0]
    b, t, d = x_prompt.shape
    db, dt, _ = x_sample.shape
    assert dt == 1, "sample path handles one new token per sequence"
    n_mem = mem_prompt.shape[1]
    aw = N_HEADS * HEAD_DIM
    mw = MEM_HEADS * MEM_HEAD_DIM
    gw = d // 2
    past = page_table.shape[1] * PAGE_SIZE

    yp = x_prompt.reshape(b * t, d)
    ys = x_sample.reshape(db * dt, d)
    outs = [[] for _ in range(9)]
    for l in range(depth):
        ws = _split_w_in(w_in[l], d)
        woa, wob, wom, wout = (w.astype(BF16) for w in (w_oa[l], w_ob[l], w_om[l], w_out[l]))
        wgu, wdown, wmem = w_gu[l].astype(BF16), w_down[l].astype(BF16), w_mem_kv[l].T.astype(BF16)

        uv, q, k, k_bf, v, v_bf, qi, kidx, wi, qm, gates = _project_common(
            yp, g_pre_mix[l], ws, ln_v_g[l], ln_v_b[l], BF16, BF16, DSA_LOG2_SCALE)
        y_a = _gmlp_spatial(uv, w_s[l], b_s[l])
        y_b = _dsa_prompt(
            q.reshape(b, t, aw), qi.reshape(b, t, -1), jnp.swapaxes(wi.reshape(b, t, IDX_HEADS), 1, 2),
            k_bf.reshape(b, t, aw), kidx.astype(BF16).reshape(b, t, IDX_DIM),
            v_bf.reshape(b, t, aw)).reshape(b * t, aw)
        hm = _rmsnorm(mem_prompt.reshape(b * n_mem, d), g_mem[l], BF16)
        (mkv,) = _matmul(hm, wmem, (F32,), name="proj_mem_kv")
        mk = mkv[:, :mw].reshape(b, n_mem, mw)
        mv = mkv[:, mw:].reshape(b, n_mem, mw)
        y_m = _mem_attend(qm.reshape(b, t, mw), mk, mv).reshape(b * t, mw)
        x1, h2 = _merge(yp, gates, y_a, y_b, y_m, woa, wob, wom, wout, g_post_mix[l], g_pre_ffn[l])
        yp = _ffn(h2, x1, wgu, wdown, g_post_ffn[l])
        outs[0].append(k.reshape(b, t, N_HEADS, HEAD_DIM))
        outs[1].append(v.reshape(b, t, N_HEADS, HEAD_DIM))
        outs[2].append(kidx.reshape(b, t, IDX_DIM))
        outs[3].append(mk.reshape(b, n_mem, MEM_HEADS, MEM_HEAD_DIM))
        outs[4].append(mv.reshape(b, n_mem, MEM_HEADS, MEM_HEAD_DIM))

        uv, q, k, k_bf, v, v_bf, qi, kidx, wi, qm, gates = _project_common(
            ys, g_pre_mix[l], ws, ln_v_g[l], ln_v_b[l], F32, F32, None)
        y_a = _gmlp_first_row(uv, w_s[l], b_s[l])
        pages = page_table + l * cache_k.shape[1]
        idx = _sample_indices(qi, wi, kidx, cache_kidx, pages)
        y_b = _sample_attend(q, k, v, idx, cache_k, cache_v, pages)
        qm_pad = jnp.pad(qm.reshape(db, 1, mw), ((0, 0), (0, 2 * SUBLANES - 1), (0, 0)))
        y_m = _mem_attend(qm_pad, cache_mem_k[l].reshape(db, n_mem, mw),
                          cache_mem_v[l].reshape(db, n_mem, mw))[:, 0, :]
        x1, h2 = _merge(ys, gates, y_a, y_b, y_m, woa, wob, wom, wout, g_post_mix[l], g_pre_ffn[l])
        ys = _ffn(h2, x1, wgu, wdown, g_post_ffn[l])
        outs[5].append(k.reshape(db, dt, N_HEADS, HEAD_DIM))
        outs[6].append(v.reshape(db, dt, N_HEADS, HEAD_DIM))
        outs[7].append(kidx.reshape(db, dt, IDX_DIM))
        outs[8].append(uv[:, gw:].reshape(db, dt, gw))

    return (yp.reshape(b, t, d), ys.reshape(db, dt, d)) + tuple(jnp.stack(o) for o in outs)
```

```python
import functools

import jax
import jax.numpy as jnp
from jax import lax
from jax.experimental import pallas as pl
from jax.experimental.pallas import tpu as pltpu

EPS = 1e-6
CHUNK = 128
GM_GROUPS = 8
N_HEADS = 8
HEAD_DIM = 128
IDX_HEADS = 16
IDX_DIM = 128
TOPK_MAX = 256
MEM_HEADS = 4
MEM_HEAD_DIM = 128
N_BRANCH = 3
PAGE_SIZE = 128

LANES = 128
SUBLANES = 8
VMEM_LIMIT_BYTES = 56 * 2**20
INT_MIN = -2**31
KEY_LOWEST = INT_MIN + 0x800000
KEY_INF = 0x7F800000
NEG = -1e30
LOG2_E = 1.4426950408889634
DSA_LOG2_SCALE = HEAD_DIM ** -0.5 * LOG2_E
ONES_ROWS = 16
COUNT_ROWS = 64
MM_TM, MM_TN = 2048, 1024

F32 = jnp.float32
BF16 = jnp.bfloat16
NT_DIMS = (((1,), (1,)), ((), ()))


def _params(*sem):
    return pltpu.CompilerParams(dimension_semantics=sem, vmem_limit_bytes=VMEM_LIMIT_BYTES)


def _tile(n, pref):
    if n <= pref:
        return n
    t = pref
    while n % t:
        t //= 2
    return t


def _key_to_float(key):
    key = jnp.clip(key, KEY_LOWEST, KEY_INF)
    return pltpu.bitcast(key ^ ((key >> 31) & 0x7FFFFFFF), F32)


def _kth_largest_key(count_ge, topk, like):
    base = jnp.where(count_ge(jnp.zeros_like(like)) >= topk, 0, INT_MIN).astype(jnp.int32)

    def bit_body(b, t):
        cand = t | jnp.left_shift(jnp.int32(1), 30 - b)
        return jnp.where(count_ge(cand) >= topk, cand, t)

    return lax.fori_loop(0, 31, bit_body, base)


def _rmsnorm_rows(x, g):
    return x * lax.rsqrt(jnp.mean(x * x, axis=-1, keepdims=True) + EPS) * g


def _rmsnorm_kernel(x_ref, g_ref, o_ref):
    o_ref[...] = _rmsnorm_rows(x_ref[...], g_ref[...]).astype(o_ref.dtype)


def _rmsnorm(x, g, out_dtype):
    m, d = x.shape
    tm = _tile(m, 512)
    return pl.pallas_call(
        _rmsnorm_kernel,
        grid=(m // tm,),
        in_specs=[pl.BlockSpec((tm, d), lambda i: (i, 0)), pl.BlockSpec((1, d), lambda i: (0, 0))],
        out_specs=pl.BlockSpec((tm, d), lambda i: (i, 0)),
        out_shape=jax.ShapeDtypeStruct((m, d), out_dtype),
        compiler_params=_params("parallel"),
        name="rmsnorm",
    )(x, g.reshape(1, d))


def _mm_kernel(x_ref, w_ref, *o_refs, act, scale):
    acc = lax.dot_general(x_ref[...], w_ref[...], NT_DIMS, preferred_element_type=F32)
    if act == "sigmoid":
        acc = jax.nn.sigmoid(acc)
    if scale is not None:
        acc = acc * scale
    for o_ref in o_refs:
        o_ref[...] = acc.astype(o_ref.dtype)


def _matmul(x, w_t, out_dtypes, act=None, scale=None, name="matmul"):
    m, k = x.shape
    n = w_t.shape[0]
    tm = _tile(m, MM_TM)
    tn = _tile(n, MM_TN)
    outs = pl.pallas_call(
        functools.partial(_mm_kernel, act=act, scale=scale),
        grid=(m // tm, n // tn),
        in_specs=[pl.BlockSpec((tm, k), lambda i, j: (i, 0)), pl.BlockSpec((tn, k), lambda i, j: (j, 0))],
        out_specs=[pl.BlockSpec((tm, tn), lambda i, j: (i, j)) for _ in out_dtypes],
        out_shape=[jax.ShapeDtypeStruct((m, n), dt) for dt in out_dtypes],
        compiler_params=_params("parallel", "parallel"),
        name=name,
    )(x, w_t)
    return outs


def _uv_kernel(x_ref, w_ref, lng_ref, lnb_ref, o_ref):
    y = jax.nn.gelu(lax.dot_general(x_ref[...], w_ref[...], NT_DIMS, preferred_element_type=F32))

    @pl.when(pl.program_id(1) == 0)
    def _():
        o_ref[...] = y.astype(o_ref.dtype)

    @pl.when(pl.program_id(1) == 1)
    def _():
        yc = y - jnp.mean(y, axis=-1, keepdims=True)
        yn = yc * lax.rsqrt(jnp.mean(yc * yc, axis=-1, keepdims=True) + EPS)
        o_ref[...] = (yn * lng_ref[...] + lnb_ref[...]).astype(o_ref.dtype)


def _project_uv(h, w_uv_t, ln_g, ln_b, out_dtype):
    m, k = h.shape
    gw = w_uv_t.shape[0] // 2
    tm = _tile(m, 1024)
    return pl.pallas_call(
        _uv_kernel,
        grid=(m // tm, 2),
        in_specs=[
            pl.BlockSpec((tm, k), lambda i, j: (i, 0)),
            pl.BlockSpec((gw, k), lambda i, j: (j, 0)),
            pl.BlockSpec((1, gw), lambda i, j: (0, 0)),
            pl.BlockSpec((1, gw), lambda i, j: (0, 0)),
        ],
        out_specs=pl.BlockSpec((tm, gw), lambda i, j: (i, j)),
        out_shape=jax.ShapeDtypeStruct((m, 2 * gw), out_dtype),
        compiler_params=_params("parallel", "arbitrary"),
        name="project_uv",
    )(h, w_uv_t, ln_g.reshape(1, gw), ln_b.reshape(1, gw))


def _gmlp_kernel(u_ref, v_ref, ws_ref, bt_ref, o_ref, *, n_chunks):
    c = CHUNK
    gd = u_ref.shape[1] // GM_GROUPS
    causal = lax.broadcasted_iota(jnp.int32, (c, c), 1) <= lax.broadcasted_iota(jnp.int32, (c, c), 0)
    for g in range(GM_GROUPS):
        w = jnp.where(causal, ws_ref[g], 0.0).astype(v_ref.dtype)
        bcol = bt_ref[:, g:g + 1]
        for ci in range(n_chunks):
            rows = slice(ci * c, (ci + 1) * c)
            cols = slice(g * gd, (g + 1) * gd)
            mixed = jnp.dot(w, v_ref[rows, cols], preferred_element_type=F32) + bcol
            o_ref[rows, cols] = (u_ref[rows, cols].astype(F32) * mixed).astype(o_ref.dtype)


def _gmlp_spatial(uv, w_s, b_s):
    m = uv.shape[0]
    gw = uv.shape[1] // 2
    tm = _tile(m, 4 * CHUNK)
    return pl.pallas_call(
        functools.partial(_gmlp_kernel, n_chunks=tm // CHUNK),
        grid=(m // tm,),
        in_specs=[
            pl.BlockSpec((tm, gw), lambda i: (i, 0)),
            pl.BlockSpec((tm, gw), lambda i: (i, 1)),
            pl.BlockSpec((GM_GROUPS, CHUNK, CHUNK), lambda i: (0, 0, 0)),
            pl.BlockSpec((CHUNK, GM_GROUPS), lambda i: (0, 0)),
        ],
        out_specs=pl.BlockSpec((tm, gw), lambda i: (i, 0)),
        out_shape=jax.ShapeDtypeStruct((m, gw), BF16),
        compiler_params=_params("parallel"),
        name="gmlp_spatial",
    )(uv, uv, w_s, b_s.T)


def _gmlp_first_kernel(u_ref, v_ref, w_ref, b_ref, o_ref):
    o_ref[...] = (u_ref[...] * (w_ref[...] * v_ref[...] + b_ref[...])).astype(o_ref.dtype)


def _gmlp_first_row(uv, w_s, b_s):
    m = uv.shape[0]
    gw = uv.shape[1] // 2
    gd = gw // GM_GROUPS
    w_row = jnp.repeat(w_s[:, 0, 0], gd).reshape(1, gw)
    b_row = jnp.repeat(b_s[:, 0], gd).reshape(1, gw)
    return pl.pallas_call(
        _gmlp_first_kernel,
        grid=(1,),
        in_specs=[
            pl.BlockSpec((m, gw), lambda i: (0, 0)),
            pl.BlockSpec((m, gw), lambda i: (0, 1)),
            pl.BlockSpec((1, gw), lambda i: (0, 0)),
            pl.BlockSpec((1, gw), lambda i: (0, 0)),
        ],
        out_specs=pl.BlockSpec((m, gw), lambda i: (0, 0)),
        out_shape=jax.ShapeDtypeStruct((m, gw), BF16),
        compiler_params=_params("arbitrary"),
        name="gmlp_first_row",
    )(uv, uv, w_row, b_row)


def _dsa_kernel(q_ref, qi_ref, wit_ref, qin_ref, witn_ref, k_ref, kidx_ref, vt_ref, o_ref,
                sc_ref, m_ref, acc_ref, s_ref, p_ref, *, tq, tk, topk, wscale):
    i = pl.program_id(1)
    last = pl.num_programs(1) - 1
    cur = lax.rem(i, 2)
    n_chunks = ((i + 1) * tq + tk - 1) // tk
    n_next = ((i + 2) * tq + tk - 1) // tk
    vrows = vt_ref.shape[1] // N_HEADS

    def idx_chunk(c, qi_r, wit_r, blk, slot):
        k0 = pl.multiple_of(c * tk, tk)
        kc = kidx_ref[0, pl.ds(k0, tk), :]
        acc = jnp.zeros((tk, tq), F32)
        for h in range(IDX_HEADS):
            s = lax.dot_general(kc, qi_r[0, :, h * IDX_DIM:(h + 1) * IDX_DIM], NT_DIMS,
                                preferred_element_type=F32)
            acc = acc + jnp.maximum(s, 0.0) * (wit_r[0, h:h + 1, :] * wscale)
        kpos = k0 + lax.broadcasted_iota(jnp.int32, (tk, 1), 0)
        qpos = blk * tq + lax.broadcasted_iota(jnp.int32, (1, tq), 1)
        sc_ref[slot, pl.ds(k0, tk), :] = jnp.where(kpos <= qpos, acc, -jnp.inf)

    def pad_odd(n, slot):
        @pl.when(n % 2 == 1)
        def _():
            sc_ref[slot, pl.ds(pl.multiple_of(n * tk, tk), tk), :] = jnp.full((tk, tq), -jnp.inf, F32)

    @pl.when(i == 0)
    def _():
        def body(c, carry):
            idx_chunk(c, qi_ref, wit_ref, i, cur)
            return carry

        lax.fori_loop(0, n_chunks, body, 0)
        pad_odd(n_chunks, cur)

    def count_ge(cand):
        cand_f = _key_to_float(cand)

        def body(c, cnt):
            k0 = pl.multiple_of(c * (2 * tk), 2 * tk)
            hit = jnp.where(sc_ref[cur, pl.ds(k0, 2 * tk), :] >= cand_f, 1.0, 0.0)
            return cnt + jnp.sum(hit.reshape(2 * tk // COUNT_ROWS, COUNT_ROWS, tq), axis=0)

        cnt = lax.fori_loop(0, (n_chunks + 1) // 2, body, jnp.zeros((COUNT_ROWS, tq), F32))
        return jnp.sum(cnt, axis=0, keepdims=True)

    thr = _key_to_float(_kth_largest_key(count_ge, topk, jnp.zeros((1, tq), jnp.int32)))

    m_ref[...] = jnp.full(m_ref.shape, NEG, F32)
    acc_ref[...] = jnp.zeros(acc_ref.shape, F32)

    def att_chunk(c):
        k0 = pl.multiple_of(c * tk, tk)
        bias = jnp.where(sc_ref[cur, pl.ds(k0, tk), :] >= thr, 0.0, NEG)
        m_old, m_new = [], []
        for h in range(N_HEADS):
            hs = slice(h * HEAD_DIM, (h + 1) * HEAD_DIM)
            s = lax.dot_general(k_ref[0, pl.ds(k0, tk), hs], q_ref[0, :, hs], NT_DIMS,
                                preferred_element_type=F32) + bias
            s_ref[h] = s
            m_old.append(m_ref[h])
            m_new.append(jnp.maximum(m_old[h], jnp.max(s, axis=0, keepdims=True)))
            m_ref[h] = m_new[h]
        for h in range(N_HEADS):
            p_ref[h] = jnp.exp2(s_ref[h] - m_new[h][0:1, :]).astype(BF16)
        for h in range(N_HEADS):
            pv = jnp.dot(vt_ref[0, h * vrows:(h + 1) * vrows, pl.ds(k0, tk)], p_ref[h],
                         preferred_element_type=F32)
            acc_ref[h] = jnp.exp2(m_old[h] - m_new[h])[0:1, :] * acc_ref[h] + pv

    @pl.when(i < last)
    def _():
        def fused(c, carry):
            att_chunk(c)
            idx_chunk(c, qin_ref, witn_ref, i + 1, 1 - cur)
            return carry

        def tail(c, carry):
            idx_chunk(c, qin_ref, witn_ref, i + 1, 1 - cur)
            return carry

        lax.fori_loop(0, n_chunks, fused, 0)
        lax.fori_loop(n_chunks, n_next, tail, 0)
        pad_odd(n_next, 1 - cur)

    @pl.when(i == last)
    def _():
        def body(c, carry):
            att_chunk(c)
            return carry

        lax.fori_loop(0, n_chunks, body, 0)

    for h in range(N_HEADS):
        o = acc_ref[h, 0:HEAD_DIM, :] / acc_ref[h, HEAD_DIM:HEAD_DIM + 1, :]
        o_ref[0, :, h * HEAD_DIM:(h + 1) * HEAD_DIM] = o.T.astype(o_ref.dtype)


def _dsa_prompt(q, qi, wi_t, k, kidx, v, tq=256, tk=256):
    b, t, aw = q.shape
    ones = jnp.ones((b, N_HEADS, ONES_ROWS, t), v.dtype)
    v_t = jnp.concatenate([jnp.transpose(v.reshape(b, t, N_HEADS, HEAD_DIM), (0, 2, 3, 1)), ones], axis=2)
    v_t = v_t.reshape(b, N_HEADS * (HEAD_DIM + ONES_ROWS), t)
    tq = _tile(t, tq)
    tk = _tile(tq, tk)
    topk = min(TOPK_MAX, t // 4)
    once = pl.Buffered(1)
    nq = t // tq

    def next_block(i):
        return jnp.minimum(i + 1, nq - 1)

    return pl.pallas_call(
        functools.partial(_dsa_kernel, tq=tq, tk=tk, topk=topk,
                          wscale=IDX_HEADS ** -0.5 * IDX_DIM ** -0.5),
        grid=(b, nq),
        in_specs=[
            pl.BlockSpec((1, tq, aw), lambda bi, i: (bi, i, 0)),
            pl.BlockSpec((1, tq, IDX_HEADS * IDX_DIM), lambda bi, i: (bi, i, 0)),
            pl.BlockSpec((1, IDX_HEADS, tq), lambda bi, i: (bi, 0, i)),
            pl.BlockSpec((1, tq, IDX_HEADS * IDX_DIM), lambda bi, i: (bi, next_block(i), 0)),
            pl.BlockSpec((1, IDX_HEADS, tq), lambda bi, i: (bi, 0, next_block(i))),
            pl.BlockSpec((1, t, aw), lambda bi, i: (bi, 0, 0), pipeline_mode=once),
            pl.BlockSpec((1, t, IDX_DIM), lambda bi, i: (bi, 0, 0), pipeline_mode=once),
            pl.BlockSpec((1, v_t.shape[1], t), lambda bi, i: (bi, 0, 0), pipeline_mode=once),
        ],
        out_specs=pl.BlockSpec((1, tq, aw), lambda bi, i: (bi, i, 0)),
        out_shape=jax.ShapeDtypeStruct((b, t, aw), BF16),
        scratch_shapes=[
            pltpu.VMEM((2, t + tk, tq), F32),
            pltpu.VMEM((N_HEADS, SUBLANES, tq), F32),
            pltpu.VMEM((N_HEADS, HEAD_DIM + ONES_ROWS, tq), F32),
            pltpu.VMEM((N_HEADS, tk, tq), F32),
            pltpu.VMEM((N_HEADS, tk, tq), BF16),
        ],
        compiler_params=_params("parallel", "arbitrary"),
        name="dsa_prompt",
    )(q, qi, wi_t, qi, wi_t, k, kidx, v_t)


def _memattn_kernel(qm_ref, mk_ref, mv_ref, o_ref):
    scale = MEM_HEAD_DIM ** -0.5
    for h in range(MEM_HEADS):
        hs = slice(h * MEM_HEAD_DIM, (h + 1) * MEM_HEAD_DIM)
        s = lax.dot_general(qm_ref[0, :, hs], mk_ref[0, :, hs].astype(BF16), NT_DIMS,
                            preferred_element_type=F32) * scale
        e = jnp.exp(s - jnp.max(s, axis=-1, keepdims=True))
        p = e / jnp.sum(e, axis=-1, keepdims=True)
        o = jnp.dot(p.astype(BF16), mv_ref[0, :, hs].astype(BF16), preferred_element_type=F32)
        o_ref[0, :, hs] = o.astype(o_ref.dtype)


def _mem_attend(qm, mk, mv):
    b, t, mw = qm.shape
    n_mem = mk.shape[1]
    tm = _tile(t, 512)
    return pl.pallas_call(
        _memattn_kernel,
        grid=(b, t // tm),
        in_specs=[
            pl.BlockSpec((1, tm, mw), lambda bi, i: (bi, i, 0)),
            pl.BlockSpec((1, n_mem, mw), lambda bi, i: (bi, 0, 0)),
            pl.BlockSpec((1, n_mem, mw), lambda bi, i: (bi, 0, 0)),
        ],
        out_specs=pl.BlockSpec((1, tm, mw), lambda bi, i: (bi, i, 0)),
        out_shape=jax.ShapeDtypeStruct((b, t, mw), BF16),
        compiler_params=_params("parallel", "parallel"),
        name="mem_attend",
    )(qm, mk, mv)


def _strict_triangle(n, lower):
    r = lax.broadcasted_iota(jnp.int32, (n, n), 0)
    c = lax.broadcasted_iota(jnp.int32, (n, n), 1)
    return jnp.where((c < r) if lower else (r < c), 1.0, 0.0).astype(BF16)


def _index_order_rank(flag):
    rows = flag.shape[0]
    within = jnp.dot(flag.astype(BF16), _strict_triangle(LANES, False), preferred_element_type=F32)
    per_row = jnp.broadcast_to(jnp.sum(flag, axis=1, keepdims=True), flag.shape)
    before = jnp.dot(_strict_triangle(rows, True), per_row.astype(BF16), preferred_element_type=F32)
    return before + within


def _sindex_kernel(pg_ref, qi_ref, wi_ref, knew_ref, cache_ref, idx_ref, kbuf, sem, sc_ref, pos_ref,
                   *, n_pages, topk, wscale, group):
    b = pl.program_id(0)
    nb = pl.num_programs(0)
    slot = lax.rem(b, 2)

    def page_copy(bb, p, sl):
        return pltpu.make_async_copy(cache_ref.at[pg_ref[bb, p]], kbuf.at[sl, p], sem.at[sl])

    def start_all(bb, sl):
        lax.fori_loop(0, n_pages, lambda p, c: (page_copy(bb, p, sl).start(), c)[1], 0)

    @pl.when(b == 0)
    def _():
        start_all(b, slot)

    @pl.when(b + 1 < nb)
    def _():
        start_all(b + 1, 1 - slot)

    lax.fori_loop(0, n_pages, lambda p, c: (page_copy(b, p, slot).wait(), c)[1], 0)

    qi = qi_ref[0]
    w = wi_ref[0] * wscale
    for g in range(n_pages // group):
        kc = kbuf[slot, g * group:(g + 1) * group].reshape(group * PAGE_SIZE, IDX_DIM).astype(BF16)
        s = lax.dot_general(qi, kc, NT_DIMS, preferred_element_type=F32)
        r = jnp.sum(jnp.maximum(s, 0.0) * w, axis=0, keepdims=True)
        for j in range(group):
            sc_ref[g * group + j:g * group + j + 1, :] = r[:, j * PAGE_SIZE:(j + 1) * PAGE_SIZE]
    s_new = jnp.sum(qi.astype(F32) * knew_ref[0].astype(BF16).astype(F32), axis=1, keepdims=True)
    s_new = jnp.sum(jnp.maximum(s_new, 0.0) * w, axis=0, keepdims=True)
    pad_rows = sc_ref.shape[0] - n_pages
    first = (lax.broadcasted_iota(jnp.int32, (pad_rows, LANES), 0) == 0) & (
        lax.broadcasted_iota(jnp.int32, (pad_rows, LANES), 1) == 0)
    sc_ref[n_pages:, :] = jnp.where(first, s_new, -jnp.inf)

    s = sc_ref[...]

    def count(hit):
        return jnp.sum(jnp.sum(jnp.where(hit, 1.0, 0.0), axis=1, keepdims=True), axis=0, keepdims=True)

    thr = _key_to_float(_kth_largest_key(lambda cand: count(s >= _key_to_float(cand)), topk,
                                         jnp.zeros((1, 1), jnp.int32)))
    above = s > thr
    tied = jnp.where(s == thr, 1.0, 0.0)
    keep_tied = (tied > 0.5) & (_index_order_rank(tied) < topk - count(above))
    chosen = jnp.where(above | keep_tied, 1.0, 0.0)
    pos_ref[...] = jnp.where(chosen > 0.5, _index_order_rank(chosen), -1.0)

    slot_id = lax.broadcasted_iota(jnp.int32, (topk, LANES), 0).astype(F32)
    lane = lax.broadcasted_iota(jnp.int32, (1, LANES), 1)

    def place(p, acc):
        key_pos = (p * PAGE_SIZE + lane).astype(F32)
        return acc + jnp.where(pos_ref[pl.ds(p, 1), :] == slot_id, key_pos, 0.0)

    acc = lax.fori_loop(0, n_pages + 1, place, jnp.zeros((topk, LANES), F32))
    idx_ref[0] = jnp.sum(acc, axis=1, keepdims=True).astype(jnp.int32)


def _sample_indices(qi, wi, kidx_new, cache_kidx, pages):
    db, n_pages = pages.shape
    past = n_pages * PAGE_SIZE
    topk = min(TOPK_MAX, (past + 1) // 4)
    assert topk % SUBLANES == 0
    group = 16 if n_pages % 16 == 0 else 1
    rows = (n_pages + 1 + SUBLANES - 1) // SUBLANES * SUBLANES
    grid_spec = pltpu.PrefetchScalarGridSpec(
        num_scalar_prefetch=1,
        grid=(db,),
        in_specs=[
            pl.BlockSpec((1, IDX_HEADS, IDX_DIM), lambda b, pg: (b, 0, 0)),
            pl.BlockSpec((1, IDX_HEADS, 1), lambda b, pg: (b, 0, 0)),
            pl.BlockSpec((1, 1, IDX_DIM), lambda b, pg: (b, 0, 0)),
            pl.BlockSpec(memory_space=pl.ANY),
        ],
        out_specs=pl.BlockSpec((1, topk, 1), lambda b, pg: (b, 0, 0)),
        scratch_shapes=[
            pltpu.VMEM((2, n_pages, PAGE_SIZE, IDX_DIM), F32),
            pltpu.SemaphoreType.DMA((2,)),
            pltpu.VMEM((rows, LANES), F32),
            pltpu.VMEM((rows, LANES), F32),
        ],
    )
    idx = pl.pallas_call(
        functools.partial(_sindex_kernel, n_pages=n_pages, topk=topk,
                          wscale=IDX_HEADS ** -0.5 * IDX_DIM ** -0.5, group=group),
        grid_spec=grid_spec,
        out_shape=jax.ShapeDtypeStruct((db, topk, 1), jnp.int32),
        compiler_params=_params("arbitrary"),
        name="sample_indices",
    )(pages, qi.reshape(db, IDX_HEADS, IDX_DIM), wi.reshape(db, IDX_HEADS, 1),
      kidx_new.reshape(db, 1, IDX_DIM), cache_kidx.reshape(-1, PAGE_SIZE, IDX_DIM))
    return idx.reshape(db, topk)


def _sgather_kernel(idx_ref, pg_ref, q_ref, knew_ref, vnew_ref, ck_ref, cv_ref, o_ref, kbuf, vbuf, sem,
                    *, past, topk):
    b = pl.program_id(0)
    nb = pl.num_programs(0)
    slot = lax.rem(b, 2)

    def start_all(bb, sl):
        def body(j, c):
            i = idx_ref[bb, j]
            ip = jnp.minimum(i, past - 1)
            page = pg_ref[bb, ip // PAGE_SIZE]
            off = lax.rem(ip, PAGE_SIZE)

            @pl.when(i >= past)
            def _():
                pltpu.make_async_copy(knew_ref.at[bb], kbuf.at[sl, j], sem.at[0, sl]).start()
                pltpu.make_async_copy(vnew_ref.at[bb], vbuf.at[sl, j], sem.at[1, sl]).start()

            @pl.when(i < past)
            def _():
                pltpu.make_async_copy(ck_ref.at[page, off], kbuf.at[sl, j], sem.at[0, sl]).start()
                pltpu.make_async_copy(cv_ref.at[page, off], vbuf.at[sl, j], sem.at[1, sl]).start()

            return c

        lax.fori_loop(0, topk, body, 0)

    @pl.when(b == 0)
    def _():
        start_all(b, slot)

    @pl.when(b + 1 < nb)
    def _():
        start_all(b + 1, 1 - slot)

    def wait_body(j, c):
        pltpu.make_async_copy(knew_ref.at[b], kbuf.at[slot, j], sem.at[0, slot]).wait()
        pltpu.make_async_copy(vnew_ref.at[b], vbuf.at[slot, j], sem.at[1, slot]).wait()
        return c

    lax.fori_loop(0, topk, wait_body, 0)

    k = kbuf[slot]
    v = vbuf[slot]
    s = jnp.sum(k * q_ref[...], axis=2, keepdims=True) * HEAD_DIM ** -0.5
    e = jnp.exp(s - jnp.max(s, axis=0, keepdims=True))
    p = e / jnp.sum(e, axis=0, keepdims=True)
    o_ref[...] = jnp.sum(p * v, axis=0, keepdims=True).astype(o_ref.dtype)


def _sample_attend(q, k_new, v_new, idx, cache_k, cache_v, pages):
    db, n_pages = pages.shape
    topk = idx.shape[1]
    tile = (N_HEADS, HEAD_DIM)
    grid_spec = pltpu.PrefetchScalarGridSpec(
        num_scalar_prefetch=2,
        grid=(db,),
        in_specs=[
            pl.BlockSpec((1,) + tile, lambda b, ix, pg: (b, 0, 0)),
            pl.BlockSpec(memory_space=pl.ANY),
            pl.BlockSpec(memory_space=pl.ANY),
            pl.BlockSpec(memory_space=pl.ANY),
            pl.BlockSpec(memory_space=pl.ANY),
        ],
        out_specs=pl.BlockSpec((1,) + tile, lambda b, ix, pg: (b, 0, 0)),
        scratch_shapes=[
            pltpu.VMEM((2, topk) + tile, F32),
            pltpu.VMEM((2, topk) + tile, F32),
            pltpu.SemaphoreType.DMA((2, 2)),
        ],
    )
    out = pl.pallas_call(
        functools.partial(_sgather_kernel, past=n_pages * PAGE_SIZE, topk=topk),
        grid_spec=grid_spec,
        out_shape=jax.ShapeDtypeStruct((db,) + tile, BF16),
        compiler_params=_params("arbitrary"),
        name="sample_attend",
    )(idx, pages, q.reshape((db,) + tile), k_new.reshape((db,) + tile), v_new.reshape((db,) + tile),
      cache_k.reshape((-1, PAGE_SIZE) + tile), cache_v.reshape((-1, PAGE_SIZE) + tile))
    return out.reshape(db, N_HEADS * HEAD_DIM)


def _merge_kernel(x_ref, g_ref, ya_ref, yb_ref, ym_ref, woa_ref, wob_ref, wom_ref, wout_ref,
                  gpm_ref, gpf_ref, x1_ref, h2_ref):
    d = x_ref.shape[1]
    merged = (g_ref[:, 0:d].astype(F32) * jnp.dot(ya_ref[...], woa_ref[...], preferred_element_type=F32)
              + g_ref[:, d:2 * d].astype(F32) * jnp.dot(yb_ref[...], wob_ref[...], preferred_element_type=F32)
              + g_ref[:, 2 * d:3 * d].astype(F32) * jnp.dot(ym_ref[...], wom_ref[...], preferred_element_type=F32))
    o = jnp.dot(merged.astype(BF16), wout_ref[...], preferred_element_type=F32)
    x1 = x_ref[...] + _rmsnorm_rows(o, gpm_ref[...])
    x1_ref[...] = x1
    h2_ref[...] = _rmsnorm_rows(x1, gpf_ref[...]).astype(h2_ref.dtype)


def _merge(x, gates, y_a, y_b, y_m, w_oa, w_ob, w_om, w_out, g_post_mix, g_pre_ffn):
    m, d = x.shape
    tm = _tile(m, 256)
    once = pl.Buffered(1)

    def rows(width):
        return pl.BlockSpec((tm, width), lambda i: (i, 0))

    def whole(a):
        return pl.BlockSpec(a.shape, lambda i: (0, 0), pipeline_mode=once)

    return pl.pallas_call(
        _merge_kernel,
        grid=(m // tm,),
        in_specs=[rows(d), rows(N_BRANCH * d), rows(y_a.shape[1]), rows(y_b.shape[1]), rows(y_m.shape[1]),
                  whole(w_oa), whole(w_ob), whole(w_om), whole(w_out),
                  pl.BlockSpec((1, d), lambda i: (0, 0)), pl.BlockSpec((1, d), lambda i: (0, 0))],
        out_specs=[rows(d), rows(d)],
        out_shape=[jax.ShapeDtypeStruct((m, d), F32), jax.ShapeDtypeStruct((m, d), BF16)],
        compiler_params=_params("parallel"),
        name="merge",
    )(x, gates, y_a, y_b, y_m, w_oa, w_ob, w_om, w_out, g_post_mix.reshape(1, d), g_pre_ffn.reshape(1, d))


def _ffn_kernel(h_ref, wg_ref, wu_ref, wd_ref, x1_ref, gpf_ref, o_ref):
    j = pl.program_id(1)

    @pl.when(j == 0)
    def _():
        o_ref[...] = jnp.zeros(o_ref.shape, F32)

    h = h_ref[...]
    a = jnp.dot(h, wg_ref[...], preferred_element_type=F32)
    c = jnp.dot(h, wu_ref[...], preferred_element_type=F32)
    act = (jax.nn.silu(a) * c).astype(BF16)
    o_ref[...] += jnp.dot(act, wd_ref[...], preferred_element_type=F32)

    @pl.when(j == pl.num_programs(1) - 1)
    def _():
        o_ref[...] = x1_ref[...] + _rmsnorm_rows(o_ref[...], gpf_ref[...])


def _ffn(h2, x1, w_gu, w_down, g_post_ffn):
    m, d = h2.shape
    d_ff = w_down.shape[0]
    tm = _tile(m, 512)
    tf = _tile(d_ff, 512)
    nf = d_ff // tf
    return pl.pallas_call(
        _ffn_kernel,
        grid=(m // tm, nf),
        in_specs=[
            pl.BlockSpec((tm, d), lambda i, j: (i, 0)),
            pl.BlockSpec((d, tf), lambda i, j: (0, j)),
            pl.BlockSpec((d, tf), lambda i, j: (0, j + nf)),
            pl.BlockSpec((tf, d), lambda i, j: (j, 0)),
            pl.BlockSpec((tm, d), lambda i, j: (i, 0)),
            pl.BlockSpec((1, d), lambda i, j: (0, 0)),
        ],
        out_specs=pl.BlockSpec((tm, d), lambda i, j: (i, 0)),
        out_shape=jax.ShapeDtypeStruct((m, d), F32),
        compiler_params=_params("parallel", "arbitrary"),
        name="ffn",
    )(h2, w_gu, w_gu, w_down, x1, g_post_ffn.reshape(1, d))


def _split_w_in(w_in, d):
    gw = d // 2
    aw = N_HEADS * HEAD_DIM
    iw = IDX_HEADS * IDX_DIM
    mw = MEM_HEADS * MEM_HEAD_DIM
    sizes = (2 * gw, aw, aw, aw, iw, IDX_DIM + IDX_HEADS, mw, N_BRANCH * d)
    w_t = w_in.T
    parts, start = [], 0
    for s in sizes:
        parts.append(w_t[start:start + s].astype(BF16))
        start += s
    w_uv, w_q, w_k, w_v, w_qi, w_kw, w_qm, w_g = parts
    w_kw = jnp.pad(w_kw, ((0, 2 * LANES - w_kw.shape[0]), (0, 0)))
    return w_uv, w_q, w_k, w_v, w_qi, w_kw, w_qm, w_g


def _project_common(x2d, g_pre_mix, ws, ln_g, ln_b, uv_dtype, q_dtype, q_scale):
    w_uv, w_q, w_k, w_v, w_qi, w_kw, w_qm, w_g = ws
    h = _rmsnorm(x2d, g_pre_mix, BF16)
    uv = _project_uv(h, w_uv, ln_g, ln_b, uv_dtype)
    (q,) = _matmul(h, w_q, (q_dtype,), scale=q_scale, name="proj_q")
    k, k_bf = _matmul(h, w_k, (F32, BF16), name="proj_k")
    v, v_bf = _matmul(h, w_v, (F32, BF16), name="proj_v")
    (qi,) = _matmul(h, w_qi, (BF16,), name="proj_qi")
    (kw,) = _matmul(h, w_kw, (F32,), name="proj_kidx")
    (qm,) = _matmul(h, w_qm, (BF16,), name="proj_qm")
    (gates,) = _matmul(h, w_g, (BF16,), act="sigmoid", name="proj_gates")
    kidx = kw[:, :IDX_DIM]
    wi = kw[:, IDX_DIM:IDX_DIM + IDX_HEADS]
    return uv, q, k, k_bf, v, v_bf, qi, kidx, wi, qm, gates


def kernel(x_prompt, x_sample, mem_prompt, cache_k, cache_v, cache_kidx, cache_mem_k, cache_mem_v, page_table, g_pre_mix, g_post_mix, g_pre_ffn, g_post_ffn, g_mem, ln_v_g, ln_v_b, w_in, w_s, b_s, w_oa, w_ob, w_om, w_out, w_mem_kv, w_gu, w_down):
    depth = w_in.shape[0]
    b, t, d = x_prompt.shape
    db, dt, _ = x_sample.shape
    assert dt == 1, "sample path handles one new token per sequence"
    n_mem = mem_prompt.shape[1]
    aw = N_HEADS * HEAD_DIM
    mw = MEM_HEADS * MEM_HEAD_DIM
    gw = d // 2
    past = page_table.shape[1] * PAGE_SIZE

    yp = x_prompt.reshape(b * t, d)
    ys = x_sample.reshape(db * dt, d)
    outs = [[] for _ in range(9)]
    for l in range(depth):
        ws = _split_w_in(w_in[l], d)
        woa, wob, wom, wout = (w.astype(BF16) for w in (w_oa[l], w_ob[l], w_om[l], w_out[l]))
        wgu, wdown, wmem = w_gu[l].astype(BF16), w_down[l].astype(BF16), w_mem_kv[l].T.astype(BF16)

        uv, q, k, k_bf, v, v_bf, qi, kidx, wi, qm, gates = _project_common(
            yp, g_pre_mix[l], ws, ln_v_g[l], ln_v_b[l], BF16, BF16, DSA_LOG2_SCALE)
        y_a = _gmlp_spatial(uv, w_s[l], b_s[l])
        y_b = _dsa_prompt(
            q.reshape(b, t, aw), qi.reshape(b, t, -1), jnp.swapaxes(wi.reshape(b, t, IDX_HEADS), 1, 2),
            k_bf.reshape(b, t, aw), kidx.astype(BF16).reshape(b, t, IDX_DIM),
            v_bf.reshape(b, t, aw)).reshape(b * t, aw)
        hm = _rmsnorm(mem_prompt.reshape(b * n_mem, d), g_mem[l], BF16)
        (mkv,) = _matmul(hm, wmem, (F32,), name="proj_mem_kv")
        mk = mkv[:, :mw].reshape(b, n_mem, mw)
        mv = mkv[:, mw:].reshape(b, n_mem, mw)
        y_m = _mem_attend(qm.reshape(b, t, mw), mk, mv).reshape(b * t, mw)
        x1, h2 = _merge(yp, gates, y_a, y_b, y_m, woa, wob, wom, wout, g_post_mix[l], g_pre_ffn[l])
        yp = _ffn(h2, x1, wgu, wdown, g_post_ffn[l])
        outs[0].append(k.reshape(b, t, N_HEADS, HEAD_DIM))
        outs[1].append(v.reshape(b, t, N_HEADS, HEAD_DIM))
        outs[2].append(kidx.reshape(b, t, IDX_DIM))
        outs[3].append(mk.reshape(b, n_mem, MEM_HEADS, MEM_HEAD_DIM))
        outs[4].append(mv.reshape(b, n_mem, MEM_HEADS, MEM_HEAD_DIM))

        uv, q, k, k_bf, v, v_bf, qi, kidx, wi, qm, gates = _project_common(
            ys, g_pre_mix[l], ws, ln_v_g[l], ln_v_b[l], F32, F32, None)
        y_a = _gmlp_first_row(uv, w_s[l], b_s[l])
        pages = page_table + l * cache_k.shape[1]
        idx = _sample_indices(qi, wi, kidx, cache_kidx, pages)
        y_b = _sample_attend(q, k, v, idx, cache_k, cache_v, pages)
        qm_pad = jnp.pad(qm.reshape(db, 1, mw), ((0, 0), (0, 2 * SUBLANES - 1), (0, 0)))
        y_m = _mem_attend(qm_pad, cache_mem_k[l].reshape(db, n_mem, mw),
                          cache_mem_v[l].reshape(db, n_mem, mw))[:, 0, :]
        x1, h2 = _merge(ys, gates, y_a, y_b, y_m, woa, wob, wom, wout, g_post_mix[l], g_pre_ffn[l])
        ys = _ffn(h2, x1, wgu, wdown, g_post_ffn[l])
        outs[5].append(k.reshape(db, dt, N_HEADS, HEAD_DIM))
        outs[6].append(v.reshape(db, dt, N_HEADS, HEAD_DIM))
        outs[7].append(kidx.reshape(db, dt, IDX_DIM))
        outs[8].append(uv[:, gw:].reshape(db, dt, gw))

    return (yp.reshape(b, t, d), ys.reshape(db, dt, d)) + tuple(jnp.stack(o) for o in outs)
```

```python
import functools

import jax
import jax.numpy as jnp
from jax import lax
from jax.experimental import pallas as pl
from jax.experimental.pallas import tpu as pltpu

EPS = 1e-6
CHUNK = 128
GM_GROUPS = 8
N_HEADS = 8
HEAD_DIM = 128
IDX_HEADS = 16
IDX_DIM = 128
TOPK_MAX = 256
MEM_HEADS = 4
MEM_HEAD_DIM = 128
N_BRANCH = 3
PAGE_SIZE = 128

LANES = 128
SUBLANES = 8
VMEM_LIMIT_BYTES = 56 * 2**20
INT_MIN = -2**31
KEY_LOWEST = INT_MIN + 0x800000
KEY_INF = 0x7F800000
NEG = -1e30
LOG2_E = 1.4426950408889634
DSA_LOG2_SCALE = HEAD_DIM ** -0.5 * LOG2_E
ONES_ROWS = 16
COUNT_ROWS = 64
MM_TM, MM_TN = 2048, 1024
FFN_UP_TM = 4096

F32 = jnp.float32
BF16 = jnp.bfloat16
NT_DIMS = (((1,), (1,)), ((), ()))


def _params(*sem):
    return pltpu.CompilerParams(dimension_semantics=sem, vmem_limit_bytes=VMEM_LIMIT_BYTES)


def _tile(n, pref):
    if n <= pref:
        return n
    t = pref
    while n % t:
        t //= 2
    return t


def _key_to_float(key):
    key = jnp.clip(key, KEY_LOWEST, KEY_INF)
    return pltpu.bitcast(key ^ ((key >> 31) & 0x7FFFFFFF), F32)


def _kth_largest_key(count_ge, topk, like):
    base = jnp.where(count_ge(jnp.zeros_like(like)) >= topk, 0, INT_MIN).astype(jnp.int32)

    def bit_body(b, t):
        cand = t | jnp.left_shift(jnp.int32(1), 30 - b)
        return jnp.where(count_ge(cand) >= topk, cand, t)

    return lax.fori_loop(0, 31, bit_body, base)


def _rmsnorm_rows(x, g):
    return x * lax.rsqrt(jnp.mean(x * x, axis=-1, keepdims=True) + EPS) * g


def _rmsnorm_kernel(x_ref, g_ref, o_ref):
    o_ref[...] = _rmsnorm_rows(x_ref[...], g_ref[...]).astype(o_ref.dtype)


def _rmsnorm(x, g, out_dtype):
    m, d = x.shape
    tm = _tile(m, 512)
    return pl.pallas_call(
        _rmsnorm_kernel,
        grid=(m // tm,),
        in_specs=[pl.BlockSpec((tm, d), lambda i: (i, 0)), pl.BlockSpec((1, d), lambda i: (0, 0))],
        out_specs=pl.BlockSpec((tm, d), lambda i: (i, 0)),
        out_shape=jax.ShapeDtypeStruct((m, d), out_dtype),
        compiler_params=_params("parallel"),
        name="rmsnorm",
    )(x, g.reshape(1, d))


def _mm_kernel(x_ref, w_ref, *o_refs, act, scale):
    acc = lax.dot_general(x_ref[...], w_ref[...], NT_DIMS, preferred_element_type=F32)
    if act == "sigmoid":
        acc = jax.nn.sigmoid(acc)
    if scale is not None:
        acc = acc * scale
    for o_ref in o_refs:
        o_ref[...] = acc.astype(o_ref.dtype)


def _matmul(x, w_t, out_dtypes, act=None, scale=None, name="matmul"):
    m, k = x.shape
    n = w_t.shape[0]
    tm = _tile(m, MM_TM)
    tn = _tile(n, MM_TN)
    outs = pl.pallas_call(
        functools.partial(_mm_kernel, act=act, scale=scale),
        grid=(m // tm, n // tn),
        in_specs=[pl.BlockSpec((tm, k), lambda i, j: (i, 0)), pl.BlockSpec((tn, k), lambda i, j: (j, 0))],
        out_specs=[pl.BlockSpec((tm, tn), lambda i, j: (i, j)) for _ in out_dtypes],
        out_shape=[jax.ShapeDtypeStruct((m, n), dt) for dt in out_dtypes],
        compiler_params=_params("parallel", "parallel"),
        name=name,
    )(x, w_t)
    return outs


def _uv_kernel(x_ref, w_ref, lng_ref, lnb_ref, o_ref):
    y = jax.nn.gelu(lax.dot_general(x_ref[...], w_ref[...], NT_DIMS, preferred_element_type=F32))

    @pl.when(pl.program_id(1) == 0)
    def _():
        o_ref[...] = y.astype(o_ref.dtype)

    @pl.when(pl.program_id(1) == 1)
    def _():
        yc = y - jnp.mean(y, axis=-1, keepdims=True)
        yn = yc * lax.rsqrt(jnp.mean(yc * yc, axis=-1, keepdims=True) + EPS)
        o_ref[...] = (yn * lng_ref[...] + lnb_ref[...]).astype(o_ref.dtype)


def _project_uv(h, w_uv_t, ln_g, ln_b, out_dtype):
    m, k = h.shape
    gw = w_uv_t.shape[0] // 2
    tm = _tile(m, 1024)
    return pl.pallas_call(
        _uv_kernel,
        grid=(m // tm, 2),
        in_specs=[
            pl.BlockSpec((tm, k), lambda i, j: (i, 0)),
            pl.BlockSpec((gw, k), lambda i, j: (j, 0)),
            pl.BlockSpec((1, gw), lambda i, j: (0, 0)),
            pl.BlockSpec((1, gw), lambda i, j: (0, 0)),
        ],
        out_specs=pl.BlockSpec((tm, gw), lambda i, j: (i, j)),
        out_shape=jax.ShapeDtypeStruct((m, 2 * gw), out_dtype),
        compiler_params=_params("parallel", "arbitrary"),
        name="project_uv",
    )(h, w_uv_t, ln_g.reshape(1, gw), ln_b.reshape(1, gw))


def _gmlp_kernel(u_ref, v_ref, ws_ref, bt_ref, o_ref, *, n_chunks):
    c = CHUNK
    gd = u_ref.shape[1] // GM_GROUPS
    causal = lax.broadcasted_iota(jnp.int32, (c, c), 1) <= lax.broadcasted_iota(jnp.int32, (c, c), 0)
    for g in range(GM_GROUPS):
        w = jnp.where(causal, ws_ref[g], 0.0).astype(v_ref.dtype)
        bcol = bt_ref[:, g:g + 1]
        for ci in range(n_chunks):
            rows = slice(ci * c, (ci + 1) * c)
            cols = slice(g * gd, (g + 1) * gd)
            mixed = jnp.dot(w, v_ref[rows, cols], preferred_element_type=F32) + bcol
            o_ref[rows, cols] = (u_ref[rows, cols].astype(F32) * mixed).astype(o_ref.dtype)


def _gmlp_spatial(uv, w_s, b_s):
    m = uv.shape[0]
    gw = uv.shape[1] // 2
    tm = _tile(m, 4 * CHUNK)
    return pl.pallas_call(
        functools.partial(_gmlp_kernel, n_chunks=tm // CHUNK),
        grid=(m // tm,),
        in_specs=[
            pl.BlockSpec((tm, gw), lambda i: (i, 0)),
            pl.BlockSpec((tm, gw), lambda i: (i, 1)),
            pl.BlockSpec((GM_GROUPS, CHUNK, CHUNK), lambda i: (0, 0, 0)),
            pl.BlockSpec((CHUNK, GM_GROUPS), lambda i: (0, 0)),
        ],
        out_specs=pl.BlockSpec((tm, gw), lambda i: (i, 0)),
        out_shape=jax.ShapeDtypeStruct((m, gw), BF16),
        compiler_params=_params("parallel"),
        name="gmlp_spatial",
    )(uv, uv, w_s, b_s.T)


def _gmlp_first_kernel(u_ref, v_ref, w_ref, b_ref, o_ref):
    o_ref[...] = (u_ref[...] * (w_ref[...] * v_ref[...] + b_ref[...])).astype(o_ref.dtype)


def _gmlp_first_row(uv, w_s, b_s):
    m = uv.shape[0]
    gw = uv.shape[1] // 2
    gd = gw // GM_GROUPS
    w_row = jnp.repeat(w_s[:, 0, 0], gd).reshape(1, gw)
    b_row = jnp.repeat(b_s[:, 0], gd).reshape(1, gw)
    return pl.pallas_call(
        _gmlp_first_kernel,
        grid=(1,),
        in_specs=[
            pl.BlockSpec((m, gw), lambda i: (0, 0)),
            pl.BlockSpec((m, gw), lambda i: (0, 1)),
            pl.BlockSpec((1, gw), lambda i: (0, 0)),
            pl.BlockSpec((1, gw), lambda i: (0, 0)),
        ],
        out_specs=pl.BlockSpec((m, gw), lambda i: (0, 0)),
        out_shape=jax.ShapeDtypeStruct((m, gw), BF16),
        compiler_params=_params("arbitrary"),
        name="gmlp_first_row",
    )(uv, uv, w_row, b_row)


def _dsa_kernel(q_ref, qi_ref, wit_ref, qin_ref, witn_ref, k_ref, kidx_ref, vt_ref, o_ref,
                sc_ref, m_ref, acc_ref, s_ref, p_ref, *, tq, tk, topk, wscale):
    i = pl.program_id(1)
    last = pl.num_programs(1) - 1
    cur = lax.rem(i, 2)
    n_chunks = ((i + 1) * tq + tk - 1) // tk
    n_next = ((i + 2) * tq + tk - 1) // tk
    vrows = vt_ref.shape[1] // N_HEADS

    def idx_chunk(c, qi_r, wit_r, blk, slot):
        k0 = pl.multiple_of(c * tk, tk)
        kc = kidx_ref[0, pl.ds(k0, tk), :]
        acc = jnp.zeros((tk, tq), F32)
        for h in range(IDX_HEADS):
            s = lax.dot_general(kc, qi_r[0, :, h * IDX_DIM:(h + 1) * IDX_DIM], NT_DIMS,
                                preferred_element_type=F32)
            acc = acc + jnp.maximum(s, 0.0) * (wit_r[0, h:h + 1, :] * wscale)
        kpos = k0 + lax.broadcasted_iota(jnp.int32, (tk, 1), 0)
        qpos = blk * tq + lax.broadcasted_iota(jnp.int32, (1, tq), 1)
        sc_ref[slot, pl.ds(k0, tk), :] = jnp.where(kpos <= qpos, acc, -jnp.inf)

    def pad_odd(n, slot):
        @pl.when(n % 2 == 1)
        def _():
            sc_ref[slot, pl.ds(pl.multiple_of(n * tk, tk), tk), :] = jnp.full((tk, tq), -jnp.inf, F32)

    @pl.when(i == 0)
    def _():
        def body(c, carry):
            idx_chunk(c, qi_ref, wit_ref, i, cur)
            return carry

        lax.fori_loop(0, n_chunks, body, 0)
        pad_odd(n_chunks, cur)

    def count_ge(cand):
        cand_f = _key_to_float(cand)

        def body(c, cnt):
            k0 = pl.multiple_of(c * (2 * tk), 2 * tk)
            hit = jnp.where(sc_ref[cur, pl.ds(k0, 2 * tk), :] >= cand_f, 1.0, 0.0)
            return cnt + jnp.sum(hit.reshape(2 * tk // COUNT_ROWS, COUNT_ROWS, tq), axis=0)

        cnt = lax.fori_loop(0, (n_chunks + 1) // 2, body, jnp.zeros((COUNT_ROWS, tq), F32))
        return jnp.sum(cnt, axis=0, keepdims=True)

    thr = _key_to_float(_kth_largest_key(count_ge, topk, jnp.zeros((1, tq), jnp.int32)))

    m_ref[...] = jnp.full(m_ref.shape, NEG, F32)
    acc_ref[...] = jnp.zeros(acc_ref.shape, F32)

    def att_chunk(c):
        k0 = pl.multiple_of(c * tk, tk)
        bias = jnp.where(sc_ref[cur, pl.ds(k0, tk), :] >= thr, 0.0, NEG)
        m_old, m_new = [], []
        for h in range(N_HEADS):
            hs = slice(h * HEAD_DIM, (h + 1) * HEAD_DIM)
            s = lax.dot_general(k_ref[0, pl.ds(k0, tk), hs], q_ref[0, :, hs], NT_DIMS,
                                preferred_element_type=F32) + bias
            s_ref[h] = s
            m_old.append(m_ref[h])
            m_new.append(jnp.maximum(m_old[h], jnp.max(s, axis=0, keepdims=True)))
            m_ref[h] = m_new[h]
        for h in range(N_HEADS):
            p_ref[h] = jnp.exp2(s_ref[h] - m_new[h][0:1, :]).astype(BF16)
        for h in range(N_HEADS):
            pv = jnp.dot(vt_ref[0, h * vrows:(h + 1) * vrows, pl.ds(k0, tk)], p_ref[h],
                         preferred_element_type=F32)
            acc_ref[h] = jnp.exp2(m_old[h] - m_new[h])[0:1, :] * acc_ref[h] + pv

    @pl.when(i < last)
    def _():
        def fused(c, carry):
            att_chunk(c)
            idx_chunk(c, qin_ref, witn_ref, i + 1, 1 - cur)
            return carry

        def tail(c, carry):
            idx_chunk(c, qin_ref, witn_ref, i + 1, 1 - cur)
            return carry

        lax.fori_loop(0, n_chunks, fused, 0)
        lax.fori_loop(n_chunks, n_next, tail, 0)
        pad_odd(n_next, 1 - cur)

    @pl.when(i == last)
    def _():
        def body(c, carry):
            att_chunk(c)
            return carry

        lax.fori_loop(0, n_chunks, body, 0)

    for h in range(N_HEADS):
        o = acc_ref[h, 0:HEAD_DIM, :] / acc_ref[h, HEAD_DIM:HEAD_DIM + 1, :]
        o_ref[0, :, h * HEAD_DIM:(h + 1) * HEAD_DIM] = o.T.astype(o_ref.dtype)


def _dsa_prompt(q, qi, wi_t, k, kidx, v, tq=256, tk=256):
    b, t, aw = q.shape
    ones = jnp.ones((b, N_HEADS, ONES_ROWS, t), v.dtype)
    v_t = jnp.concatenate([jnp.transpose(v.reshape(b, t, N_HEADS, HEAD_DIM), (0, 2, 3, 1)), ones], axis=2)
    v_t = v_t.reshape(b, N_HEADS * (HEAD_DIM + ONES_ROWS), t)
    tq = _tile(t, tq)
    tk = _tile(tq, tk)
    topk = min(TOPK_MAX, t // 4)
    once = pl.Buffered(1)
    nq = t // tq

    def next_block(i):
        return jnp.minimum(i + 1, nq - 1)

    return pl.pallas_call(
        functools.partial(_dsa_kernel, tq=tq, tk=tk, topk=topk,
                          wscale=IDX_HEADS ** -0.5 * IDX_DIM ** -0.5),
        grid=(b, nq),
        in_specs=[
            pl.BlockSpec((1, tq, aw), lambda bi, i: (bi, i, 0)),
            pl.BlockSpec((1, tq, IDX_HEADS * IDX_DIM), lambda bi, i: (bi, i, 0)),
            pl.BlockSpec((1, IDX_HEADS, tq), lambda bi, i: (bi, 0, i)),
            pl.BlockSpec((1, tq, IDX_HEADS * IDX_DIM), lambda bi, i: (bi, next_block(i), 0)),
            pl.BlockSpec((1, IDX_HEADS, tq), lambda bi, i: (bi, 0, next_block(i))),
            pl.BlockSpec((1, t, aw), lambda bi, i: (bi, 0, 0), pipeline_mode=once),
            pl.BlockSpec((1, t, IDX_DIM), lambda bi, i: (bi, 0, 0), pipeline_mode=once),
            pl.BlockSpec((1, v_t.shape[1], t), lambda bi, i: (bi, 0, 0), pipeline_mode=once),
        ],
        out_specs=pl.BlockSpec((1, tq, aw), lambda bi, i: (bi, i, 0)),
        out_shape=jax.ShapeDtypeStruct((b, t, aw), BF16),
        scratch_shapes=[
            pltpu.VMEM((2, t + tk, tq), F32),
            pltpu.VMEM((N_HEADS, SUBLANES, tq), F32),
            pltpu.VMEM((N_HEADS, HEAD_DIM + ONES_ROWS, tq), F32),
            pltpu.VMEM((N_HEADS, tk, tq), F32),
            pltpu.VMEM((N_HEADS, tk, tq), BF16),
        ],
        compiler_params=_params("parallel", "arbitrary"),
        name="dsa_prompt",
    )(q, qi, wi_t, qi, wi_t, k, kidx, v_t)


def _memattn_kernel(qm_ref, mk_ref, mv_ref, o_ref):
    scale = MEM_HEAD_DIM ** -0.5
    for h in range(MEM_HEADS):
        hs = slice(h * MEM_HEAD_DIM, (h + 1) * MEM_HEAD_DIM)
        s = lax.dot_general(qm_ref[0, :, hs], mk_ref[0, :, hs].astype(BF16), NT_DIMS,
                            preferred_element_type=F32) * scale
        e = jnp.exp(s - jnp.max(s, axis=-1, keepdims=True))
        p = e / jnp.sum(e, axis=-1, keepdims=True)
        o = jnp.dot(p.astype(BF16), mv_ref[0, :, hs].astype(BF16), preferred_element_type=F32)
        o_ref[0, :, hs] = o.astype(o_ref.dtype)


def _mem_attend(qm, mk, mv):
    b, t, mw = qm.shape
    n_mem = mk.shape[1]
    tm = _tile(t, 512)
    return pl.pallas_call(
        _memattn_kernel,
        grid=(b, t // tm),
        in_specs=[
            pl.BlockSpec((1, tm, mw), lambda bi, i: (bi, i, 0)),
            pl.BlockSpec((1, n_mem, mw), lambda bi, i: (bi, 0, 0)),
            pl.BlockSpec((1, n_mem, mw), lambda bi, i: (bi, 0, 0)),
        ],
        out_specs=pl.BlockSpec((1, tm, mw), lambda bi, i: (bi, i, 0)),
        out_shape=jax.ShapeDtypeStruct((b, t, mw), BF16),
        compiler_params=_params("parallel", "parallel"),
        name="mem_attend",
    )(qm, mk, mv)


def _strict_triangle(n, lower):
    r = lax.broadcasted_iota(jnp.int32, (n, n), 0)
    c = lax.broadcasted_iota(jnp.int32, (n, n), 1)
    return jnp.where((c < r) if lower else (r < c), 1.0, 0.0).astype(BF16)


def _index_order_rank(flag):
    rows = flag.shape[0]
    within = jnp.dot(flag.astype(BF16), _strict_triangle(LANES, False), preferred_element_type=F32)
    per_row = jnp.broadcast_to(jnp.sum(flag, axis=1, keepdims=True), flag.shape)
    before = jnp.dot(_strict_triangle(rows, True), per_row.astype(BF16), preferred_element_type=F32)
    return before + within


def _sindex_kernel(pg_ref, qi_ref, wi_ref, knew_ref, cache_ref, idx_ref, kbuf, sem, sc_ref, pos_ref,
                   *, n_pages, topk, wscale, group):
    b = pl.program_id(0)
    nb = pl.num_programs(0)
    slot = lax.rem(b, 2)

    def page_copy(bb, p, sl):
        return pltpu.make_async_copy(cache_ref.at[pg_ref[bb, p]], kbuf.at[sl, p], sem.at[sl])

    def start_all(bb, sl):
        lax.fori_loop(0, n_pages, lambda p, c: (page_copy(bb, p, sl).start(), c)[1], 0)

    @pl.when(b == 0)
    def _():
        start_all(b, slot)

    @pl.when(b + 1 < nb)
    def _():
        start_all(b + 1, 1 - slot)

    lax.fori_loop(0, n_pages, lambda p, c: (page_copy(b, p, slot).wait(), c)[1], 0)

    qi = qi_ref[0]
    w = wi_ref[0] * wscale
    for g in range(n_pages // group):
        kc = kbuf[slot, g * group:(g + 1) * group].reshape(group * PAGE_SIZE, IDX_DIM).astype(BF16)
        s = lax.dot_general(qi, kc, NT_DIMS, preferred_element_type=F32)
        r = jnp.sum(jnp.maximum(s, 0.0) * w, axis=0, keepdims=True)
        for j in range(group):
            sc_ref[g * group + j:g * group + j + 1, :] = r[:, j * PAGE_SIZE:(j + 1) * PAGE_SIZE]
    s_new = jnp.sum(qi.astype(F32) * knew_ref[0].astype(BF16).astype(F32), axis=1, keepdims=True)
    s_new = jnp.sum(jnp.maximum(s_new, 0.0) * w, axis=0, keepdims=True)
    pad_rows = sc_ref.shape[0] - n_pages
    first = (lax.broadcasted_iota(jnp.int32, (pad_rows, LANES), 0) == 0) & (
        lax.broadcasted_iota(jnp.int32, (pad_rows, LANES), 1) == 0)
    sc_ref[n_pages:, :] = jnp.where(first, s_new, -jnp.inf)

    s = sc_ref[...]

    def count(hit):
        return jnp.sum(jnp.sum(jnp.where(hit, 1.0, 0.0), axis=1, keepdims=True), axis=0, keepdims=True)

    thr = _key_to_float(_kth_largest_key(lambda cand: count(s >= _key_to_float(cand)), topk,
                                         jnp.zeros((1, 1), jnp.int32)))
    above = s > thr
    tied = jnp.where(s == thr, 1.0, 0.0)
    keep_tied = (tied > 0.5) & (_index_order_rank(tied) < topk - count(above))
    chosen = jnp.where(above | keep_tied, 1.0, 0.0)
    pos_ref[...] = jnp.where(chosen > 0.5, _index_order_rank(chosen), -1.0)

    slot_id = lax.broadcasted_iota(jnp.int32, (topk, LANES), 0).astype(F32)
    lane = lax.broadcasted_iota(jnp.int32, (1, LANES), 1)

    def place(p, acc):
        key_pos = (p * PAGE_SIZE + lane).astype(F32)
        return acc + jnp.where(pos_ref[pl.ds(p, 1), :] == slot_id, key_pos, 0.0)

    acc = lax.fori_loop(0, n_pages + 1, place, jnp.zeros((topk, LANES), F32))
    idx_ref[0] = jnp.sum(acc, axis=1, keepdims=True).astype(jnp.int32)


def _sample_indices(qi, wi, kidx_new, cache_kidx, pages):
    db, n_pages = pages.shape
    past = n_pages * PAGE_SIZE
    topk = min(TOPK_MAX, (past + 1) // 4)
    assert topk % SUBLANES == 0
    group = 16 if n_pages % 16 == 0 else 1
    rows = (n_pages + 1 + SUBLANES - 1) // SUBLANES * SUBLANES
    grid_spec = pltpu.PrefetchScalarGridSpec(
        num_scalar_prefetch=1,
        grid=(db,),
        in_specs=[
            pl.BlockSpec((1, IDX_HEADS, IDX_DIM), lambda b, pg: (b, 0, 0)),
            pl.BlockSpec((1, IDX_HEADS, 1), lambda b, pg: (b, 0, 0)),
            pl.BlockSpec((1, 1, IDX_DIM), lambda b, pg: (b, 0, 0)),
            pl.BlockSpec(memory_space=pl.ANY),
        ],
        out_specs=pl.BlockSpec((1, topk, 1), lambda b, pg: (b, 0, 0)),
        scratch_shapes=[
            pltpu.VMEM((2, n_pages, PAGE_SIZE, IDX_DIM), F32),
            pltpu.SemaphoreType.DMA((2,)),
            pltpu.VMEM((rows, LANES), F32),
            pltpu.VMEM((rows, LANES), F32),
        ],
    )
    idx = pl.pallas_call(
        functools.partial(_sindex_kernel, n_pages=n_pages, topk=topk,
                          wscale=IDX_HEADS ** -0.5 * IDX_DIM ** -0.5, group=group),
        grid_spec=grid_spec,
        out_shape=jax.ShapeDtypeStruct((db, topk, 1), jnp.int32),
        compiler_params=_params("arbitrary"),
        name="sample_indices",
    )(pages, qi.reshape(db, IDX_HEADS, IDX_DIM), wi.reshape(db, IDX_HEADS, 1),
      kidx_new.reshape(db, 1, IDX_DIM), cache_kidx.reshape(-1, PAGE_SIZE, IDX_DIM))
    return idx.reshape(db, topk)


def _sgather_kernel(idx_ref, pg_ref, q_ref, knew_ref, vnew_ref, ck_ref, cv_ref, o_ref, kbuf, vbuf, sem,
                    *, past, topk):
    b = pl.program_id(0)
    nb = pl.num_programs(0)
    slot = lax.rem(b, 2)

    def start_all(bb, sl):
        def body(j, c):
            i = idx_ref[bb, j]
            ip = jnp.minimum(i, past - 1)
            page = pg_ref[bb, ip // PAGE_SIZE]
            off = lax.rem(ip, PAGE_SIZE)

            @pl.when(i >= past)
            def _():
                pltpu.make_async_copy(knew_ref.at[bb], kbuf.at[sl, j], sem.at[0, sl]).start()
                pltpu.make_async_copy(vnew_ref.at[bb], vbuf.at[sl, j], sem.at[1, sl]).start()

            @pl.when(i < past)
            def _():
                pltpu.make_async_copy(ck_ref.at[page, off], kbuf.at[sl, j], sem.at[0, sl]).start()
                pltpu.make_async_copy(cv_ref.at[page, off], vbuf.at[sl, j], sem.at[1, sl]).start()

            return c

        lax.fori_loop(0, topk, body, 0)

    @pl.when(b == 0)
    def _():
        start_all(b, slot)

    @pl.when(b + 1 < nb)
    def _():
        start_all(b + 1, 1 - slot)

    def wait_body(j, c):
        pltpu.make_async_copy(knew_ref.at[b], kbuf.at[slot, j], sem.at[0, slot]).wait()
        pltpu.make_async_copy(vnew_ref.at[b], vbuf.at[slot, j], sem.at[1, slot]).wait()
        return c

    lax.fori_loop(0, topk, wait_body, 0)

    k = kbuf[slot]
    v = vbuf[slot]
    s = jnp.sum(k * q_ref[...], axis=2, keepdims=True) * HEAD_DIM ** -0.5
    e = jnp.exp(s - jnp.max(s, axis=0, keepdims=True))
    p = e / jnp.sum(e, axis=0, keepdims=True)
    o_ref[...] = jnp.sum(p * v, axis=0, keepdims=True).astype(o_ref.dtype)


def _sample_attend(q, k_new, v_new, idx, cache_k, cache_v, pages):
    db, n_pages = pages.shape
    topk = idx.shape[1]
    tile = (N_HEADS, HEAD_DIM)
    grid_spec = pltpu.PrefetchScalarGridSpec(
        num_scalar_prefetch=2,
        grid=(db,),
        in_specs=[
            pl.BlockSpec((1,) + tile, lambda b, ix, pg: (b, 0, 0)),
            pl.BlockSpec(memory_space=pl.ANY),
            pl.BlockSpec(memory_space=pl.ANY),
            pl.BlockSpec(memory_space=pl.ANY),
            pl.BlockSpec(memory_space=pl.ANY),
        ],
        out_specs=pl.BlockSpec((1,) + tile, lambda b, ix, pg: (b, 0, 0)),
        scratch_shapes=[
            pltpu.VMEM((2, topk) + tile, F32),
            pltpu.VMEM((2, topk) + tile, F32),
            pltpu.SemaphoreType.DMA((2, 2)),
        ],
    )
    out = pl.pallas_call(
        functools.partial(_sgather_kernel, past=n_pages * PAGE_SIZE, topk=topk),
        grid_spec=grid_spec,
        out_shape=jax.ShapeDtypeStruct((db,) + tile, BF16),
        compiler_params=_params("arbitrary"),
        name="sample_attend",
    )(idx, pages, q.reshape((db,) + tile), k_new.reshape((db,) + tile), v_new.reshape((db,) + tile),
      cache_k.reshape((-1, PAGE_SIZE) + tile), cache_v.reshape((-1, PAGE_SIZE) + tile))
    return out.reshape(db, N_HEADS * HEAD_DIM)


def _merge_kernel(x_ref, g_ref, ya_ref, yb_ref, ym_ref, woa_ref, wob_ref, wom_ref, wout_ref,
                  gpm_ref, gpf_ref, x1_ref, h2_ref):
    d = x_ref.shape[1]
    merged = (g_ref[:, 0:d].astype(F32) * jnp.dot(ya_ref[...], woa_ref[...], preferred_element_type=F32)
              + g_ref[:, d:2 * d].astype(F32) * jnp.dot(yb_ref[...], wob_ref[...], preferred_element_type=F32)
              + g_ref[:, 2 * d:3 * d].astype(F32) * jnp.dot(ym_ref[...], wom_ref[...], preferred_element_type=F32))
    o = jnp.dot(merged.astype(BF16), wout_ref[...], preferred_element_type=F32)
    x1 = x_ref[...] + _rmsnorm_rows(o, gpm_ref[...])
    x1_ref[...] = x1
    h2_ref[...] = _rmsnorm_rows(x1, gpf_ref[...]).astype(h2_ref.dtype)


def _merge(x, gates, y_a, y_b, y_m, w_oa, w_ob, w_om, w_out, g_post_mix, g_pre_ffn):
    m, d = x.shape
    tm = _tile(m, 256)
    once = pl.Buffered(1)

    def rows(width):
        return pl.BlockSpec((tm, width), lambda i: (i, 0))

    def whole(a):
        return pl.BlockSpec(a.shape, lambda i: (0, 0), pipeline_mode=once)

    return pl.pallas_call(
        _merge_kernel,
        grid=(m // tm,),
        in_specs=[rows(d), rows(N_BRANCH * d), rows(y_a.shape[1]), rows(y_b.shape[1]), rows(y_m.shape[1]),
                  whole(w_oa), whole(w_ob), whole(w_om), whole(w_out),
                  pl.BlockSpec((1, d), lambda i: (0, 0)), pl.BlockSpec((1, d), lambda i: (0, 0))],
        out_specs=[rows(d), rows(d)],
        out_shape=[jax.ShapeDtypeStruct((m, d), F32), jax.ShapeDtypeStruct((m, d), BF16)],
        compiler_params=_params("parallel"),
        name="merge",
    )(x, gates, y_a, y_b, y_m, w_oa, w_ob, w_om, w_out, g_post_mix.reshape(1, d), g_pre_ffn.reshape(1, d))


def _ffn_up_kernel(h_ref, wg_ref, wu_ref, o_ref, *, rows):
    wg = wg_ref[...].astype(BF16)
    wu = wu_ref[...].astype(BF16)

    def body(r, carry):
        r0 = pl.multiple_of(r * rows, rows)
        h = h_ref[pl.ds(r0, rows), :]
        a = jnp.dot(h, wg, preferred_element_type=F32)
        c = jnp.dot(h, wu, preferred_element_type=F32)
        o_ref[pl.ds(r0, rows), :] = (jax.nn.silu(a) * c).astype(o_ref.dtype)
        return carry

    lax.fori_loop(0, h_ref.shape[0] // rows, body, 0)


def _ffn_up(h2, w_gu):
    m, d = h2.shape
    d_ff = w_gu.shape[1] // 2
    tm = _tile(m, FFN_UP_TM)
    tf = _tile(d_ff, 2 * LANES)
    nf = d_ff // tf
    return pl.pallas_call(
        functools.partial(_ffn_up_kernel, rows=_tile(tm, 2048)),
        grid=(m // tm, nf),
        in_specs=[
            pl.BlockSpec((tm, d), lambda i, j: (i, 0)),
            pl.BlockSpec((d, tf), lambda i, j: (0, j)),
            pl.BlockSpec((d, tf), lambda i, j: (0, j + nf)),
        ],
        out_specs=pl.BlockSpec((tm, tf), lambda i, j: (i, j)),
        out_shape=jax.ShapeDtypeStruct((m, d_ff), BF16),
        compiler_params=_params("parallel", "parallel"),
        name="ffn_up",
    )(h2, w_gu, w_gu)


def _ffn_down_kernel(a_ref, wd_ref, x1_ref, gpf_ref, o_ref):
    y = jnp.dot(a_ref[...], wd_ref[...], preferred_element_type=F32)
    o_ref[...] = x1_ref[...] + _rmsnorm_rows(y, gpf_ref[...])


def _ffn_down(act, x1, w_down, g_post_ffn):
    m, d_ff = act.shape
    d = w_down.shape[1]
    tm = _tile(m, 256)
    return pl.pallas_call(
        _ffn_down_kernel,
        grid=(m // tm,),
        in_specs=[
            pl.BlockSpec((tm, d_ff), lambda i: (i, 0)),
            pl.BlockSpec((d_ff, d), lambda i: (0, 0), pipeline_mode=pl.Buffered(1)),
            pl.BlockSpec((tm, d), lambda i: (i, 0)),
            pl.BlockSpec((1, d), lambda i: (0, 0)),
        ],
        out_specs=pl.BlockSpec((tm, d), lambda i: (i, 0)),
        out_shape=jax.ShapeDtypeStruct((m, d), F32),
        compiler_params=_params("parallel"),
        name="ffn_down",
    )(act, w_down, x1, g_post_ffn.reshape(1, d))


def _split_w_in(w_in, d):
    gw = d // 2
    aw = N_HEADS * HEAD_DIM
    iw = IDX_HEADS * IDX_DIM
    mw = MEM_HEADS * MEM_HEAD_DIM
    sizes = (2 * gw, aw, aw, aw, iw, IDX_DIM + IDX_HEADS, mw, N_BRANCH * d)
    w_t = w_in.T
    parts, start = [], 0
    for s in sizes:
        parts.append(w_t[start:start + s].astype(BF16))
        start += s
    w_uv, w_q, w_k, w_v, w_qi, w_kw, w_qm, w_g = parts
    w_kw = jnp.pad(w_kw, ((0, 2 * LANES - w_kw.shape[0]), (0, 0)))
    return w_uv, w_q, w_k, w_v, w_qi, w_kw, w_qm, w_g


def _project_common(x2d, g_pre_mix, ws, ln_g, ln_b, uv_dtype, q_dtype, q_scale):
    w_uv, w_q, w_k, w_v, w_qi, w_kw, w_qm, w_g = ws
    h = _rmsnorm(x2d, g_pre_mix, BF16)
    uv = _project_uv(h, w_uv, ln_g, ln_b, uv_dtype)
    (q,) = _matmul(h, w_q, (q_dtype,), scale=q_scale, name="proj_q")
    k, k_bf = _matmul(h, w_k, (F32, BF16), name="proj_k")
    v, v_bf = _matmul(h, w_v, (F32, BF16), name="proj_v")
    (qi,) = _matmul(h, w_qi, (BF16,), name="proj_qi")
    (kw,) = _matmul(h, w_kw, (F32,), name="proj_kidx")
    (qm,) = _matmul(h, w_qm, (BF16,), name="proj_qm")
    (gates,) = _matmul(h, w_g, (BF16,), act="sigmoid", name="proj_gates")
    kidx = kw[:, :IDX_DIM]
    wi = kw[:, IDX_DIM:IDX_DIM + IDX_HEADS]
    return uv, q, k, k_bf, v, v_bf, qi, kidx, wi, qm, gates


def kernel(x_prompt, x_sample, mem_prompt, cache_k, cache_v, cache_kidx, cache_mem_k, cache_mem_v, page_table, g_pre_mix, g_post_mix, g_pre_ffn, g_post_ffn, g_mem, ln_v_g, ln_v_b, w_in, w_s, b_s, w_oa, w_ob, w_om, w_out, w_mem_kv, w_gu, w_down):
    depth = w_in.shape[0]
    b, t, d = x_prompt.shape
    db, dt, _ = x_sample.shape
    assert dt == 1, "sample path handles one new token per sequence"
    n_mem = mem_prompt.shape[1]
    aw = N_HEADS * HEAD_DIM
    mw = MEM_HEADS * MEM_HEAD_DIM
    gw = d // 2
    past = page_table.shape[1] * PAGE_SIZE

    yp = x_prompt.reshape(b * t, d)
    ys = x_sample.reshape(db * dt, d)
    outs = [[] for _ in range(9)]
    for l in range(depth):
        ws = _split_w_in(w_in[l], d)
        woa, wob, wom, wout = (w.astype(BF16) for w in (w_oa[l], w_ob[l], w_om[l], w_out[l]))
        wdown, wmem = w_down[l].astype(BF16), w_mem_kv[l].T.astype(BF16)

        uv, q, k, k_bf, v, v_bf, qi, kidx, wi, qm, gates = _project_common(
            yp, g_pre_mix[l], ws, ln_v_g[l], ln_v_b[l], BF16, BF16, DSA_LOG2_SCALE)
        y_a = _gmlp_spatial(uv, w_s[l], b_s[l])
        y_b = _dsa_prompt(
            q.reshape(b, t, aw), qi.reshape(b, t, -1), jnp.swapaxes(wi.reshape(b, t, IDX_HEADS), 1, 2),
            k_bf.reshape(b, t, aw), kidx.astype(BF16).reshape(b, t, IDX_DIM),
            v_bf.reshape(b, t, aw)).reshape(b * t, aw)
        hm = _rmsnorm(mem_prompt.reshape(b * n_mem, d), g_mem[l], BF16)
        (mkv,) = _matmul(hm, wmem, (F32,), name="proj_mem_kv")
        mk = mkv[:, :mw].reshape(b, n_mem, mw)
        mv = mkv[:, mw:].reshape(b, n_mem, mw)
        y_m = _mem_attend(qm.reshape(b, t, mw), mk, mv).reshape(b * t, mw)
        x1, h2 = _merge(yp, gates, y_a, y_b, y_m, woa, wob, wom, wout, g_post_mix[l], g_pre_ffn[l])
        yp = _ffn_down(_ffn_up(h2, w_gu[l]), x1, wdown, g_post_ffn[l])
        outs[0].append(k.reshape(b, t, N_HEADS, HEAD_DIM))
        outs[1].append(v.reshape(b, t, N_HEADS, HEAD_DIM))
        outs[2].append(kidx.reshape(b, t, IDX_DIM))
        outs[3].append(mk.reshape(b, n_mem, MEM_HEADS, MEM_HEAD_DIM))
        outs[4].append(mv.reshape(b, n_mem, MEM_HEADS, MEM_HEAD_DIM))

        uv, q, k, k_bf, v, v_bf, qi, kidx, wi, qm, gates = _project_common(
            ys, g_pre_mix[l], ws, ln_v_g[l], ln_v_b[l], F32, F32, None)
        y_a = _gmlp_first_row(uv, w_s[l], b_s[l])
        pages = page_table + l * cache_k.shape[1]
        idx = _sample_indices(qi, wi, kidx, cache_kidx, pages)
        y_b = _sample_attend(q, k, v, idx, cache_k, cache_v, pages)
        qm_pad = jnp.pad(qm.reshape(db, 1, mw), ((0, 0), (0, 2 * SUBLANES - 1), (0, 0)))
        y_m = _mem_attend(qm_pad, cache_mem_k[l].reshape(db, n_mem, mw),
                          cache_mem_v[l].reshape(db, n_mem, mw))[:, 0, :]
        x1, h2 = _merge(ys, gates, y_a, y_b, y_m, woa, wob, wom, wout, g_post_mix[l], g_pre_ffn[l])
        ys = _ffn_down(_ffn_up(h2, w_gu[l]), x1, wdown, g_post_ffn[l])
        outs[5].append(k.reshape(db, dt, N_HEADS, HEAD_DIM))
        outs[6].append(v.reshape(db, dt, N_HEADS, HEAD_DIM))
        outs[7].append(kidx.reshape(db, dt, IDX_DIM))
        outs[8].append(uv[:, gw:].reshape(db, dt, gw))

    return (yp.reshape(b, t, d), ys.reshape(db, dt, d)) + tuple(jnp.stack(o) for o in outs)
```

```python
import functools

import jax
import jax.numpy as jnp
from jax import lax
from jax.experimental import pallas as pl
from jax.experimental.pallas import tpu as pltpu

EPS = 1e-6
CHUNK = 128
GM_GROUPS = 8
N_HEADS = 8
HEAD_DIM = 128
IDX_HEADS = 16
IDX_DIM = 128
TOPK_MAX = 256
MEM_HEADS = 4
MEM_HEAD_DIM = 128
N_BRANCH = 3
PAGE_SIZE = 128

LANES = 128
SUBLANES = 8
VMEM_LIMIT_BYTES = 56 * 2**20
INT_MIN = -2**31
KEY_LOWEST = INT_MIN + 0x800000
KEY_INF = 0x7F800000
NEG = -1e30
LOG2_E = 1.4426950408889634
DSA_LOG2_SCALE = HEAD_DIM ** -0.5 * LOG2_E
ONES_ROWS = 16
COUNT_ROWS = 64
MM_TM, MM_TN = 2048, 1024
FFN_UP_TM = 4096

F32 = jnp.float32
BF16 = jnp.bfloat16
NT_DIMS = (((1,), (1,)), ((), ()))


def _params(*sem):
    return pltpu.CompilerParams(dimension_semantics=sem, vmem_limit_bytes=VMEM_LIMIT_BYTES)


def _tile(n, pref):
    if n <= pref:
        return n
    t = pref
    while n % t:
        t //= 2
    return t


def _key_to_float(key):
    key = jnp.clip(key, KEY_LOWEST, KEY_INF)
    return pltpu.bitcast(key ^ ((key >> 31) & 0x7FFFFFFF), F32)


def _kth_largest_key(count_ge, topk, like):
    base = jnp.where(count_ge(jnp.zeros_like(like)) >= topk, 0, INT_MIN).astype(jnp.int32)

    def bit_body(b, t):
        cand = t | jnp.left_shift(jnp.int32(1), 30 - b)
        return jnp.where(count_ge(cand) >= topk, cand, t)

    return lax.fori_loop(0, 31, bit_body, base)


def _rmsnorm_rows(x, g):
    return x * lax.rsqrt(jnp.mean(x * x, axis=-1, keepdims=True) + EPS) * g


def _rmsnorm_kernel(x_ref, g_ref, o_ref):
    o_ref[...] = _rmsnorm_rows(x_ref[...], g_ref[...]).astype(o_ref.dtype)


def _rmsnorm(x, g, out_dtype):
    m, d = x.shape
    tm = _tile(m, 512)
    return pl.pallas_call(
        _rmsnorm_kernel,
        grid=(m // tm,),
        in_specs=[pl.BlockSpec((tm, d), lambda i: (i, 0)), pl.BlockSpec((1, d), lambda i: (0, 0))],
        out_specs=pl.BlockSpec((tm, d), lambda i: (i, 0)),
        out_shape=jax.ShapeDtypeStruct((m, d), out_dtype),
        compiler_params=_params("parallel"),
        name="rmsnorm",
    )(x, g.reshape(1, d))


def _mm_kernel(x_ref, w_ref, *o_refs, act, scale, heads_t):
    acc = lax.dot_general(x_ref[...], w_ref[...].astype(BF16), NT_DIMS, preferred_element_type=F32)
    if act == "sigmoid":
        acc = jax.nn.sigmoid(acc)
    if scale is not None:
        acc = acc * scale
    if heads_t:
        o_refs, t_ref = o_refs[:-1], o_refs[-1]
        rows = HEAD_DIM + ONES_ROWS
        for h in range(acc.shape[1] // HEAD_DIM):
            t_ref[0, h * rows:h * rows + HEAD_DIM, :] = acc[:, h * HEAD_DIM:(h + 1) * HEAD_DIM].T.astype(t_ref.dtype)
            t_ref[0, h * rows + HEAD_DIM:(h + 1) * rows, :] = jnp.ones((ONES_ROWS, acc.shape[0]), t_ref.dtype)
    for o_ref in o_refs:
        o_ref[...] = acc.astype(o_ref.dtype)


def _matmul(x, w_t, out_dtypes, act=None, scale=None, rows=None, heads_t=None, name="matmul"):
    m, k = x.shape
    row0, n = rows if rows is not None else (0, w_t.shape[0])
    tm = _tile(m, MM_TM)
    tn = _tile(n, MM_TN if w_t.dtype == BF16 else MM_TN // 2)
    assert row0 % tn == 0
    out_specs = [pl.BlockSpec((tm, tn), lambda i, j: (i, j)) for _ in out_dtypes]
    out_shapes = [jax.ShapeDtypeStruct((m, n), dt) for dt in out_dtypes]
    if heads_t is not None:
        batch, dt = heads_t
        per_batch = m // batch // tm
        t_rows = tn // HEAD_DIM * (HEAD_DIM + ONES_ROWS)
        out_specs.append(pl.BlockSpec((1, t_rows, tm), lambda i, j: (i // per_batch, j, i % per_batch)))
        out_shapes.append(jax.ShapeDtypeStruct((batch, n // tn * t_rows, m // batch), dt))
    outs = pl.pallas_call(
        functools.partial(_mm_kernel, act=act, scale=scale, heads_t=heads_t is not None),
        grid=(m // tm, n // tn),
        in_specs=[pl.BlockSpec((tm, k), lambda i, j: (i, 0)),
                  pl.BlockSpec((tn, k), lambda i, j: (row0 // tn + j, 0))],
        out_specs=out_specs,
        out_shape=out_shapes,
        compiler_params=_params("parallel", "parallel"),
        name=name,
    )(x, w_t)
    return outs


def _uv_kernel(x_ref, w_ref, lng_ref, lnb_ref, o_ref):
    y = jax.nn.gelu(lax.dot_general(x_ref[...], w_ref[...].astype(BF16), NT_DIMS,
                                    preferred_element_type=F32))

    @pl.when(pl.program_id(1) == 0)
    def _():
        o_ref[...] = y.astype(o_ref.dtype)

    @pl.when(pl.program_id(1) == 1)
    def _():
        yc = y - jnp.mean(y, axis=-1, keepdims=True)
        yn = yc * lax.rsqrt(jnp.mean(yc * yc, axis=-1, keepdims=True) + EPS)
        o_ref[...] = (yn * lng_ref[...] + lnb_ref[...]).astype(o_ref.dtype)


def _project_uv(h, w_t, ln_g, ln_b, out_dtype):
    m, k = h.shape
    gw = ln_g.shape[0]
    tm = _tile(m, 1024)
    return pl.pallas_call(
        _uv_kernel,
        grid=(m // tm, 2),
        in_specs=[
            pl.BlockSpec((tm, k), lambda i, j: (i, 0)),
            pl.BlockSpec((gw, k), lambda i, j: (j, 0)),
            pl.BlockSpec((1, gw), lambda i, j: (0, 0)),
            pl.BlockSpec((1, gw), lambda i, j: (0, 0)),
        ],
        out_specs=pl.BlockSpec((tm, gw), lambda i, j: (i, j)),
        out_shape=jax.ShapeDtypeStruct((m, 2 * gw), out_dtype),
        compiler_params=_params("parallel", "arbitrary"),
        name="project_uv",
    )(h, w_t, ln_g.reshape(1, gw), ln_b.reshape(1, gw))


def _gmlp_kernel(u_ref, v_ref, ws_ref, bt_ref, o_ref, *, n_chunks):
    c = CHUNK
    gd = u_ref.shape[1] // GM_GROUPS
    causal = lax.broadcasted_iota(jnp.int32, (c, c), 1) <= lax.broadcasted_iota(jnp.int32, (c, c), 0)
    for g in range(GM_GROUPS):
        w = jnp.where(causal, ws_ref[g], 0.0).astype(v_ref.dtype)
        bcol = bt_ref[:, g:g + 1]
        for ci in range(n_chunks):
            rows = slice(ci * c, (ci + 1) * c)
            cols = slice(g * gd, (g + 1) * gd)
            mixed = jnp.dot(w, v_ref[rows, cols], preferred_element_type=F32) + bcol
            o_ref[rows, cols] = (u_ref[rows, cols].astype(F32) * mixed).astype(o_ref.dtype)


def _gmlp_spatial(uv, w_s, b_s):
    m = uv.shape[0]
    gw = uv.shape[1] // 2
    tm = _tile(m, 4 * CHUNK)
    return pl.pallas_call(
        functools.partial(_gmlp_kernel, n_chunks=tm // CHUNK),
        grid=(m // tm,),
        in_specs=[
            pl.BlockSpec((tm, gw), lambda i: (i, 0)),
            pl.BlockSpec((tm, gw), lambda i: (i, 1)),
            pl.BlockSpec((GM_GROUPS, CHUNK, CHUNK), lambda i: (0, 0, 0)),
            pl.BlockSpec((CHUNK, GM_GROUPS), lambda i: (0, 0)),
        ],
        out_specs=pl.BlockSpec((tm, gw), lambda i: (i, 0)),
        out_shape=jax.ShapeDtypeStruct((m, gw), BF16),
        compiler_params=_params("parallel"),
        name="gmlp_spatial",
    )(uv, uv, w_s, b_s.T)


def _gmlp_first_kernel(u_ref, v_ref, w_ref, b_ref, o_ref):
    o_ref[...] = (u_ref[...] * (w_ref[...] * v_ref[...] + b_ref[...])).astype(o_ref.dtype)


def _gmlp_first_row(uv, w_s, b_s):
    m = uv.shape[0]
    gw = uv.shape[1] // 2
    gd = gw // GM_GROUPS
    w_row = jnp.repeat(w_s[:, 0, 0], gd).reshape(1, gw)
    b_row = jnp.repeat(b_s[:, 0], gd).reshape(1, gw)
    return pl.pallas_call(
        _gmlp_first_kernel,
        grid=(1,),
        in_specs=[
            pl.BlockSpec((m, gw), lambda i: (0, 0)),
            pl.BlockSpec((m, gw), lambda i: (0, 1)),
            pl.BlockSpec((1, gw), lambda i: (0, 0)),
            pl.BlockSpec((1, gw), lambda i: (0, 0)),
        ],
        out_specs=pl.BlockSpec((m, gw), lambda i: (0, 0)),
        out_shape=jax.ShapeDtypeStruct((m, gw), BF16),
        compiler_params=_params("arbitrary"),
        name="gmlp_first_row",
    )(uv, uv, w_row, b_row)


def _dsa_kernel(q_ref, qi_ref, wit_ref, qin_ref, witn_ref, k_ref, kidx_ref, vt_ref, o_ref,
                sc_ref, m_ref, acc_ref, s_ref, p_ref, *, tq, tk, topk, wscale):
    i = pl.program_id(1)
    last = pl.num_programs(1) - 1
    cur = lax.rem(i, 2)
    n_chunks = ((i + 1) * tq + tk - 1) // tk
    n_next = ((i + 2) * tq + tk - 1) // tk
    vrows = vt_ref.shape[1] // N_HEADS

    def idx_chunk(c, qi_r, wit_r, blk, slot):
        k0 = pl.multiple_of(c * tk, tk)
        kc = kidx_ref[0, pl.ds(k0, tk), :]
        acc = jnp.zeros((tk, tq), F32)
        for h in range(IDX_HEADS):
            s = lax.dot_general(kc, qi_r[0, :, h * IDX_DIM:(h + 1) * IDX_DIM], NT_DIMS,
                                preferred_element_type=F32)
            acc = acc + jnp.maximum(s, 0.0) * (wit_r[0, h:h + 1, :] * wscale)
        kpos = k0 + lax.broadcasted_iota(jnp.int32, (tk, 1), 0)
        qpos = blk * tq + lax.broadcasted_iota(jnp.int32, (1, tq), 1)
        sc_ref[slot, pl.ds(k0, tk), :] = jnp.where(kpos <= qpos, acc, -jnp.inf)

    def pad_odd(n, slot):
        @pl.when(n % 2 == 1)
        def _():
            sc_ref[slot, pl.ds(pl.multiple_of(n * tk, tk), tk), :] = jnp.full((tk, tq), -jnp.inf, F32)

    @pl.when(i == 0)
    def _():
        def body(c, carry):
            idx_chunk(c, qi_ref, wit_ref, i, cur)
            return carry

        lax.fori_loop(0, n_chunks, body, 0)
        pad_odd(n_chunks, cur)

    def count_ge(cand):
        cand_f = _key_to_float(cand)

        def body(c, cnt):
            k0 = pl.multiple_of(c * (2 * tk), 2 * tk)
            hit = jnp.where(sc_ref[cur, pl.ds(k0, 2 * tk), :] >= cand_f, 1.0, 0.0)
            return cnt + jnp.sum(hit.reshape(2 * tk // COUNT_ROWS, COUNT_ROWS, tq), axis=0)

        cnt = lax.fori_loop(0, (n_chunks + 1) // 2, body, jnp.zeros((COUNT_ROWS, tq), F32))
        return jnp.sum(cnt, axis=0, keepdims=True)

    thr = _key_to_float(_kth_largest_key(count_ge, topk, jnp.zeros((1, tq), jnp.int32)))

    m_ref[...] = jnp.full(m_ref.shape, NEG, F32)
    acc_ref[...] = jnp.zeros(acc_ref.shape, F32)

    def att_chunk(c):
        k0 = pl.multiple_of(c * tk, tk)
        bias = jnp.where(sc_ref[cur, pl.ds(k0, tk), :] >= thr, 0.0, NEG)
        m_old, m_new = [], []
        for h in range(N_HEADS):
            hs = slice(h * HEAD_DIM, (h + 1) * HEAD_DIM)
            s = lax.dot_general(k_ref[0, pl.ds(k0, tk), hs], q_ref[0, :, hs], NT_DIMS,
                                preferred_element_type=F32) + bias
            s_ref[h] = s
            m_old.append(m_ref[h])
            m_new.append(jnp.maximum(m_old[h], jnp.max(s, axis=0, keepdims=True)))
            m_ref[h] = m_new[h]
        for h in range(N_HEADS):
            p_ref[h] = jnp.exp2(s_ref[h] - m_new[h][0:1, :]).astype(BF16)
        for h in range(N_HEADS):
            pv = jnp.dot(vt_ref[0, h * vrows:(h + 1) * vrows, pl.ds(k0, tk)], p_ref[h],
                         preferred_element_type=F32)
            acc_ref[h] = jnp.exp2(m_old[h] - m_new[h])[0:1, :] * acc_ref[h] + pv

    @pl.when(i < last)
    def _():
        def fused(c, carry):
            att_chunk(c)
            idx_chunk(c, qin_ref, witn_ref, i + 1, 1 - cur)
            return carry

        def tail(c, carry):
            idx_chunk(c, qin_ref, witn_ref, i + 1, 1 - cur)
            return carry

        lax.fori_loop(0, n_chunks, fused, 0)
        lax.fori_loop(n_chunks, n_next, tail, 0)
        pad_odd(n_next, 1 - cur)

    @pl.when(i == last)
    def _():
        def body(c, carry):
            att_chunk(c)
            return carry

        lax.fori_loop(0, n_chunks, body, 0)

    for h in range(N_HEADS):
        o = acc_ref[h, 0:HEAD_DIM, :] / acc_ref[h, HEAD_DIM:HEAD_DIM + 1, :]
        o_ref[0, :, h * HEAD_DIM:(h + 1) * HEAD_DIM] = o.T.astype(o_ref.dtype)


def _dsa_prompt(q, qi, wi_t, k, kidx, v_t, tq=256, tk=256):
    b, t, aw = q.shape
    tq = _tile(t, tq)
    tk = _tile(tq, tk)
    topk = min(TOPK_MAX, t // 4)
    once = pl.Buffered(1)
    nq = t // tq

    def next_block(i):
        return jnp.minimum(i + 1, nq - 1)

    return pl.pallas_call(
        functools.partial(_dsa_kernel, tq=tq, tk=tk, topk=topk,
                          wscale=IDX_HEADS ** -0.5 * IDX_DIM ** -0.5),
        grid=(b, nq),
        in_specs=[
            pl.BlockSpec((1, tq, aw), lambda bi, i: (bi, i, 0)),
            pl.BlockSpec((1, tq, IDX_HEADS * IDX_DIM), lambda bi, i: (bi, i, 0)),
            pl.BlockSpec((1, IDX_HEADS, tq), lambda bi, i: (bi, 0, i)),
            pl.BlockSpec((1, tq, IDX_HEADS * IDX_DIM), lambda bi, i: (bi, next_block(i), 0)),
            pl.BlockSpec((1, IDX_HEADS, tq), lambda bi, i: (bi, 0, next_block(i))),
            pl.BlockSpec((1, t, aw), lambda bi, i: (bi, 0, 0), pipeline_mode=once),
            pl.BlockSpec((1, t, IDX_DIM), lambda bi, i: (bi, 0, 0), pipeline_mode=once),
            pl.BlockSpec((1, v_t.shape[1], t), lambda bi, i: (bi, 0, 0), pipeline_mode=once),
        ],
        out_specs=pl.BlockSpec((1, tq, aw), lambda bi, i: (bi, i, 0)),
        out_shape=jax.ShapeDtypeStruct((b, t, aw), BF16),
        scratch_shapes=[
            pltpu.VMEM((2, t + tk, tq), F32),
            pltpu.VMEM((N_HEADS, SUBLANES, tq), F32),
            pltpu.VMEM((N_HEADS, HEAD_DIM + ONES_ROWS, tq), F32),
            pltpu.VMEM((N_HEADS, tk, tq), F32),
            pltpu.VMEM((N_HEADS, tk, tq), BF16),
        ],
        compiler_params=_params("parallel", "arbitrary"),
        name="dsa_prompt",
    )(q, qi, wi_t, qi, wi_t, k, kidx, v_t)


def _memattn_kernel(qm_ref, mk_ref, mv_ref, o_ref):
    scale = MEM_HEAD_DIM ** -0.5
    for h in range(MEM_HEADS):
        hs = slice(h * MEM_HEAD_DIM, (h + 1) * MEM_HEAD_DIM)
        s = lax.dot_general(qm_ref[0, :, hs], mk_ref[0, :, hs].astype(BF16), NT_DIMS,
                            preferred_element_type=F32) * scale
        e = jnp.exp(s - jnp.max(s, axis=-1, keepdims=True))
        p = e / jnp.sum(e, axis=-1, keepdims=True)
        o = jnp.dot(p.astype(BF16), mv_ref[0, :, hs].astype(BF16), preferred_element_type=F32)
        o_ref[0, :, hs] = o.astype(o_ref.dtype)


def _mem_attend(qm, mk, mv):
    b, t, mw = qm.shape
    n_mem = mk.shape[1]
    tm = _tile(t, 512)
    return pl.pallas_call(
        _memattn_kernel,
        grid=(b, t // tm),
        in_specs=[
            pl.BlockSpec((1, tm, mw), lambda bi, i: (bi, i, 0)),
            pl.BlockSpec((1, n_mem, mw), lambda bi, i: (bi, 0, 0)),
            pl.BlockSpec((1, n_mem, mw), lambda bi, i: (bi, 0, 0)),
        ],
        out_specs=pl.BlockSpec((1, tm, mw), lambda bi, i: (bi, i, 0)),
        out_shape=jax.ShapeDtypeStruct((b, t, mw), BF16),
        compiler_params=_params("parallel", "parallel"),
        name="mem_attend",
    )(qm, mk, mv)


def _strict_triangle(n, lower):
    r = lax.broadcasted_iota(jnp.int32, (n, n), 0)
    c = lax.broadcasted_iota(jnp.int32, (n, n), 1)
    return jnp.where((c < r) if lower else (r < c), 1.0, 0.0).astype(BF16)


def _index_order_rank(flag):
    rows = flag.shape[0]
    within = jnp.dot(flag.astype(BF16), _strict_triangle(LANES, False), preferred_element_type=F32)
    per_row = jnp.broadcast_to(jnp.sum(flag, axis=1, keepdims=True), flag.shape)
    before = jnp.dot(_strict_triangle(rows, True), per_row.astype(BF16), preferred_element_type=F32)
    return before + within


def _sindex_kernel(pg_ref, qi_ref, wi_ref, knew_ref, cache_ref, idx_ref, kbuf, sem, sc_ref, pos_ref,
                   *, n_pages, topk, wscale, group):
    b = pl.program_id(0)
    nb = pl.num_programs(0)
    slot = lax.rem(b, 2)

    def page_copy(bb, p, sl):
        return pltpu.make_async_copy(cache_ref.at[pg_ref[bb, p]], kbuf.at[sl, p], sem.at[sl])

    def start_all(bb, sl):
        lax.fori_loop(0, n_pages, lambda p, c: (page_copy(bb, p, sl).start(), c)[1], 0)

    @pl.when(b == 0)
    def _():
        start_all(b, slot)

    @pl.when(b + 1 < nb)
    def _():
        start_all(b + 1, 1 - slot)

    lax.fori_loop(0, n_pages, lambda p, c: (page_copy(b, p, slot).wait(), c)[1], 0)

    qi = qi_ref[0]
    w = wi_ref[0] * wscale
    for g in range(n_pages // group):
        kc = kbuf[slot, g * group:(g + 1) * group].reshape(group * PAGE_SIZE, IDX_DIM).astype(BF16)
        s = lax.dot_general(qi, kc, NT_DIMS, preferred_element_type=F32)
        r = jnp.sum(jnp.maximum(s, 0.0) * w, axis=0, keepdims=True)
        for j in range(group):
            sc_ref[g * group + j:g * group + j + 1, :] = r[:, j * PAGE_SIZE:(j + 1) * PAGE_SIZE]
    s_new = jnp.sum(qi.astype(F32) * knew_ref[0].astype(BF16).astype(F32), axis=1, keepdims=True)
    s_new = jnp.sum(jnp.maximum(s_new, 0.0) * w, axis=0, keepdims=True)
    pad_rows = sc_ref.shape[0] - n_pages
    first = (lax.broadcasted_iota(jnp.int32, (pad_rows, LANES), 0) == 0) & (
        lax.broadcasted_iota(jnp.int32, (pad_rows, LANES), 1) == 0)
    sc_ref[n_pages:, :] = jnp.where(first, s_new, -jnp.inf)

    s = sc_ref[...]

    def count(hit):
        return jnp.sum(jnp.sum(jnp.where(hit, 1.0, 0.0), axis=1, keepdims=True), axis=0, keepdims=True)

    thr = _key_to_float(_kth_largest_key(lambda cand: count(s >= _key_to_float(cand)), topk,
                                         jnp.zeros((1, 1), jnp.int32)))
    above = s > thr
    tied = jnp.where(s == thr, 1.0, 0.0)
    keep_tied = (tied > 0.5) & (_index_order_rank(tied) < topk - count(above))
    chosen = jnp.where(above | keep_tied, 1.0, 0.0)
    pos_ref[...] = jnp.where(chosen > 0.5, _index_order_rank(chosen), -1.0)

    slot_id = lax.broadcasted_iota(jnp.int32, (topk, LANES), 0).astype(F32)
    lane = lax.broadcasted_iota(jnp.int32, (1, LANES), 1)

    def place(p, acc):
        key_pos = (p * PAGE_SIZE + lane).astype(F32)
        return acc + jnp.where(pos_ref[pl.ds(p, 1), :] == slot_id, key_pos, 0.0)

    acc = lax.fori_loop(0, n_pages + 1, place, jnp.zeros((topk, LANES), F32))
    idx_ref[0] = jnp.sum(acc, axis=1, keepdims=True).astype(jnp.int32)


def _sample_indices(qi, wi, kidx_new, cache_kidx, pages):
    db, n_pages = pages.shape
    past = n_pages * PAGE_SIZE
    topk = min(TOPK_MAX, (past + 1) // 4)
    assert topk % SUBLANES == 0
    group = 16 if n_pages % 16 == 0 else 1
    rows = (n_pages + 1 + SUBLANES - 1) // SUBLANES * SUBLANES
    grid_spec = pltpu.PrefetchScalarGridSpec(
        num_scalar_prefetch=1,
        grid=(db,),
        in_specs=[
            pl.BlockSpec((1, IDX_HEADS, IDX_DIM), lambda b, pg: (b, 0, 0)),
            pl.BlockSpec((1, IDX_HEADS, 1), lambda b, pg: (b, 0, 0)),
            pl.BlockSpec((1, 1, IDX_DIM), lambda b, pg: (b, 0, 0)),
            pl.BlockSpec(memory_space=pl.ANY),
        ],
        out_specs=pl.BlockSpec((1, topk, 1), lambda b, pg: (b, 0, 0)),
        scratch_shapes=[
            pltpu.VMEM((2, n_pages, PAGE_SIZE, IDX_DIM), F32),
            pltpu.SemaphoreType.DMA((2,)),
            pltpu.VMEM((rows, LANES), F32),
            pltpu.VMEM((rows, LANES), F32),
        ],
    )
    idx = pl.pallas_call(
        functools.partial(_sindex_kernel, n_pages=n_pages, topk=topk,
                          wscale=IDX_HEADS ** -0.5 * IDX_DIM ** -0.5, group=group),
        grid_spec=grid_spec,
        out_shape=jax.ShapeDtypeStruct((db, topk, 1), jnp.int32),
        compiler_params=_params("arbitrary"),
        name="sample_indices",
    )(pages, qi.reshape(db, IDX_HEADS, IDX_DIM), wi.reshape(db, IDX_HEADS, 1),
      kidx_new.reshape(db, 1, IDX_DIM), cache_kidx.reshape(-1, PAGE_SIZE, IDX_DIM))
    return idx.reshape(db, topk)


def _sgather_kernel(idx_ref, pg_ref, q_ref, knew_ref, vnew_ref, ck_ref, cv_ref, o_ref, kbuf, vbuf, sem,
                    *, past, topk):
    b = pl.program_id(0)
    nb = pl.num_programs(0)
    slot = lax.rem(b, 2)

    def start_all(bb, sl):
        def body(j, c):
            i = idx_ref[bb, j]
            ip = jnp.minimum(i, past - 1)
            page = pg_ref[bb, ip // PAGE_SIZE]
            off = lax.rem(ip, PAGE_SIZE)

            @pl.when(i >= past)
            def _():
                pltpu.make_async_copy(knew_ref.at[bb], kbuf.at[sl, j], sem.at[0, sl]).start()
                pltpu.make_async_copy(vnew_ref.at[bb], vbuf.at[sl, j], sem.at[1, sl]).start()

            @pl.when(i < past)
            def _():
                pltpu.make_async_copy(ck_ref.at[page, off], kbuf.at[sl, j], sem.at[0, sl]).start()
                pltpu.make_async_copy(cv_ref.at[page, off], vbuf.at[sl, j], sem.at[1, sl]).start()

            return c

        lax.fori_loop(0, topk, body, 0)

    @pl.when(b == 0)
    def _():
        start_all(b, slot)

    @pl.when(b + 1 < nb)
    def _():
        start_all(b + 1, 1 - slot)

    def wait_body(j, c):
        pltpu.make_async_copy(knew_ref.at[b], kbuf.at[slot, j], sem.at[0, slot]).wait()
        pltpu.make_async_copy(vnew_ref.at[b], vbuf.at[slot, j], sem.at[1, slot]).wait()
        return c

    lax.fori_loop(0, topk, wait_body, 0)

    k = kbuf[slot]
    v = vbuf[slot]
    s = jnp.sum(k * q_ref[...], axis=2, keepdims=True) * HEAD_DIM ** -0.5
    e = jnp.exp(s - jnp.max(s, axis=0, keepdims=True))
    p = e / jnp.sum(e, axis=0, keepdims=True)
    o_ref[...] = jnp.sum(p * v, axis=0, keepdims=True).astype(o_ref.dtype)


def _sample_attend(q, k_new, v_new, idx, cache_k, cache_v, pages):
    db, n_pages = pages.shape
    topk = idx.shape[1]
    tile = (N_HEADS, HEAD_DIM)
    grid_spec = pltpu.PrefetchScalarGridSpec(
        num_scalar_prefetch=2,
        grid=(db,),
        in_specs=[
            pl.BlockSpec((1,) + tile, lambda b, ix, pg: (b, 0, 0)),
            pl.BlockSpec(memory_space=pl.ANY),
            pl.BlockSpec(memory_space=pl.ANY),
            pl.BlockSpec(memory_space=pl.ANY),
            pl.BlockSpec(memory_space=pl.ANY),
        ],
        out_specs=pl.BlockSpec((1,) + tile, lambda b, ix, pg: (b, 0, 0)),
        scratch_shapes=[
            pltpu.VMEM((2, topk) + tile, F32),
            pltpu.VMEM((2, topk) + tile, F32),
            pltpu.SemaphoreType.DMA((2, 2)),
        ],
    )
    out = pl.pallas_call(
        functools.partial(_sgather_kernel, past=n_pages * PAGE_SIZE, topk=topk),
        grid_spec=grid_spec,
        out_shape=jax.ShapeDtypeStruct((db,) + tile, BF16),
        compiler_params=_params("arbitrary"),
        name="sample_attend",
    )(idx, pages, q.reshape((db,) + tile), k_new.reshape((db,) + tile), v_new.reshape((db,) + tile),
      cache_k.reshape((-1, PAGE_SIZE) + tile), cache_v.reshape((-1, PAGE_SIZE) + tile))
    return out.reshape(db, N_HEADS * HEAD_DIM)


def _merge_kernel(x_ref, g_ref, ya_ref, yb_ref, ym_ref, woa_ref, wob_ref, wom_ref, wout_ref,
                  gpm_ref, gpf_ref, x1_ref, h2_ref):
    d = x_ref.shape[1]
    merged = (g_ref[:, 0:d].astype(F32) * jnp.dot(ya_ref[...], woa_ref[...], preferred_element_type=F32)
              + g_ref[:, d:2 * d].astype(F32) * jnp.dot(yb_ref[...], wob_ref[...], preferred_element_type=F32)
              + g_ref[:, 2 * d:3 * d].astype(F32) * jnp.dot(ym_ref[...], wom_ref[...], preferred_element_type=F32))
    o = jnp.dot(merged.astype(BF16), wout_ref[...], preferred_element_type=F32)
    x1 = x_ref[...] + _rmsnorm_rows(o, gpm_ref[...])
    x1_ref[...] = x1
    h2_ref[...] = _rmsnorm_rows(x1, gpf_ref[...]).astype(h2_ref.dtype)


def _merge(x, gates, y_a, y_b, y_m, w_oa, w_ob, w_om, w_out, g_post_mix, g_pre_ffn):
    m, d = x.shape
    tm = _tile(m, 256)
    once = pl.Buffered(1)

    def rows(width):
        return pl.BlockSpec((tm, width), lambda i: (i, 0))

    def whole(a):
        return pl.BlockSpec(a.shape, lambda i: (0, 0), pipeline_mode=once)

    return pl.pallas_call(
        _merge_kernel,
        grid=(m // tm,),
        in_specs=[rows(d), rows(N_BRANCH * d), rows(y_a.shape[1]), rows(y_b.shape[1]), rows(y_m.shape[1]),
                  whole(w_oa), whole(w_ob), whole(w_om), whole(w_out),
                  pl.BlockSpec((1, d), lambda i: (0, 0)), pl.BlockSpec((1, d), lambda i: (0, 0))],
        out_specs=[rows(d), rows(d)],
        out_shape=[jax.ShapeDtypeStruct((m, d), F32), jax.ShapeDtypeStruct((m, d), BF16)],
        compiler_params=_params("parallel"),
        name="merge",
    )(x, gates, y_a, y_b, y_m, w_oa, w_ob, w_om, w_out, g_post_mix.reshape(1, d), g_pre_ffn.reshape(1, d))


def _ffn_up_kernel(h_ref, wg_ref, wu_ref, o_ref, *, rows):
    wg = wg_ref[...].astype(BF16)
    wu = wu_ref[...].astype(BF16)

    def body(r, carry):
        r0 = pl.multiple_of(r * rows, rows)
        h = h_ref[pl.ds(r0, rows), :]
        a = jnp.dot(h, wg, preferred_element_type=F32)
        c = jnp.dot(h, wu, preferred_element_type=F32)
        o_ref[pl.ds(r0, rows), :] = (jax.nn.silu(a) * c).astype(o_ref.dtype)
        return carry

    lax.fori_loop(0, h_ref.shape[0] // rows, body, 0)


def _ffn_up(h2, w_gu):
    m, d = h2.shape
    d_ff = w_gu.shape[1] // 2
    tm = _tile(m, FFN_UP_TM)
    tf = _tile(d_ff, 2 * LANES)
    nf = d_ff // tf
    return pl.pallas_call(
        functools.partial(_ffn_up_kernel, rows=_tile(tm, 2048)),
        grid=(m // tm, nf),
        in_specs=[
            pl.BlockSpec((tm, d), lambda i, j: (i, 0)),
            pl.BlockSpec((d, tf), lambda i, j: (0, j)),
            pl.BlockSpec((d, tf), lambda i, j: (0, j + nf)),
        ],
        out_specs=pl.BlockSpec((tm, tf), lambda i, j: (i, j)),
        out_shape=jax.ShapeDtypeStruct((m, d_ff), BF16),
        compiler_params=_params("parallel", "parallel"),
        name="ffn_up",
    )(h2, w_gu, w_gu)


def _ffn_down_kernel(a_ref, wd_ref, x1_ref, gpf_ref, o_ref):
    y = jnp.dot(a_ref[...], wd_ref[...], preferred_element_type=F32)
    o_ref[...] = x1_ref[...] + _rmsnorm_rows(y, gpf_ref[...])


def _ffn_down(act, x1, w_down, g_post_ffn):
    m, d_ff = act.shape
    d = w_down.shape[1]
    tm = _tile(m, 256)
    return pl.pallas_call(
        _ffn_down_kernel,
        grid=(m // tm,),
        in_specs=[
            pl.BlockSpec((tm, d_ff), lambda i: (i, 0)),
            pl.BlockSpec((d_ff, d), lambda i: (0, 0), pipeline_mode=pl.Buffered(1)),
            pl.BlockSpec((tm, d), lambda i: (i, 0)),
            pl.BlockSpec((1, d), lambda i: (0, 0)),
        ],
        out_specs=pl.BlockSpec((tm, d), lambda i: (i, 0)),
        out_shape=jax.ShapeDtypeStruct((m, d), F32),
        compiler_params=_params("parallel"),
        name="ffn_down",
    )(act, w_down, x1, g_post_ffn.reshape(1, d))


def _split_w_in(w_in, d):
    w_t = w_in.T
    o_qm = d + 3 * N_HEADS * HEAD_DIM + IDX_HEADS * IDX_DIM + IDX_DIM + IDX_HEADS
    o_g = o_qm + MEM_HEADS * MEM_HEAD_DIM
    return w_t, w_t[o_qm:o_g].astype(BF16), w_t[o_g:].astype(BF16)


def _project_common(x2d, g_pre_mix, ws, ln_g, ln_b, uv_dtype, q_dtype, q_scale, v_heads_t):
    w_t, w_qm, w_g = ws
    d = x2d.shape[1]
    aw, iw = N_HEADS * HEAD_DIM, IDX_HEADS * IDX_DIM
    h = _rmsnorm(x2d, g_pre_mix, BF16)
    uv = _project_uv(h, w_t, ln_g, ln_b, uv_dtype)
    (q,) = _matmul(h, w_t, (q_dtype,), scale=q_scale, rows=(d, aw), name="proj_q")
    k, k_bf = _matmul(h, w_t, (F32, BF16), rows=(d + aw, aw), name="proj_k")
    v_out = _matmul(h, w_t, (F32,), rows=(d + 2 * aw, aw), heads_t=v_heads_t, name="proj_v")
    v, v_t = v_out if v_heads_t is not None else (v_out[0], None)
    (qi,) = _matmul(h, w_t, (BF16,), rows=(d + 3 * aw, iw), name="proj_qi")
    (kw,) = _matmul(h, w_t, (F32,), rows=(d + 3 * aw + iw, 2 * LANES), name="proj_kidx")
    (qm,) = _matmul(h, w_qm, (BF16,), name="proj_qm")
    (gates,) = _matmul(h, w_g, (BF16,), act="sigmoid", name="proj_gates")
    kidx = kw[:, :IDX_DIM]
    wi = kw[:, IDX_DIM:IDX_DIM + IDX_HEADS]
    return uv, q, k, k_bf, v, v_t, qi, kidx, wi, qm, gates


def kernel(x_prompt, x_sample, mem_prompt, cache_k, cache_v, cache_kidx, cache_mem_k, cache_mem_v, page_table, g_pre_mix, g_post_mix, g_pre_ffn, g_post_ffn, g_mem, ln_v_g, ln_v_b, w_in, w_s, b_s, w_oa, w_ob, w_om, w_out, w_mem_kv, w_gu, w_down):
    depth = w_in.shape[0]
    b, t, d = x_prompt.shape
    db, dt, _ = x_sample.shape
    assert dt == 1, "sample path handles one new token per sequence"
    n_mem = mem_prompt.shape[1]
    aw = N_HEADS * HEAD_DIM
    mw = MEM_HEADS * MEM_HEAD_DIM
    gw = d // 2
    past = page_table.shape[1] * PAGE_SIZE

    yp = x_prompt.reshape(b * t, d)
    ys = x_sample.reshape(db * dt, d)
    outs = [[] for _ in range(9)]
    for l in range(depth):
        ws = _split_w_in(w_in[l], d)
        woa, wob, wom, wout = (w.astype(BF16) for w in (w_oa[l], w_ob[l], w_om[l], w_out[l]))
        wdown, wmem = w_down[l].astype(BF16), w_mem_kv[l].T.astype(BF16)

        uv, q, k, k_bf, v, v_t, qi, kidx, wi, qm, gates = _project_common(
            yp, g_pre_mix[l], ws, ln_v_g[l], ln_v_b[l], BF16, BF16, DSA_LOG2_SCALE, (b, BF16))
        y_a = _gmlp_spatial(uv, w_s[l], b_s[l])
        y_b = _dsa_prompt(
            q.reshape(b, t, aw), qi.reshape(b, t, -1), jnp.swapaxes(wi.reshape(b, t, IDX_HEADS), 1, 2),
            k_bf.reshape(b, t, aw), kidx.astype(BF16).reshape(b, t, IDX_DIM), v_t).reshape(b * t, aw)
        hm = _rmsnorm(mem_prompt.reshape(b * n_mem, d), g_mem[l], BF16)
        (mkv,) = _matmul(hm, wmem, (F32,), name="proj_mem_kv")
        mk = mkv[:, :mw].reshape(b, n_mem, mw)
        mv = mkv[:, mw:].reshape(b, n_mem, mw)
        y_m = _mem_attend(qm.reshape(b, t, mw), mk, mv).reshape(b * t, mw)
        x1, h2 = _merge(yp, gates, y_a, y_b, y_m, woa, wob, wom, wout, g_post_mix[l], g_pre_ffn[l])
        yp = _ffn_down(_ffn_up(h2, w_gu[l]), x1, wdown, g_post_ffn[l])
        outs[0].append(k.reshape(b, t, N_HEADS, HEAD_DIM))
        outs[1].append(v.reshape(b, t, N_HEADS, HEAD_DIM))
        outs[2].append(kidx.reshape(b, t, IDX_DIM))
        outs[3].append(mk.reshape(b, n_mem, MEM_HEADS, MEM_HEAD_DIM))
        outs[4].append(mv.reshape(b, n_mem, MEM_HEADS, MEM_HEAD_DIM))

        uv, q, k, _, v, _, qi, kidx, wi, qm, gates = _project_common(
            ys, g_pre_mix[l], ws, ln_v_g[l], ln_v_b[l], F32, F32, None, None)
        y_a = _gmlp_first_row(uv, w_s[l], b_s[l])
        pages = page_table + l * cache_k.shape[1]
        idx = _sample_indices(qi, wi, kidx, cache_kidx, pages)
        y_b = _sample_attend(q, k, v, idx, cache_k, cache_v, pages)
        qm_pad = jnp.pad(qm.reshape(db, 1, mw), ((0, 0), (0, 2 * SUBLANES - 1), (0, 0)))
        y_m = _mem_attend(qm_pad, cache_mem_k[l].reshape(db, n_mem, mw),
                          cache_mem_v[l].reshape(db, n_mem, mw))[:, 0, :]
        x1, h2 = _merge(ys, gates, y_a, y_b, y_m, woa, wob, wom, wout, g_post_mix[l], g_pre_ffn[l])
        ys = _ffn_down(_ffn_up(h2, w_gu[l]), x1, wdown, g_post_ffn[l])
        outs[5].append(k.reshape(db, dt, N_HEADS, HEAD_DIM))
        outs[6].append(v.reshape(db, dt, N_HEADS, HEAD_DIM))
        outs[7].append(kidx.reshape(db, dt, IDX_DIM))
        outs[8].append(uv[:, gw:].reshape(db, dt, gw))

    return (yp.reshape(b, t, d), ys.reshape(db, dt, d)) + tuple(jnp.stack(o) for o in outs)
```

```python
import functools

import jax
import jax.numpy as jnp
from jax import lax
from jax.experimental import pallas as pl
from jax.experimental.pallas import tpu as pltpu

EPS = 1e-6
CHUNK = 128
GM_GROUPS = 8
N_HEADS = 8
HEAD_DIM = 128
IDX_HEADS = 16
IDX_DIM = 128
TOPK_MAX = 256
MEM_HEADS = 4
MEM_HEAD_DIM = 128
N_BRANCH = 3
PAGE_SIZE = 128

LANES = 128
SUBLANES = 8
VMEM_LIMIT_BYTES = 56 * 2**20
INT_MIN = -2**31
KEY_LOWEST = INT_MIN + 0x800000
KEY_INF = 0x7F800000
NEG = -1e30
LOG2_E = 1.4426950408889634
DSA_LOG2_SCALE = HEAD_DIM ** -0.5 * LOG2_E
ONES_ROWS = 16
COUNT_ROWS = 64
MM_TM, MM_TN = 2048, 1024
FFN_UP_TM = 4096

F32 = jnp.float32
BF16 = jnp.bfloat16
NT_DIMS = (((1,), (1,)), ((), ()))


def _params(*sem):
    return pltpu.CompilerParams(dimension_semantics=sem, vmem_limit_bytes=VMEM_LIMIT_BYTES)


def _tile(n, pref):
    if n <= pref:
        return n
    t = pref
    while n % t:
        t //= 2
    return t


def _key_to_float(key):
    key = jnp.clip(key, KEY_LOWEST, KEY_INF)
    return pltpu.bitcast(key ^ ((key >> 31) & 0x7FFFFFFF), F32)


def _kth_largest_key(count_ge, topk, like):
    base = jnp.where(count_ge(jnp.zeros_like(like)) >= topk, 0, INT_MIN).astype(jnp.int32)

    def bit_body(b, t):
        cand = t | jnp.left_shift(jnp.int32(1), 30 - b)
        return jnp.where(count_ge(cand) >= topk, cand, t)

    return lax.fori_loop(0, 31, bit_body, base)


def _rmsnorm_rows(x, g):
    return x * lax.rsqrt(jnp.mean(x * x, axis=-1, keepdims=True) + EPS) * g


def _rmsnorm_kernel(x_ref, g_ref, o_ref):
    o_ref[...] = _rmsnorm_rows(x_ref[...], g_ref[...]).astype(o_ref.dtype)


def _rmsnorm(x, g, out_dtype):
    m, d = x.shape
    tm = _tile(m, 512)
    return pl.pallas_call(
        _rmsnorm_kernel,
        grid=(m // tm,),
        in_specs=[pl.BlockSpec((tm, d), lambda i: (i, 0)), pl.BlockSpec((1, d), lambda i: (0, 0))],
        out_specs=pl.BlockSpec((tm, d), lambda i: (i, 0)),
        out_shape=jax.ShapeDtypeStruct((m, d), out_dtype),
        compiler_params=_params("parallel"),
        name="rmsnorm",
    )(x, g.reshape(1, d))


def _mm_kernel(x_ref, w_ref, *o_refs, act, scale, heads_t):
    acc = lax.dot_general(x_ref[...], w_ref[...].astype(BF16), NT_DIMS, preferred_element_type=F32)
    if act == "sigmoid":
        acc = jax.nn.sigmoid(acc)
    if scale is not None:
        acc = acc * scale
    if heads_t:
        o_refs, t_ref = o_refs[:-1], o_refs[-1]
        rows = HEAD_DIM + ONES_ROWS
        for h in range(acc.shape[1] // HEAD_DIM):
            t_ref[0, h * rows:h * rows + HEAD_DIM, :] = acc[:, h * HEAD_DIM:(h + 1) * HEAD_DIM].T.astype(t_ref.dtype)
            t_ref[0, h * rows + HEAD_DIM:(h + 1) * rows, :] = jnp.ones((ONES_ROWS, acc.shape[0]), t_ref.dtype)
    for o_ref in o_refs:
        o_ref[...] = acc.astype(o_ref.dtype)


def _matmul(x, w_t, out_dtypes, act=None, scale=None, rows=None, heads_t=None, name="matmul"):
    m, k = x.shape
    row0, n = rows if rows is not None else (0, w_t.shape[0])
    tm = _tile(m, MM_TM)
    tn = _tile(n, MM_TN if w_t.dtype == BF16 else MM_TN // 2)
    assert row0 % SUBLANES == 0
    out_specs = [pl.BlockSpec((tm, tn), lambda i, j: (i, j)) for _ in out_dtypes]
    out_shapes = [jax.ShapeDtypeStruct((m, n), dt) for dt in out_dtypes]
    if heads_t is not None:
        batch, dt = heads_t
        per_batch = m // batch // tm
        t_rows = tn // HEAD_DIM * (HEAD_DIM + ONES_ROWS)
        out_specs.append(pl.BlockSpec((1, t_rows, tm), lambda i, j: (i // per_batch, j, i % per_batch)))
        out_shapes.append(jax.ShapeDtypeStruct((batch, n // tn * t_rows, m // batch), dt))
    outs = pl.pallas_call(
        functools.partial(_mm_kernel, act=act, scale=scale, heads_t=heads_t is not None),
        grid=(m // tm, n // tn),
        in_specs=[pl.BlockSpec((tm, k), lambda i, j: (i, 0)),
                  pl.BlockSpec((pl.Element(tn), pl.Element(k)),
                               lambda i, j: (pl.multiple_of(row0 + j * tn, SUBLANES), 0))],
        out_specs=out_specs,
        out_shape=out_shapes,
        compiler_params=_params("parallel", "parallel"),
        name=name,
    )(x, w_t)
    return outs


def _uv_kernel(x_ref, w_ref, lng_ref, lnb_ref, o_ref):
    y = jax.nn.gelu(lax.dot_general(x_ref[...], w_ref[...].astype(BF16), NT_DIMS,
                                    preferred_element_type=F32))

    @pl.when(pl.program_id(1) == 0)
    def _():
        o_ref[...] = y.astype(o_ref.dtype)

    @pl.when(pl.program_id(1) == 1)
    def _():
        yc = y - jnp.mean(y, axis=-1, keepdims=True)
        yn = yc * lax.rsqrt(jnp.mean(yc * yc, axis=-1, keepdims=True) + EPS)
        o_ref[...] = (yn * lng_ref[...] + lnb_ref[...]).astype(o_ref.dtype)


def _project_uv(h, w_t, ln_g, ln_b, out_dtype):
    m, k = h.shape
    gw = ln_g.shape[0]
    tm = _tile(m, 1024)
    return pl.pallas_call(
        _uv_kernel,
        grid=(m // tm, 2),
        in_specs=[
            pl.BlockSpec((tm, k), lambda i, j: (i, 0)),
            pl.BlockSpec((gw, k), lambda i, j: (j, 0)),
            pl.BlockSpec((1, gw), lambda i, j: (0, 0)),
            pl.BlockSpec((1, gw), lambda i, j: (0, 0)),
        ],
        out_specs=pl.BlockSpec((tm, gw), lambda i, j: (i, j)),
        out_shape=jax.ShapeDtypeStruct((m, 2 * gw), out_dtype),
        compiler_params=_params("parallel", "arbitrary"),
        name="project_uv",
    )(h, w_t, ln_g.reshape(1, gw), ln_b.reshape(1, gw))


def _gmlp_kernel(u_ref, v_ref, ws_ref, bt_ref, o_ref, *, n_chunks):
    c = CHUNK
    gd = u_ref.shape[1] // GM_GROUPS
    causal = lax.broadcasted_iota(jnp.int32, (c, c), 1) <= lax.broadcasted_iota(jnp.int32, (c, c), 0)
    for g in range(GM_GROUPS):
        w = jnp.where(causal, ws_ref[g], 0.0).astype(v_ref.dtype)
        bcol = bt_ref[:, g:g + 1]
        for ci in range(n_chunks):
            rows = slice(ci * c, (ci + 1) * c)
            cols = slice(g * gd, (g + 1) * gd)
            mixed = jnp.dot(w, v_ref[rows, cols], preferred_element_type=F32) + bcol
            o_ref[rows, cols] = (u_ref[rows, cols].astype(F32) * mixed).astype(o_ref.dtype)


def _gmlp_spatial(uv, w_s, b_s):
    m = uv.shape[0]
    gw = uv.shape[1] // 2
    tm = _tile(m, 4 * CHUNK)
    return pl.pallas_call(
        functools.partial(_gmlp_kernel, n_chunks=tm // CHUNK),
        grid=(m // tm,),
        in_specs=[
            pl.BlockSpec((tm, gw), lambda i: (i, 0)),
            pl.BlockSpec((tm, gw), lambda i: (i, 1)),
            pl.BlockSpec((GM_GROUPS, CHUNK, CHUNK), lambda i: (0, 0, 0)),
            pl.BlockSpec((CHUNK, GM_GROUPS), lambda i: (0, 0)),
        ],
        out_specs=pl.BlockSpec((tm, gw), lambda i: (i, 0)),
        out_shape=jax.ShapeDtypeStruct((m, gw), BF16),
        compiler_params=_params("parallel"),
        name="gmlp_spatial",
    )(uv, uv, w_s, b_s.T)


def _gmlp_first_kernel(u_ref, v_ref, w_ref, b_ref, o_ref):
    o_ref[...] = (u_ref[...] * (w_ref[...] * v_ref[...] + b_ref[...])).astype(o_ref.dtype)


def _gmlp_first_row(uv, w_s, b_s):
    m = uv.shape[0]
    gw = uv.shape[1] // 2
    gd = gw // GM_GROUPS
    w_row = jnp.repeat(w_s[:, 0, 0], gd).reshape(1, gw)
    b_row = jnp.repeat(b_s[:, 0], gd).reshape(1, gw)
    return pl.pallas_call(
        _gmlp_first_kernel,
        grid=(1,),
        in_specs=[
            pl.BlockSpec((m, gw), lambda i: (0, 0)),
            pl.BlockSpec((m, gw), lambda i: (0, 1)),
            pl.BlockSpec((1, gw), lambda i: (0, 0)),
            pl.BlockSpec((1, gw), lambda i: (0, 0)),
        ],
        out_specs=pl.BlockSpec((m, gw), lambda i: (0, 0)),
        out_shape=jax.ShapeDtypeStruct((m, gw), BF16),
        compiler_params=_params("arbitrary"),
        name="gmlp_first_row",
    )(uv, uv, w_row, b_row)


def _dsa_kernel(q_ref, qi_ref, wit_ref, qin_ref, witn_ref, k_ref, kidx_ref, vt_ref, o_ref,
                sc_ref, m_ref, acc_ref, s_ref, p_ref, *, tq, tk, topk, wscale):
    i = pl.program_id(1)
    last = pl.num_programs(1) - 1
    cur = lax.rem(i, 2)
    n_chunks = ((i + 1) * tq + tk - 1) // tk
    n_next = ((i + 2) * tq + tk - 1) // tk
    vrows = vt_ref.shape[1] // N_HEADS

    def idx_chunk(c, qi_r, wit_r, blk, slot):
        k0 = pl.multiple_of(c * tk, tk)
        kc = kidx_ref[0, pl.ds(k0, tk), :]
        acc = jnp.zeros((tk, tq), F32)
        for h in range(IDX_HEADS):
            s = lax.dot_general(kc, qi_r[0, :, h * IDX_DIM:(h + 1) * IDX_DIM], NT_DIMS,
                                preferred_element_type=F32)
            acc = acc + jnp.maximum(s, 0.0) * (wit_r[0, h:h + 1, :] * wscale)
        kpos = k0 + lax.broadcasted_iota(jnp.int32, (tk, 1), 0)
        qpos = blk * tq + lax.broadcasted_iota(jnp.int32, (1, tq), 1)
        sc_ref[slot, pl.ds(k0, tk), :] = jnp.where(kpos <= qpos, acc, -jnp.inf)

    def pad_odd(n, slot):
        @pl.when(n % 2 == 1)
        def _():
            sc_ref[slot, pl.ds(pl.multiple_of(n * tk, tk), tk), :] = jnp.full((tk, tq), -jnp.inf, F32)

    @pl.when(i == 0)
    def _():
        def body(c, carry):
            idx_chunk(c, qi_ref, wit_ref, i, cur)
            return carry

        lax.fori_loop(0, n_chunks, body, 0)
        pad_odd(n_chunks, cur)

    def count_ge(cand):
        cand_f = _key_to_float(cand)

        def body(c, cnt):
            k0 = pl.multiple_of(c * (2 * tk), 2 * tk)
            hit = jnp.where(sc_ref[cur, pl.ds(k0, 2 * tk), :] >= cand_f, 1.0, 0.0)
            return cnt + jnp.sum(hit.reshape(2 * tk // COUNT_ROWS, COUNT_ROWS, tq), axis=0)

        cnt = lax.fori_loop(0, (n_chunks + 1) // 2, body, jnp.zeros((COUNT_ROWS, tq), F32))
        return jnp.sum(cnt, axis=0, keepdims=True)

    thr = _key_to_float(_kth_largest_key(count_ge, topk, jnp.zeros((1, tq), jnp.int32)))

    m_ref[...] = jnp.full(m_ref.shape, NEG, F32)
    acc_ref[...] = jnp.zeros(acc_ref.shape, F32)

    def att_chunk(c):
        k0 = pl.multiple_of(c * tk, tk)
        bias = jnp.where(sc_ref[cur, pl.ds(k0, tk), :] >= thr, 0.0, NEG)
        m_old, m_new = [], []
        for h in range(N_HEADS):
            hs = slice(h * HEAD_DIM, (h + 1) * HEAD_DIM)
            s = lax.dot_general(k_ref[0, pl.ds(k0, tk), hs], q_ref[0, :, hs], NT_DIMS,
                                preferred_element_type=F32) + bias
            s_ref[h] = s
            m_old.append(m_ref[h])
            m_new.append(jnp.maximum(m_old[h], jnp.max(s, axis=0, keepdims=True)))
            m_ref[h] = m_new[h]
        for h in range(N_HEADS):
            p_ref[h] = jnp.exp2(s_ref[h] - m_new[h][0:1, :]).astype(BF16)
        for h in range(N_HEADS):
            pv = jnp.dot(vt_ref[0, h * vrows:(h + 1) * vrows, pl.ds(k0, tk)], p_ref[h],
                         preferred_element_type=F32)
            acc_ref[h] = jnp.exp2(m_old[h] - m_new[h])[0:1, :] * acc_ref[h] + pv

    @pl.when(i < last)
    def _():
        def fused(c, carry):
            att_chunk(c)
            idx_chunk(c, qin_ref, witn_ref, i + 1, 1 - cur)
            return carry

        def tail(c, carry):
            idx_chunk(c, qin_ref, witn_ref, i + 1, 1 - cur)
            return carry

        lax.fori_loop(0, n_chunks, fused, 0)
        lax.fori_loop(n_chunks, n_next, tail, 0)
        pad_odd(n_next, 1 - cur)

    @pl.when(i == last)
    def _():
        def body(c, carry):
            att_chunk(c)
            return carry

        lax.fori_loop(0, n_chunks, body, 0)

    for h in range(N_HEADS):
        o = acc_ref[h, 0:HEAD_DIM, :] / acc_ref[h, HEAD_DIM:HEAD_DIM + 1, :]
        o_ref[0, :, h * HEAD_DIM:(h + 1) * HEAD_DIM] = o.T.astype(o_ref.dtype)


def _dsa_prompt(q, qi, wi_t, k, kidx, v_t, tq=256, tk=256):
    b, t, aw = q.shape
    tq = _tile(t, tq)
    tk = _tile(tq, tk)
    topk = min(TOPK_MAX, t // 4)
    once = pl.Buffered(1)
    nq = t // tq

    def next_block(i):
        return jnp.minimum(i + 1, nq - 1)

    return pl.pallas_call(
        functools.partial(_dsa_kernel, tq=tq, tk=tk, topk=topk,
                          wscale=IDX_HEADS ** -0.5 * IDX_DIM ** -0.5),
        grid=(b, nq),
        in_specs=[
            pl.BlockSpec((1, tq, aw), lambda bi, i: (bi, i, 0)),
            pl.BlockSpec((1, tq, IDX_HEADS * IDX_DIM), lambda bi, i: (bi, i, 0)),
            pl.BlockSpec((1, IDX_HEADS, tq), lambda bi, i: (bi, 0, i)),
            pl.BlockSpec((1, tq, IDX_HEADS * IDX_DIM), lambda bi, i: (bi, next_block(i), 0)),
            pl.BlockSpec((1, IDX_HEADS, tq), lambda bi, i: (bi, 0, next_block(i))),
            pl.BlockSpec((1, t, aw), lambda bi, i: (bi, 0, 0), pipeline_mode=once),
            pl.BlockSpec((1, t, IDX_DIM), lambda bi, i: (bi, 0, 0), pipeline_mode=once),
            pl.BlockSpec((1, v_t.shape[1], t), lambda bi, i: (bi, 0, 0), pipeline_mode=once),
        ],
        out_specs=pl.BlockSpec((1, tq, aw), lambda bi, i: (bi, i, 0)),
        out_shape=jax.ShapeDtypeStruct((b, t, aw), BF16),
        scratch_shapes=[
            pltpu.VMEM((2, t + tk, tq), F32),
            pltpu.VMEM((N_HEADS, SUBLANES, tq), F32),
            pltpu.VMEM((N_HEADS, HEAD_DIM + ONES_ROWS, tq), F32),
            pltpu.VMEM((N_HEADS, tk, tq), F32),
            pltpu.VMEM((N_HEADS, tk, tq), BF16),
        ],
        compiler_params=_params("parallel", "arbitrary"),
        name="dsa_prompt",
    )(q, qi, wi_t, qi, wi_t, k, kidx, v_t)


def _memattn_kernel(qm_ref, mk_ref, mv_ref, o_ref):
    scale = MEM_HEAD_DIM ** -0.5
    for h in range(MEM_HEADS):
        hs = slice(h * MEM_HEAD_DIM, (h + 1) * MEM_HEAD_DIM)
        s = lax.dot_general(qm_ref[0, :, hs], mk_ref[0, :, hs].astype(BF16), NT_DIMS,
                            preferred_element_type=F32) * scale
        e = jnp.exp(s - jnp.max(s, axis=-1, keepdims=True))
        p = e / jnp.sum(e, axis=-1, keepdims=True)
        o = jnp.dot(p.astype(BF16), mv_ref[0, :, hs].astype(BF16), preferred_element_type=F32)
        o_ref[0, :, hs] = o.astype(o_ref.dtype)


def _mem_attend(qm, mk, mv):
    b, t, mw = qm.shape
    n_mem = mk.shape[1]
    tm = _tile(t, 512)
    return pl.pallas_call(
        _memattn_kernel,
        grid=(b, t // tm),
        in_specs=[
            pl.BlockSpec((1, tm, mw), lambda bi, i: (bi, i, 0)),
            pl.BlockSpec((1, n_mem, mw), lambda bi, i: (bi, 0, 0)),
            pl.BlockSpec((1, n_mem, mw), lambda bi, i: (bi, 0, 0)),
        ],
        out_specs=pl.BlockSpec((1, tm, mw), lambda bi, i: (bi, i, 0)),
        out_shape=jax.ShapeDtypeStruct((b, t, mw), BF16),
        compiler_params=_params("parallel", "parallel"),
        name="mem_attend",
    )(qm, mk, mv)


def _strict_triangle(n, lower):
    r = lax.broadcasted_iota(jnp.int32, (n, n), 0)
    c = lax.broadcasted_iota(jnp.int32, (n, n), 1)
    return jnp.where((c < r) if lower else (r < c), 1.0, 0.0).astype(BF16)


def _index_order_rank(flag):
    rows = flag.shape[0]
    within = jnp.dot(flag.astype(BF16), _strict_triangle(LANES, False), preferred_element_type=F32)
    per_row = jnp.broadcast_to(jnp.sum(flag, axis=1, keepdims=True), flag.shape)
    before = jnp.dot(_strict_triangle(rows, True), per_row.astype(BF16), preferred_element_type=F32)
    return before + within


def _sindex_kernel(pg_ref, qi_ref, wi_ref, knew_ref, cache_ref, idx_ref, kbuf, sem, sc_ref, pos_ref,
                   *, n_pages, topk, wscale, group):
    b = pl.program_id(0)
    nb = pl.num_programs(0)
    slot = lax.rem(b, 2)

    def page_copy(bb, p, sl):
        return pltpu.make_async_copy(cache_ref.at[pg_ref[bb, p]], kbuf.at[sl, p], sem.at[sl])

    def start_all(bb, sl):
        lax.fori_loop(0, n_pages, lambda p, c: (page_copy(bb, p, sl).start(), c)[1], 0)

    @pl.when(b == 0)
    def _():
        start_all(b, slot)

    @pl.when(b + 1 < nb)
    def _():
        start_all(b + 1, 1 - slot)

    lax.fori_loop(0, n_pages, lambda p, c: (page_copy(b, p, slot).wait(), c)[1], 0)

    qi = qi_ref[0]
    w = wi_ref[0] * wscale
    for g in range(n_pages // group):
        kc = kbuf[slot, g * group:(g + 1) * group].reshape(group * PAGE_SIZE, IDX_DIM).astype(BF16)
        s = lax.dot_general(qi, kc, NT_DIMS, preferred_element_type=F32)
        r = jnp.sum(jnp.maximum(s, 0.0) * w, axis=0, keepdims=True)
        for j in range(group):
            sc_ref[g * group + j:g * group + j + 1, :] = r[:, j * PAGE_SIZE:(j + 1) * PAGE_SIZE]
    s_new = jnp.sum(qi.astype(F32) * knew_ref[0].astype(BF16).astype(F32), axis=1, keepdims=True)
    s_new = jnp.sum(jnp.maximum(s_new, 0.0) * w, axis=0, keepdims=True)
    pad_rows = sc_ref.shape[0] - n_pages
    first = (lax.broadcasted_iota(jnp.int32, (pad_rows, LANES), 0) == 0) & (
        lax.broadcasted_iota(jnp.int32, (pad_rows, LANES), 1) == 0)
    sc_ref[n_pages:, :] = jnp.where(first, s_new, -jnp.inf)

    s = sc_ref[...]

    def count(hit):
        return jnp.sum(jnp.sum(jnp.where(hit, 1.0, 0.0), axis=1, keepdims=True), axis=0, keepdims=True)

    thr = _key_to_float(_kth_largest_key(lambda cand: count(s >= _key_to_float(cand)), topk,
                                         jnp.zeros((1, 1), jnp.int32)))
    above = s > thr
    tied = jnp.where(s == thr, 1.0, 0.0)
    keep_tied = (tied > 0.5) & (_index_order_rank(tied) < topk - count(above))
    chosen = jnp.where(above | keep_tied, 1.0, 0.0)
    pos_ref[...] = jnp.where(chosen > 0.5, _index_order_rank(chosen), -1.0)

    slot_id = lax.broadcasted_iota(jnp.int32, (topk, LANES), 0).astype(F32)
    lane = lax.broadcasted_iota(jnp.int32, (1, LANES), 1)

    def place(p, acc):
        key_pos = (p * PAGE_SIZE + lane).astype(F32)
        return acc + jnp.where(pos_ref[pl.ds(p, 1), :] == slot_id, key_pos, 0.0)

    acc = lax.fori_loop(0, n_pages + 1, place, jnp.zeros((topk, LANES), F32))
    idx_ref[0] = jnp.sum(acc, axis=1, keepdims=True).astype(jnp.int32)


def _sample_indices(qi, wi, kidx_new, cache_kidx, pages):
    db, n_pages = pages.shape
    past = n_pages * PAGE_SIZE
    topk = min(TOPK_MAX, (past + 1) // 4)
    assert topk % SUBLANES == 0
    group = 16 if n_pages % 16 == 0 else 1
    rows = (n_pages + 1 + SUBLANES - 1) // SUBLANES * SUBLANES
    grid_spec = pltpu.PrefetchScalarGridSpec(
        num_scalar_prefetch=1,
        grid=(db,),
        in_specs=[
            pl.BlockSpec((1, IDX_HEADS, IDX_DIM), lambda b, pg: (b, 0, 0)),
            pl.BlockSpec((1, IDX_HEADS, 1), lambda b, pg: (b, 0, 0)),
            pl.BlockSpec((1, 1, IDX_DIM), lambda b, pg: (b, 0, 0)),
            pl.BlockSpec(memory_space=pl.ANY),
        ],
        out_specs=pl.BlockSpec((1, topk, 1), lambda b, pg: (b, 0, 0)),
        scratch_shapes=[
            pltpu.VMEM((2, n_pages, PAGE_SIZE, IDX_DIM), F32),
            pltpu.SemaphoreType.DMA((2,)),
            pltpu.VMEM((rows, LANES), F32),
            pltpu.VMEM((rows, LANES), F32),
        ],
    )
    idx = pl.pallas_call(
        functools.partial(_sindex_kernel, n_pages=n_pages, topk=topk,
                          wscale=IDX_HEADS ** -0.5 * IDX_DIM ** -0.5, group=group),
        grid_spec=grid_spec,
        out_shape=jax.ShapeDtypeStruct((db, topk, 1), jnp.int32),
        compiler_params=_params("arbitrary"),
        name="sample_indices",
    )(pages, qi.reshape(db, IDX_HEADS, IDX_DIM), wi.reshape(db, IDX_HEADS, 1),
      kidx_new.reshape(db, 1, IDX_DIM), cache_kidx.reshape(-1, PAGE_SIZE, IDX_DIM))
    return idx.reshape(db, topk)


def _sgather_kernel(idx_ref, pg_ref, q_ref, knew_ref, vnew_ref, ck_ref, cv_ref, o_ref, kbuf, vbuf, sem,
                    *, past, topk):
    b = pl.program_id(0)
    nb = pl.num_programs(0)
    slot = lax.rem(b, 2)

    def start_all(bb, sl):
        def body(j, c):
            i = idx_ref[bb, j]
            ip = jnp.minimum(i, past - 1)
            page = pg_ref[bb, ip // PAGE_SIZE]
            off = lax.rem(ip, PAGE_SIZE)

            @pl.when(i >= past)
            def _():
                pltpu.make_async_copy(knew_ref.at[bb], kbuf.at[sl, j], sem.at[0, sl]).start()
                pltpu.make_async_copy(vnew_ref.at[bb], vbuf.at[sl, j], sem.at[1, sl]).start()

            @pl.when(i < past)
            def _():
                pltpu.make_async_copy(ck_ref.at[page, off], kbuf.at[sl, j], sem.at[0, sl]).start()
                pltpu.make_async_copy(cv_ref.at[page, off], vbuf.at[sl, j], sem.at[1, sl]).start()

            return c

        lax.fori_loop(0, topk, body, 0)

    @pl.when(b == 0)
    def _():
        start_all(b, slot)

    @pl.when(b + 1 < nb)
    def _():
        start_all(b + 1, 1 - slot)

    def wait_body(j, c):
        pltpu.make_async_copy(knew_ref.at[b], kbuf.at[slot, j], sem.at[0, slot]).wait()
        pltpu.make_async_copy(vnew_ref.at[b], vbuf.at[slot, j], sem.at[1, slot]).wait()
        return c

    lax.fori_loop(0, topk, wait_body, 0)

    k = kbuf[slot]
    v = vbuf[slot]
    s = jnp.sum(k * q_ref[...], axis=2, keepdims=True) * HEAD_DIM ** -0.5
    e = jnp.exp(s - jnp.max(s, axis=0, keepdims=True))
    p = e / jnp.sum(e, axis=0, keepdims=True)
    o_ref[...] = jnp.sum(p * v, axis=0, keepdims=True).astype(o_ref.dtype)


def _sample_attend(q, k_new, v_new, idx, cache_k, cache_v, pages):
    db, n_pages = pages.shape
    topk = idx.shape[1]
    tile = (N_HEADS, HEAD_DIM)
    grid_spec = pltpu.PrefetchScalarGridSpec(
        num_scalar_prefetch=2,
        grid=(db,),
        in_specs=[
            pl.BlockSpec((1,) + tile, lambda b, ix, pg: (b, 0, 0)),
            pl.BlockSpec(memory_space=pl.ANY),
            pl.BlockSpec(memory_space=pl.ANY),
            pl.BlockSpec(memory_space=pl.ANY),
            pl.BlockSpec(memory_space=pl.ANY),
        ],
        out_specs=pl.BlockSpec((1,) + tile, lambda b, ix, pg: (b, 0, 0)),
        scratch_shapes=[
            pltpu.VMEM((2, topk) + tile, F32),
            pltpu.VMEM((2, topk) + tile, F32),
            pltpu.SemaphoreType.DMA((2, 2)),
        ],
    )
    out = pl.pallas_call(
        functools.partial(_sgather_kernel, past=n_pages * PAGE_SIZE, topk=topk),
        grid_spec=grid_spec,
        out_shape=jax.ShapeDtypeStruct((db,) + tile, BF16),
        compiler_params=_params("arbitrary"),
        name="sample_attend",
    )(idx, pages, q.reshape((db,) + tile), k_new.reshape((db,) + tile), v_new.reshape((db,) + tile),
      cache_k.reshape((-1, PAGE_SIZE) + tile), cache_v.reshape((-1, PAGE_SIZE) + tile))
    return out.reshape(db, N_HEADS * HEAD_DIM)


def _merge_kernel(x_ref, g_ref, ya_ref, yb_ref, ym_ref, woa_ref, wob_ref, wom_ref, wout_ref,
                  gpm_ref, gpf_ref, x1_ref, h2_ref):
    d = x_ref.shape[1]
    merged = (g_ref[:, 0:d].astype(F32) * jnp.dot(ya_ref[...], woa_ref[...], preferred_element_type=F32)
              + g_ref[:, d:2 * d].astype(F32) * jnp.dot(yb_ref[...], wob_ref[...], preferred_element_type=F32)
              + g_ref[:, 2 * d:3 * d].astype(F32) * jnp.dot(ym_ref[...], wom_ref[...], preferred_element_type=F32))
    o = jnp.dot(merged.astype(BF16), wout_ref[...], preferred_element_type=F32)
    x1 = x_ref[...] + _rmsnorm_rows(o, gpm_ref[...])
    x1_ref[...] = x1
    h2_ref[...] = _rmsnorm_rows(x1, gpf_ref[...]).astype(h2_ref.dtype)


def _merge(x, gates, y_a, y_b, y_m, w_oa, w_ob, w_om, w_out, g_post_mix, g_pre_ffn):
    m, d = x.shape
    tm = _tile(m, 256)
    once = pl.Buffered(1)

    def rows(width):
        return pl.BlockSpec((tm, width), lambda i: (i, 0))

    def whole(a):
        return pl.BlockSpec(a.shape, lambda i: (0, 0), pipeline_mode=once)

    return pl.pallas_call(
        _merge_kernel,
        grid=(m // tm,),
        in_specs=[rows(d), rows(N_BRANCH * d), rows(y_a.shape[1]), rows(y_b.shape[1]), rows(y_m.shape[1]),
                  whole(w_oa), whole(w_ob), whole(w_om), whole(w_out),
                  pl.BlockSpec((1, d), lambda i: (0, 0)), pl.BlockSpec((1, d), lambda i: (0, 0))],
        out_specs=[rows(d), rows(d)],
        out_shape=[jax.ShapeDtypeStruct((m, d), F32), jax.ShapeDtypeStruct((m, d), BF16)],
        compiler_params=_params("parallel"),
        name="merge",
    )(x, gates, y_a, y_b, y_m, w_oa, w_ob, w_om, w_out, g_post_mix.reshape(1, d), g_pre_ffn.reshape(1, d))


def _ffn_up_kernel(h_ref, wg_ref, wu_ref, o_ref, *, rows):
    wg = wg_ref[...].astype(BF16)
    wu = wu_ref[...].astype(BF16)

    def body(r, carry):
        r0 = pl.multiple_of(r * rows, rows)
        h = h_ref[pl.ds(r0, rows), :]
        a = jnp.dot(h, wg, preferred_element_type=F32)
        c = jnp.dot(h, wu, preferred_element_type=F32)
        o_ref[pl.ds(r0, rows), :] = (jax.nn.silu(a) * c).astype(o_ref.dtype)
        return carry

    lax.fori_loop(0, h_ref.shape[0] // rows, body, 0)


def _ffn_up(h2, w_gu):
    m, d = h2.shape
    d_ff = w_gu.shape[1] // 2
    tm = _tile(m, FFN_UP_TM)
    tf = _tile(d_ff, 2 * LANES)
    nf = d_ff // tf
    return pl.pallas_call(
        functools.partial(_ffn_up_kernel, rows=_tile(tm, 2048)),
        grid=(m // tm, nf),
        in_specs=[
            pl.BlockSpec((tm, d), lambda i, j: (i, 0)),
            pl.BlockSpec((d, tf), lambda i, j: (0, j)),
            pl.BlockSpec((d, tf), lambda i, j: (0, j + nf)),
        ],
        out_specs=pl.BlockSpec((tm, tf), lambda i, j: (i, j)),
        out_shape=jax.ShapeDtypeStruct((m, d_ff), BF16),
        compiler_params=_params("parallel", "parallel"),
        name="ffn_up",
    )(h2, w_gu, w_gu)


def _ffn_down_kernel(a_ref, wd_ref, x1_ref, gpf_ref, o_ref):
    y = jnp.dot(a_ref[...], wd_ref[...], preferred_element_type=F32)
    o_ref[...] = x1_ref[...] + _rmsnorm_rows(y, gpf_ref[...])


def _ffn_down(act, x1, w_down, g_post_ffn):
    m, d_ff = act.shape
    d = w_down.shape[1]
    tm = _tile(m, 256)
    return pl.pallas_call(
        _ffn_down_kernel,
        grid=(m // tm,),
        in_specs=[
            pl.BlockSpec((tm, d_ff), lambda i: (i, 0)),
            pl.BlockSpec((d_ff, d), lambda i: (0, 0), pipeline_mode=pl.Buffered(1)),
            pl.BlockSpec((tm, d), lambda i: (i, 0)),
            pl.BlockSpec((1, d), lambda i: (0, 0)),
        ],
        out_specs=pl.BlockSpec((tm, d), lambda i: (i, 0)),
        out_shape=jax.ShapeDtypeStruct((m, d), F32),
        compiler_params=_params("parallel"),
        name="ffn_down",
    )(act, w_down, x1, g_post_ffn.reshape(1, d))


def _project_common(x2d, g_pre_mix, w_t, ln_g, ln_b, uv_dtype, q_dtype, q_scale, v_heads_t):
    d = x2d.shape[1]
    aw, iw = N_HEADS * HEAD_DIM, IDX_HEADS * IDX_DIM
    h = _rmsnorm(x2d, g_pre_mix, BF16)
    uv = _project_uv(h, w_t, ln_g, ln_b, uv_dtype)
    (q,) = _matmul(h, w_t, (q_dtype,), scale=q_scale, rows=(d, aw), name="proj_q")
    k, k_bf = _matmul(h, w_t, (F32, BF16), rows=(d + aw, aw), name="proj_k")
    v_out = _matmul(h, w_t, (F32,), rows=(d + 2 * aw, aw), heads_t=v_heads_t, name="proj_v")
    v, v_t = v_out if v_heads_t is not None else (v_out[0], None)
    (qi,) = _matmul(h, w_t, (BF16,), rows=(d + 3 * aw, iw), name="proj_qi")
    (kw,) = _matmul(h, w_t, (F32,), rows=(d + 3 * aw + iw, 2 * LANES), name="proj_kidx")
    o_qm = d + 3 * aw + iw + IDX_DIM + IDX_HEADS
    o_g = o_qm + MEM_HEADS * MEM_HEAD_DIM
    (qm,) = _matmul(h, w_t, (BF16,), rows=(o_qm, o_g - o_qm), name="proj_qm")
    (gates,) = _matmul(h, w_t, (BF16,), act="sigmoid", rows=(o_g, N_BRANCH * d), name="proj_gates")
    kidx = kw[:, :IDX_DIM]
    wi = kw[:, IDX_DIM:IDX_DIM + IDX_HEADS]
    return uv, q, k, k_bf, v, v_t, qi, kidx, wi, qm, gates


def kernel(x_prompt, x_sample, mem_prompt, cache_k, cache_v, cache_kidx, cache_mem_k, cache_mem_v, page_table, g_pre_mix, g_post_mix, g_pre_ffn, g_post_ffn, g_mem, ln_v_g, ln_v_b, w_in, w_s, b_s, w_oa, w_ob, w_om, w_out, w_mem_kv, w_gu, w_down):
    depth = w_in.shape[0]
    b, t, d = x_prompt.shape
    db, dt, _ = x_sample.shape
    assert dt == 1, "sample path handles one new token per sequence"
    n_mem = mem_prompt.shape[1]
    aw = N_HEADS * HEAD_DIM
    mw = MEM_HEADS * MEM_HEAD_DIM
    gw = d // 2
    past = page_table.shape[1] * PAGE_SIZE

    yp = x_prompt.reshape(b * t, d)
    ys = x_sample.reshape(db * dt, d)
    outs = [[] for _ in range(9)]
    for l in range(depth):
        ws = w_in[l].T
        woa, wob, wom, wout = (w.astype(BF16) for w in (w_oa[l], w_ob[l], w_om[l], w_out[l]))
        wdown, wmem = w_down[l].astype(BF16), w_mem_kv[l].T.astype(BF16)

        uv, q, k, k_bf, v, v_t, qi, kidx, wi, qm, gates = _project_common(
            yp, g_pre_mix[l], ws, ln_v_g[l], ln_v_b[l], BF16, BF16, DSA_LOG2_SCALE, (b, BF16))
        y_a = _gmlp_spatial(uv, w_s[l], b_s[l])
        y_b = _dsa_prompt(
            q.reshape(b, t, aw), qi.reshape(b, t, -1), jnp.swapaxes(wi.reshape(b, t, IDX_HEADS), 1, 2),
            k_bf.reshape(b, t, aw), kidx.astype(BF16).reshape(b, t, IDX_DIM), v_t).reshape(b * t, aw)
        hm = _rmsnorm(mem_prompt.reshape(b * n_mem, d), g_mem[l], BF16)
        (mkv,) = _matmul(hm, wmem, (F32,), name="proj_mem_kv")
        mk = mkv[:, :mw].reshape(b, n_mem, mw)
        mv = mkv[:, mw:].reshape(b, n_mem, mw)
        y_m = _mem_attend(qm.reshape(b, t, mw), mk, mv).reshape(b * t, mw)
        x1, h2 = _merge(yp, gates, y_a, y_b, y_m, woa, wob, wom, wout, g_post_mix[l], g_pre_ffn[l])
        yp = _ffn_down(_ffn_up(h2, w_gu[l]), x1, wdown, g_post_ffn[l])
        outs[0].append(k.reshape(b, t, N_HEADS, HEAD_DIM))
        outs[1].append(v.reshape(b, t, N_HEADS, HEAD_DIM))
        outs[2].append(kidx.reshape(b, t, IDX_DIM))
        outs[3].append(mk.reshape(b, n_mem, MEM_HEADS, MEM_HEAD_DIM))
        outs[4].append(mv.reshape(b, n_mem, MEM_HEADS, MEM_HEAD_DIM))

        uv, q, k, _, v, _, qi, kidx, wi, qm, gates = _project_common(
            ys, g_pre_mix[l], ws, ln_v_g[l], ln_v_b[l], F32, F32, None, None)
        y_a = _gmlp_first_row(uv, w_s[l], b_s[l])
        pages = page_table + l * cache_k.shape[1]
        idx = _sample_indices(qi, wi, kidx, cache_kidx, pages)
        y_b = _sample_attend(q, k, v, idx, cache_k, cache_v, pages)
        qm_pad = jnp.pad(qm.reshape(db, 1, mw), ((0, 0), (0, 2 * SUBLANES - 1), (0, 0)))
        y_m = _mem_attend(qm_pad, cache_mem_k[l].reshape(db, n_mem, mw),
                          cache_mem_v[l].reshape(db, n_mem, mw))[:, 0, :]
        x1, h2 = _merge(ys, gates, y_a, y_b, y_m, woa, wob, wom, wout, g_post_mix[l], g_pre_ffn[l])
        ys = _ffn_down(_ffn_up(h2, w_gu[l]), x1, wdown, g_post_ffn[l])
        outs[5].append(k.reshape(db, dt, N_HEADS, HEAD_DIM))
        outs[6].append(v.reshape(db, dt, N_HEADS, HEAD_DIM))
        outs[7].append(kidx.reshape(db, dt, IDX_DIM))
        outs[8].append(uv[:, gw:].reshape(db, dt, gw))

    return (yp.reshape(b, t, d), ys.reshape(db, dt, d)) + tuple(jnp.stack(o) for o in outs)
```

```python
import functools

import jax
import jax.numpy as jnp
from jax import lax
from jax.experimental import pallas as pl
from jax.experimental.pallas import tpu as pltpu

EPS = 1e-6
CHUNK = 128
GM_GROUPS = 8
N_HEADS = 8
HEAD_DIM = 128
IDX_HEADS = 16
IDX_DIM = 128
TOPK_MAX = 256
MEM_HEADS = 4
MEM_HEAD_DIM = 128
N_BRANCH = 3
PAGE_SIZE = 128

LANES = 128
SUBLANES = 8
VMEM_LIMIT_BYTES = 56 * 2**20
INT_MIN = -2**31
KEY_LOWEST = INT_MIN + 0x800000
KEY_INF = 0x7F800000
NEG = -1e30
LOG2_E = 1.4426950408889634
DSA_LOG2_SCALE = HEAD_DIM ** -0.5 * LOG2_E
ONES_ROWS = 16
COUNT_ROWS = 64
BF16_ROWS = 16
COARSE_COUNT_ROWS = 128
MM_TM, MM_TN = 2048, 1024
FFN_UP_TM = 4096

F32 = jnp.float32
BF16 = jnp.bfloat16
NT_DIMS = (((1,), (1,)), ((), ()))


def _params(*sem):
    return pltpu.CompilerParams(dimension_semantics=sem, vmem_limit_bytes=VMEM_LIMIT_BYTES)


def _tile(n, pref):
    if n <= pref:
        return n
    t = pref
    while n % t:
        t //= 2
    return t


def _key_to_float(key):
    key = jnp.clip(key, KEY_LOWEST, KEY_INF)
    return pltpu.bitcast(key ^ ((key >> 31) & 0x7FFFFFFF), F32)


def _key16_to_float(key):
    key = jnp.clip(key, KEY_LOWEST >> 16, KEY_INF >> 16)
    return pltpu.bitcast((key ^ ((key >> 31) & 0x7FFF)) << 16, F32)


def _key16_to_key(key):
    key = jnp.clip(key, -2**15, 2**15 - 1)
    return (key << 16) + ((key >> 31) & 0xFFFF)


def _build_key(count_ge, topk, like, bits):
    base = jnp.where(count_ge(jnp.zeros_like(like)) >= topk, 0, -(1 << (bits - 1))).astype(jnp.int32)

    def body(b, t):
        cand = t | jnp.left_shift(jnp.int32(1), bits - 2 - b)
        return jnp.where(count_ge(cand) >= topk, cand, t)

    return lax.fori_loop(0, bits - 1, body, base)


def _kth_largest_key(count_ge, topk, like, count_ge_bf16=None):
    if count_ge_bf16 is None:
        return _build_key(count_ge, topk, like, 32)
    k16 = _build_key(count_ge_bf16, topk, like, 16)
    lo, hi = _key16_to_key(k16 - 1), _key16_to_key(k16 + 1)

    def bisect(_, lohi):
        lo, hi = lohi
        mid = lo + ((hi - lo) >> 1)
        ok = count_ge(mid) >= topk
        return jnp.where(ok, mid, lo), jnp.where(ok, hi, mid)

    lo, _ = lax.fori_loop(0, 17, bisect, (lo, hi))
    return lo


def _rmsnorm_rows(x, g):
    return x * lax.rsqrt(jnp.mean(x * x, axis=-1, keepdims=True) + EPS) * g


def _rmsnorm_kernel(x_ref, g_ref, o_ref):
    o_ref[...] = _rmsnorm_rows(x_ref[...], g_ref[...]).astype(o_ref.dtype)


def _rmsnorm(x, g, out_dtype):
    m, d = x.shape
    tm = _tile(m, 512)
    return pl.pallas_call(
        _rmsnorm_kernel,
        grid=(m // tm,),
        in_specs=[pl.BlockSpec((tm, d), lambda i: (i, 0)), pl.BlockSpec((1, d), lambda i: (0, 0))],
        out_specs=pl.BlockSpec((tm, d), lambda i: (i, 0)),
        out_shape=jax.ShapeDtypeStruct((m, d), out_dtype),
        compiler_params=_params("parallel"),
        name="rmsnorm",
    )(x, g.reshape(1, d))


def _mm_kernel(x_ref, w_ref, *o_refs, act, scale, heads_t):
    acc = lax.dot_general(x_ref[...], w_ref[...].astype(BF16), NT_DIMS, preferred_element_type=F32)
    if act == "sigmoid":
        acc = jax.nn.sigmoid(acc)
    if scale is not None:
        acc = acc * scale
    if heads_t:
        o_refs, t_ref = o_refs[:-1], o_refs[-1]
        rows = HEAD_DIM + ONES_ROWS
        for h in range(acc.shape[1] // HEAD_DIM):
            t_ref[0, h * rows:h * rows + HEAD_DIM, :] = acc[:, h * HEAD_DIM:(h + 1) * HEAD_DIM].T.astype(t_ref.dtype)
            t_ref[0, h * rows + HEAD_DIM:(h + 1) * rows, :] = jnp.ones((ONES_ROWS, acc.shape[0]), t_ref.dtype)
    for o_ref in o_refs:
        o_ref[...] = acc.astype(o_ref.dtype)


def _matmul(x, w_t, out_dtypes, act=None, scale=None, rows=None, heads_t=None, name="matmul"):
    m, k = x.shape
    row0, n = rows if rows is not None else (0, w_t.shape[0])
    tm = _tile(m if heads_t is None else m // heads_t[0], MM_TM)
    tn = _tile(n, MM_TN if w_t.dtype == BF16 else MM_TN // 2)
    assert row0 % SUBLANES == 0
    out_specs = [pl.BlockSpec((tm, tn), lambda i, j: (i, j)) for _ in out_dtypes]
    out_shapes = [jax.ShapeDtypeStruct((m, n), dt) for dt in out_dtypes]
    if heads_t is not None:
        batch, dt = heads_t
        per_batch = m // batch // tm
        t_rows = tn // HEAD_DIM * (HEAD_DIM + ONES_ROWS)
        out_specs.append(pl.BlockSpec((1, t_rows, tm), lambda i, j: (i // per_batch, j, i % per_batch)))
        out_shapes.append(jax.ShapeDtypeStruct((batch, n // tn * t_rows, m // batch), dt))
    outs = pl.pallas_call(
        functools.partial(_mm_kernel, act=act, scale=scale, heads_t=heads_t is not None),
        grid=(m // tm, n // tn),
        in_specs=[pl.BlockSpec((tm, k), lambda i, j: (i, 0)),
                  pl.BlockSpec((pl.Element(tn), pl.Element(k)),
                               lambda i, j: (pl.multiple_of(row0 + j * tn, SUBLANES), 0))],
        out_specs=out_specs,
        out_shape=out_shapes,
        compiler_params=_params("parallel", "parallel"),
        name=name,
    )(x, w_t)
    return outs


def _uv_kernel(x_ref, w_ref, lng_ref, lnb_ref, o_ref):
    y = jax.nn.gelu(lax.dot_general(x_ref[...], w_ref[...].astype(BF16), NT_DIMS,
                                    preferred_element_type=F32))

    @pl.when(pl.program_id(1) == 0)
    def _():
        o_ref[...] = y.astype(o_ref.dtype)

    @pl.when(pl.program_id(1) == 1)
    def _():
        yc = y - jnp.mean(y, axis=-1, keepdims=True)
        yn = yc * lax.rsqrt(jnp.mean(yc * yc, axis=-1, keepdims=True) + EPS)
        o_ref[...] = (yn * lng_ref[...] + lnb_ref[...]).astype(o_ref.dtype)


def _project_uv(h, w_t, ln_g, ln_b, out_dtype):
    m, k = h.shape
    gw = ln_g.shape[0]
    tm = _tile(m, 1024)
    return pl.pallas_call(
        _uv_kernel,
        grid=(m // tm, 2),
        in_specs=[
            pl.BlockSpec((tm, k), lambda i, j: (i, 0)),
            pl.BlockSpec((gw, k), lambda i, j: (j, 0)),
            pl.BlockSpec((1, gw), lambda i, j: (0, 0)),
            pl.BlockSpec((1, gw), lambda i, j: (0, 0)),
        ],
        out_specs=pl.BlockSpec((tm, gw), lambda i, j: (i, j)),
        out_shape=jax.ShapeDtypeStruct((m, 2 * gw), out_dtype),
        compiler_params=_params("parallel", "arbitrary"),
        name="project_uv",
    )(h, w_t, ln_g.reshape(1, gw), ln_b.reshape(1, gw))


def _gmlp_kernel(u_ref, v_ref, ws_ref, bt_ref, o_ref, *, n_chunks):
    c = CHUNK
    gd = u_ref.shape[1] // GM_GROUPS
    causal = lax.broadcasted_iota(jnp.int32, (c, c), 1) <= lax.broadcasted_iota(jnp.int32, (c, c), 0)
    for g in range(GM_GROUPS):
        w = jnp.where(causal, ws_ref[g], 0.0).astype(v_ref.dtype)
        bcol = bt_ref[:, g:g + 1]
        for ci in range(n_chunks):
            rows = slice(ci * c, (ci + 1) * c)
            cols = slice(g * gd, (g + 1) * gd)
            mixed = jnp.dot(w, v_ref[rows, cols], preferred_element_type=F32) + bcol
            o_ref[rows, cols] = (u_ref[rows, cols].astype(F32) * mixed).astype(o_ref.dtype)


def _gmlp_spatial(uv, w_s, b_s):
    m = uv.shape[0]
    gw = uv.shape[1] // 2
    tm = _tile(m, 4 * CHUNK)
    return pl.pallas_call(
        functools.partial(_gmlp_kernel, n_chunks=tm // CHUNK),
        grid=(m // tm,),
        in_specs=[
            pl.BlockSpec((tm, gw), lambda i: (i, 0)),
            pl.BlockSpec((tm, gw), lambda i: (i, 1)),
            pl.BlockSpec((GM_GROUPS, CHUNK, CHUNK), lambda i: (0, 0, 0)),
            pl.BlockSpec((CHUNK, GM_GROUPS), lambda i: (0, 0)),
        ],
        out_specs=pl.BlockSpec((tm, gw), lambda i: (i, 0)),
        out_shape=jax.ShapeDtypeStruct((m, gw), BF16),
        compiler_params=_params("parallel"),
        name="gmlp_spatial",
    )(uv, uv, w_s, b_s.T)


def _gmlp_first_kernel(u_ref, v_ref, w_ref, b_ref, o_ref):
    o_ref[...] = (u_ref[...] * (w_ref[...] * v_ref[...] + b_ref[...])).astype(o_ref.dtype)


def _gmlp_first_row(uv, w_s, b_s):
    m = uv.shape[0]
    gw = uv.shape[1] // 2
    gd = gw // GM_GROUPS
    w_row = jnp.repeat(w_s[:, 0, 0], gd).reshape(1, gw)
    b_row = jnp.repeat(b_s[:, 0], gd).reshape(1, gw)
    return pl.pallas_call(
        _gmlp_first_kernel,
        grid=(1,),
        in_specs=[
            pl.BlockSpec((m, gw), lambda i: (0, 0)),
            pl.BlockSpec((m, gw), lambda i: (0, 1)),
            pl.BlockSpec((1, gw), lambda i: (0, 0)),
            pl.BlockSpec((1, gw), lambda i: (0, 0)),
        ],
        out_specs=pl.BlockSpec((m, gw), lambda i: (0, 0)),
        out_shape=jax.ShapeDtypeStruct((m, gw), BF16),
        compiler_params=_params("arbitrary"),
        name="gmlp_first_row",
    )(uv, uv, w_row, b_row)


def _dsa_kernel(q_ref, qi_ref, wit_ref, qin_ref, witn_ref, k_ref, kidx_ref, vt_ref, o_ref,
                sc_ref, sc16_ref, m_ref, acc_ref, s_ref, p_ref, *, tq, tk, topk, wscale):
    i = pl.program_id(1)
    last = pl.num_programs(1) - 1
    cur = lax.rem(i, 2)
    n_chunks = ((i + 1) * tq + tk - 1) // tk
    n_next = ((i + 2) * tq + tk - 1) // tk
    vrows = vt_ref.shape[1] // N_HEADS

    def idx_chunk(c, qi_r, wit_r, blk, slot):
        k0 = pl.multiple_of(c * tk, tk)
        kc = kidx_ref[0, pl.ds(k0, tk), :]
        acc = jnp.zeros((tk, tq), F32)
        for h in range(IDX_HEADS):
            s = lax.dot_general(kc, qi_r[0, :, h * IDX_DIM:(h + 1) * IDX_DIM], NT_DIMS,
                                preferred_element_type=F32)
            acc = acc + jnp.maximum(s, 0.0) * (wit_r[0, h:h + 1, :] * wscale)
        kpos = k0 + lax.broadcasted_iota(jnp.int32, (tk, 1), 0)
        qpos = blk * tq + lax.broadcasted_iota(jnp.int32, (1, tq), 1)
        sc = jnp.where(kpos <= qpos, acc, -jnp.inf)
        sc_ref[slot, pl.ds(k0, tk), :] = sc
        sc16_ref[slot, pl.ds(k0, tk), :] = sc.astype(BF16)

    def pad_odd(n, slot):
        @pl.when(n % 2 == 1)
        def _():
            rows = pl.ds(pl.multiple_of(n * tk, tk), tk)
            sc_ref[slot, rows, :] = jnp.full((tk, tq), -jnp.inf, F32)
            sc16_ref[slot, rows, :] = jnp.full((tk, tq), -jnp.inf, BF16)

    @pl.when(i == 0)
    def _():
        def body(c, carry):
            idx_chunk(c, qi_ref, wit_ref, i, cur)
            return carry

        lax.fori_loop(0, n_chunks, body, 0)
        pad_odd(n_chunks, cur)

    def count_ge(cand):
        cand_f = _key_to_float(cand)

        def body(c, cnt):
            k0 = pl.multiple_of(c * (2 * tk), 2 * tk)
            hit = jnp.where(sc_ref[cur, pl.ds(k0, 2 * tk), :] >= cand_f, 1.0, 0.0)
            return cnt + jnp.sum(hit.reshape(2 * tk // COUNT_ROWS, COUNT_ROWS, tq), axis=0)

        cnt = lax.fori_loop(0, (n_chunks + 1) // 2, body, jnp.zeros((COUNT_ROWS, tq), F32))
        return jnp.sum(cnt, axis=0, keepdims=True)

    def count_ge_bf16(cand16):
        cand_c = jnp.broadcast_to(_key16_to_float(cand16), (BF16_ROWS, tq)).astype(BF16)
        one, zero = jnp.ones((), BF16), jnp.zeros((), BF16)
        parts = 2 * tk // COARSE_COUNT_ROWS

        def body(c, cnt):
            k0 = pl.multiple_of(c * (2 * tk), 2 * tk)
            x = sc16_ref[cur, pl.ds(k0, 2 * tk), :].reshape(2 * tk // BF16_ROWS, BF16_ROWS, tq)
            hit = jnp.where(x >= cand_c[None], one, zero).reshape(parts, COARSE_COUNT_ROWS, tq)
            for part in range(parts):
                cnt = cnt + hit[part]
            return cnt

        cnt = lax.fori_loop(0, (n_chunks + 1) // 2, body, jnp.zeros((COARSE_COUNT_ROWS, tq), BF16))
        return jnp.sum(cnt.astype(F32), axis=0, keepdims=True)

    thr = _key_to_float(_kth_largest_key(count_ge, topk, jnp.zeros((1, tq), jnp.int32), count_ge_bf16))

    m_ref[...] = jnp.full(m_ref.shape, NEG, F32)
    acc_ref[...] = jnp.zeros(acc_ref.shape, F32)

    def att_chunk(c):
        k0 = pl.multiple_of(c * tk, tk)
        bias = jnp.where(sc_ref[cur, pl.ds(k0, tk), :] >= thr, 0.0, NEG)
        m_old, m_new = [], []
        for h in range(N_HEADS):
            hs = slice(h * HEAD_DIM, (h + 1) * HEAD_DIM)
            s = lax.dot_general(k_ref[0, pl.ds(k0, tk), hs], q_ref[0, :, hs], NT_DIMS,
                                preferred_element_type=F32) + bias
            s_ref[h] = s
            m_old.append(m_ref[h])
            m_new.append(jnp.maximum(m_old[h], jnp.max(s, axis=0, keepdims=True)))
            m_ref[h] = m_new[h]
        for h in range(N_HEADS):
            p_ref[h] = jnp.exp2(s_ref[h] - m_new[h][0:1, :]).astype(BF16)
        for h in range(N_HEADS):
            pv = jnp.dot(vt_ref[0, h * vrows:(h + 1) * vrows, pl.ds(k0, tk)], p_ref[h],
                         preferred_element_type=F32)
            acc_ref[h] = jnp.exp2(m_old[h] - m_new[h])[0:1, :] * acc_ref[h] + pv

    @pl.when(i < last)
    def _():
        def fused(c, carry):
            att_chunk(c)
            idx_chunk(c, qin_ref, witn_ref, i + 1, 1 - cur)
            return carry

        def tail(c, carry):
            idx_chunk(c, qin_ref, witn_ref, i + 1, 1 - cur)
            return carry

        lax.fori_loop(0, n_chunks, fused, 0)
        lax.fori_loop(n_chunks, n_next, tail, 0)
        pad_odd(n_next, 1 - cur)

    @pl.when(i == last)
    def _():
        def body(c, carry):
            att_chunk(c)
            return carry

        lax.fori_loop(0, n_chunks, body, 0)

    for h in range(N_HEADS):
        o = acc_ref[h, 0:HEAD_DIM, :] / acc_ref[h, HEAD_DIM:HEAD_DIM + 1, :]
        o_ref[0, :, h * HEAD_DIM:(h + 1) * HEAD_DIM] = o.T.astype(o_ref.dtype)


def _dsa_prompt(q, qi, wi_t, k, kidx, v_t, tq=256, tk=256):
    b, t, aw = q.shape
    tq = _tile(t, tq)
    tk = _tile(tq, tk)
    topk = min(TOPK_MAX, t // 4)
    once = pl.Buffered(1)
    nq = t // tq

    def next_block(i):
        return jnp.minimum(i + 1, nq - 1)

    return pl.pallas_call(
        functools.partial(_dsa_kernel, tq=tq, tk=tk, topk=topk,
                          wscale=IDX_HEADS ** -0.5 * IDX_DIM ** -0.5),
        grid=(b, nq),
        in_specs=[
            pl.BlockSpec((1, tq, aw), lambda bi, i: (bi, i, 0)),
            pl.BlockSpec((1, tq, IDX_HEADS * IDX_DIM), lambda bi, i: (bi, i, 0)),
            pl.BlockSpec((1, IDX_HEADS, tq), lambda bi, i: (bi, 0, i)),
            pl.BlockSpec((1, tq, IDX_HEADS * IDX_DIM), lambda bi, i: (bi, next_block(i), 0)),
            pl.BlockSpec((1, IDX_HEADS, tq), lambda bi, i: (bi, 0, next_block(i))),
            pl.BlockSpec((1, t, aw), lambda bi, i: (bi, 0, 0), pipeline_mode=once),
            pl.BlockSpec((1, t, IDX_DIM), lambda bi, i: (bi, 0, 0), pipeline_mode=once),
            pl.BlockSpec((1, v_t.shape[1], t), lambda bi, i: (bi, 0, 0), pipeline_mode=once),
        ],
        out_specs=pl.BlockSpec((1, tq, aw), lambda bi, i: (bi, i, 0)),
        out_shape=jax.ShapeDtypeStruct((b, t, aw), BF16),
        scratch_shapes=[
            pltpu.VMEM((2, t + tk, tq), F32),
            pltpu.VMEM((2, t + tk, tq), BF16),
            pltpu.VMEM((N_HEADS, SUBLANES, tq), F32),
            pltpu.VMEM((N_HEADS, HEAD_DIM + ONES_ROWS, tq), F32),
            pltpu.VMEM((N_HEADS, tk, tq), F32),
            pltpu.VMEM((N_HEADS, tk, tq), BF16),
        ],
        compiler_params=_params("parallel", "arbitrary"),
        name="dsa_prompt",
    )(q, qi, wi_t, qi, wi_t, k, kidx, v_t)


def _memattn_kernel(qm_ref, mk_ref, mv_ref, o_ref):
    scale = MEM_HEAD_DIM ** -0.5
    for h in range(MEM_HEADS):
        hs = slice(h * MEM_HEAD_DIM, (h + 1) * MEM_HEAD_DIM)
        s = lax.dot_general(qm_ref[0, :, hs], mk_ref[0, :, hs].astype(BF16), NT_DIMS,
                            preferred_element_type=F32) * scale
        e = jnp.exp(s - jnp.max(s, axis=-1, keepdims=True))
        p = e / jnp.sum(e, axis=-1, keepdims=True)
        o = jnp.dot(p.astype(BF16), mv_ref[0, :, hs].astype(BF16), preferred_element_type=F32)
        o_ref[0, :, hs] = o.astype(o_ref.dtype)


def _mem_attend(qm, mk, mv):
    b, t, mw = qm.shape
    n_mem = mk.shape[1]
    tm = _tile(t, 512)
    return pl.pallas_call(
        _memattn_kernel,
        grid=(b, t // tm),
        in_specs=[
            pl.BlockSpec((1, tm, mw), lambda bi, i: (bi, i, 0)),
            pl.BlockSpec((1, n_mem, mw), lambda bi, i: (bi, 0, 0)),
            pl.BlockSpec((1, n_mem, mw), lambda bi, i: (bi, 0, 0)),
        ],
        out_specs=pl.BlockSpec((1, tm, mw), lambda bi, i: (bi, i, 0)),
        out_shape=jax.ShapeDtypeStruct((b, t, mw), BF16),
        compiler_params=_params("parallel", "parallel"),
        name="mem_attend",
    )(qm, mk, mv)


def _strict_triangle(n, lower):
    r = lax.broadcasted_iota(jnp.int32, (n, n), 0)
    c = lax.broadcasted_iota(jnp.int32, (n, n), 1)
    return jnp.where((c < r) if lower else (r < c), 1.0, 0.0).astype(BF16)


def _index_order_rank(flag):
    rows = flag.shape[0]
    within = jnp.dot(flag.astype(BF16), _strict_triangle(LANES, False), preferred_element_type=F32)
    per_row = jnp.broadcast_to(jnp.sum(flag, axis=1, keepdims=True), flag.shape)
    before = jnp.dot(_strict_triangle(rows, True), per_row.astype(BF16), preferred_element_type=F32)
    return before + within


def _sindex_kernel(pg_ref, qi_ref, wi_ref, knew_ref, cache_ref, idx_ref, kbuf, sem, sc_ref, pos_ref,
                   *, n_pages, topk, wscale, group):
    b = pl.program_id(0)
    nb = pl.num_programs(0)
    slot = lax.rem(b, 2)

    def page_copy(bb, p, sl):
        return pltpu.make_async_copy(cache_ref.at[pg_ref[bb, p]], kbuf.at[sl, p], sem.at[sl])

    def start_all(bb, sl):
        lax.fori_loop(0, n_pages, lambda p, c: (page_copy(bb, p, sl).start(), c)[1], 0)

    @pl.when(b == 0)
    def _():
        start_all(b, slot)

    @pl.when(b + 1 < nb)
    def _():
        start_all(b + 1, 1 - slot)

    lax.fori_loop(0, n_pages, lambda p, c: (page_copy(b, p, slot).wait(), c)[1], 0)

    qi = qi_ref[0]
    w = wi_ref[0] * wscale
    for g in range(n_pages // group):
        kc = kbuf[slot, g * group:(g + 1) * group].reshape(group * PAGE_SIZE, IDX_DIM).astype(BF16)
        s = lax.dot_general(qi, kc, NT_DIMS, preferred_element_type=F32)
        r = jnp.sum(jnp.maximum(s, 0.0) * w, axis=0, keepdims=True)
        for j in range(group):
            sc_ref[g * group + j:g * group + j + 1, :] = r[:, j * PAGE_SIZE:(j + 1) * PAGE_SIZE]
    s_new = jnp.sum(qi.astype(F32) * knew_ref[0].astype(BF16).astype(F32), axis=1, keepdims=True)
    s_new = jnp.sum(jnp.maximum(s_new, 0.0) * w, axis=0, keepdims=True)
    pad_rows = sc_ref.shape[0] - n_pages
    first = (lax.broadcasted_iota(jnp.int32, (pad_rows, LANES), 0) == 0) & (
        lax.broadcasted_iota(jnp.int32, (pad_rows, LANES), 1) == 0)
    sc_ref[n_pages:, :] = jnp.where(first, s_new, -jnp.inf)

    s = sc_ref[...]

    def count(hit):
        return jnp.sum(jnp.sum(jnp.where(hit, 1.0, 0.0), axis=1, keepdims=True), axis=0, keepdims=True)

    thr = _key_to_float(_kth_largest_key(lambda cand: count(s >= _key_to_float(cand)), topk,
                                         jnp.zeros((1, 1), jnp.int32)))
    above = s > thr
    tied = jnp.where(s == thr, 1.0, 0.0)
    keep_tied = (tied > 0.5) & (_index_order_rank(tied) < topk - count(above))
    chosen = jnp.where(above | keep_tied, 1.0, 0.0)
    pos_ref[...] = jnp.where(chosen > 0.5, _index_order_rank(chosen), -1.0)

    slot_id = lax.broadcasted_iota(jnp.int32, (topk, LANES), 0).astype(F32)
    lane = lax.broadcasted_iota(jnp.int32, (1, LANES), 1)

    def place(p, acc):
        key_pos = (p * PAGE_SIZE + lane).astype(F32)
        return acc + jnp.where(pos_ref[pl.ds(p, 1), :] == slot_id, key_pos, 0.0)

    acc = lax.fori_loop(0, n_pages + 1, place, jnp.zeros((topk, LANES), F32))
    idx_ref[0] = jnp.sum(acc, axis=1, keepdims=True).astype(jnp.int32)


def _sample_indices(qi, wi, kidx_new, cache_kidx, pages):
    db, n_pages = pages.shape
    past = n_pages * PAGE_SIZE
    topk = min(TOPK_MAX, (past + 1) // 4)
    assert topk % SUBLANES == 0
    group = 16 if n_pages % 16 == 0 else 1
    rows = (n_pages + 1 + SUBLANES - 1) // SUBLANES * SUBLANES
    grid_spec = pltpu.PrefetchScalarGridSpec(
        num_scalar_prefetch=1,
        grid=(db,),
        in_specs=[
            pl.BlockSpec((1, IDX_HEADS, IDX_DIM), lambda b, pg: (b, 0, 0)),
            pl.BlockSpec((1, IDX_HEADS, 1), lambda b, pg: (b, 0, 0)),
            pl.BlockSpec((1, 1, IDX_DIM), lambda b, pg: (b, 0, 0)),
            pl.BlockSpec(memory_space=pl.ANY),
        ],
        out_specs=pl.BlockSpec((1, topk, 1), lambda b, pg: (b, 0, 0)),
        scratch_shapes=[
            pltpu.VMEM((2, n_pages, PAGE_SIZE, IDX_DIM), F32),
            pltpu.SemaphoreType.DMA((2,)),
            pltpu.VMEM((rows, LANES), F32),
            pltpu.VMEM((rows, LANES), F32),
        ],
    )
    idx = pl.pallas_call(
        functools.partial(_sindex_kernel, n_pages=n_pages, topk=topk,
                          wscale=IDX_HEADS ** -0.5 * IDX_DIM ** -0.5, group=group),
        grid_spec=grid_spec,
        out_shape=jax.ShapeDtypeStruct((db, topk, 1), jnp.int32),
        compiler_params=_params("arbitrary"),
        name="sample_indices",
    )(pages, qi.reshape(db, IDX_HEADS, IDX_DIM), wi.reshape(db, IDX_HEADS, 1),
      kidx_new.reshape(db, 1, IDX_DIM), cache_kidx.reshape(-1, PAGE_SIZE, IDX_DIM))
    return idx.reshape(db, topk)


def _sgather_kernel(idx_ref, pg_ref, q_ref, knew_ref, vnew_ref, ck_ref, cv_ref, o_ref, kbuf, vbuf, sem,
                    *, past, topk):
    b = pl.program_id(0)
    nb = pl.num_programs(0)
    slot = lax.rem(b, 2)

    def start_all(bb, sl):
        def body(j, c):
            i = idx_ref[bb, j]
            ip = jnp.minimum(i, past - 1)
            page = pg_ref[bb, ip // PAGE_SIZE]
            off = lax.rem(ip, PAGE_SIZE)

            @pl.when(i >= past)
            def _():
                pltpu.make_async_copy(knew_ref.at[bb], kbuf.at[sl, j], sem.at[0, sl]).start()
                pltpu.make_async_copy(vnew_ref.at[bb], vbuf.at[sl, j], sem.at[1, sl]).start()

            @pl.when(i < past)
            def _():
                pltpu.make_async_copy(ck_ref.at[page, off], kbuf.at[sl, j], sem.at[0, sl]).start()
                pltpu.make_async_copy(cv_ref.at[page, off], vbuf.at[sl, j], sem.at[1, sl]).start()

            return c

        lax.fori_loop(0, topk, body, 0)

    @pl.when(b == 0)
    def _():
        start_all(b, slot)

    @pl.when(b + 1 < nb)
    def _():
        start_all(b + 1, 1 - slot)

    def wait_body(j, c):
        pltpu.make_async_copy(knew_ref.at[b], kbuf.at[slot, j], sem.at[0, slot]).wait()
        pltpu.make_async_copy(vnew_ref.at[b], vbuf.at[slot, j], sem.at[1, slot]).wait()
        return c

    lax.fori_loop(0, topk, wait_body, 0)

    k = kbuf[slot]
    v = vbuf[slot]
    s = jnp.sum(k * q_ref[...], axis=2, keepdims=True) * HEAD_DIM ** -0.5
    e = jnp.exp(s - jnp.max(s, axis=0, keepdims=True))
    p = e / jnp.sum(e, axis=0, keepdims=True)
    o_ref[...] = jnp.sum(p * v, axis=0, keepdims=True).astype(o_ref.dtype)


def _sample_attend(q, k_new, v_new, idx, cache_k, cache_v, pages):
    db, n_pages = pages.shape
    topk = idx.shape[1]
    tile = (N_HEADS, HEAD_DIM)
    grid_spec = pltpu.PrefetchScalarGridSpec(
        num_scalar_prefetch=2,
        grid=(db,),
        in_specs=[
            pl.BlockSpec((1,) + tile, lambda b, ix, pg: (b, 0, 0)),
            pl.BlockSpec(memory_space=pl.ANY),
            pl.BlockSpec(memory_space=pl.ANY),
            pl.BlockSpec(memory_space=pl.ANY),
            pl.BlockSpec(memory_space=pl.ANY),
        ],
        out_specs=pl.BlockSpec((1,) + tile, lambda b, ix, pg: (b, 0, 0)),
        scratch_shapes=[
            pltpu.VMEM((2, topk) + tile, F32),
            pltpu.VMEM((2, topk) + tile, F32),
            pltpu.SemaphoreType.DMA((2, 2)),
        ],
    )
    out = pl.pallas_call(
        functools.partial(_sgather_kernel, past=n_pages * PAGE_SIZE, topk=topk),
        grid_spec=grid_spec,
        out_shape=jax.ShapeDtypeStruct((db,) + tile, BF16),
        compiler_params=_params("arbitrary"),
        name="sample_attend",
    )(idx, pages, q.reshape((db,) + tile), k_new.reshape((db,) + tile), v_new.reshape((db,) + tile),
      cache_k.reshape((-1, PAGE_SIZE) + tile), cache_v.reshape((-1, PAGE_SIZE) + tile))
    return out.reshape(db, N_HEADS * HEAD_DIM)


def _merge_kernel(x_ref, g_ref, ya_ref, yb_ref, ym_ref, woa_ref, wob_ref, wom_ref, wout_ref,
                  gpm_ref, gpf_ref, x1_ref, h2_ref):
    d = x_ref.shape[1]
    merged = (g_ref[:, 0:d].astype(F32) * jnp.dot(ya_ref[...], woa_ref[...], preferred_element_type=F32)
              + g_ref[:, d:2 * d].astype(F32) * jnp.dot(yb_ref[...], wob_ref[...], preferred_element_type=F32)
              + g_ref[:, 2 * d:3 * d].astype(F32) * jnp.dot(ym_ref[...], wom_ref[...], preferred_element_type=F32))
    o = jnp.dot(merged.astype(BF16), wout_ref[...], preferred_element_type=F32)
    x1 = x_ref[...] + _rmsnorm_rows(o, gpm_ref[...])
    x1_ref[...] = x1
    h2_ref[...] = _rmsnorm_rows(x1, gpf_ref[...]).astype(h2_ref.dtype)


def _merge(x, gates, y_a, y_b, y_m, w_oa, w_ob, w_om, w_out, g_post_mix, g_pre_ffn):
    m, d = x.shape
    tm = _tile(m, 256)
    once = pl.Buffered(1)

    def rows(width):
        return pl.BlockSpec((tm, width), lambda i: (i, 0))

    def whole(a):
        return pl.BlockSpec(a.shape, lambda i: (0, 0), pipeline_mode=once)

    return pl.pallas_call(
        _merge_kernel,
        grid=(m // tm,),
        in_specs=[rows(d), rows(N_BRANCH * d), rows(y_a.shape[1]), rows(y_b.shape[1]), rows(y_m.shape[1]),
                  whole(w_oa), whole(w_ob), whole(w_om), whole(w_out),
                  pl.BlockSpec((1, d), lambda i: (0, 0)), pl.BlockSpec((1, d), lambda i: (0, 0))],
        out_specs=[rows(d), rows(d)],
        out_shape=[jax.ShapeDtypeStruct((m, d), F32), jax.ShapeDtypeStruct((m, d), BF16)],
        compiler_params=_params("parallel"),
        name="merge",
    )(x, gates, y_a, y_b, y_m, w_oa, w_ob, w_om, w_out, g_post_mix.reshape(1, d), g_pre_ffn.reshape(1, d))


def _ffn_up_kernel(h_ref, wg_ref, wu_ref, o_ref, *, rows):
    wg = wg_ref[...].astype(BF16)
    wu = wu_ref[...].astype(BF16)

    def body(r, carry):
        r0 = pl.multiple_of(r * rows, rows)
        h = h_ref[pl.ds(r0, rows), :]
        a = jnp.dot(h, wg, preferred_element_type=F32)
        c = jnp.dot(h, wu, preferred_element_type=F32)
        o_ref[pl.ds(r0, rows), :] = (jax.nn.silu(a) * c).astype(o_ref.dtype)
        return carry

    lax.fori_loop(0, h_ref.shape[0] // rows, body, 0)


def _ffn_up(h2, w_gu):
    m, d = h2.shape
    d_ff = w_gu.shape[1] // 2
    tm = _tile(m, FFN_UP_TM)
    tf = _tile(d_ff, 2 * LANES)
    nf = d_ff // tf
    return pl.pallas_call(
        functools.partial(_ffn_up_kernel, rows=_tile(tm, 2048)),
        grid=(m // tm, nf),
        in_specs=[
            pl.BlockSpec((tm, d), lambda i, j: (i, 0)),
            pl.BlockSpec((d, tf), lambda i, j: (0, j)),
            pl.BlockSpec((d, tf), lambda i, j: (0, j + nf)),
        ],
        out_specs=pl.BlockSpec((tm, tf), lambda i, j: (i, j)),
        out_shape=jax.ShapeDtypeStruct((m, d_ff), BF16),
        compiler_params=_params("parallel", "parallel"),
        name="ffn_up",
    )(h2, w_gu, w_gu)


def _ffn_down_kernel(a_ref, wd_ref, x1_ref, gpf_ref, o_ref):
    y = jnp.dot(a_ref[...], wd_ref[...], preferred_element_type=F32)
    o_ref[...] = x1_ref[...] + _rmsnorm_rows(y, gpf_ref[...])


def _ffn_down(act, x1, w_down, g_post_ffn):
    m, d_ff = act.shape
    d = w_down.shape[1]
    tm = _tile(m, 256)
    return pl.pallas_call(
        _ffn_down_kernel,
        grid=(m // tm,),
        in_specs=[
            pl.BlockSpec((tm, d_ff), lambda i: (i, 0)),
            pl.BlockSpec((d_ff, d), lambda i: (0, 0), pipeline_mode=pl.Buffered(1)),
            pl.BlockSpec((tm, d), lambda i: (i, 0)),
            pl.BlockSpec((1, d), lambda i: (0, 0)),
        ],
        out_specs=pl.BlockSpec((tm, d), lambda i: (i, 0)),
        out_shape=jax.ShapeDtypeStruct((m, d), F32),
        compiler_params=_params("parallel"),
        name="ffn_down",
    )(act, w_down, x1, g_post_ffn.reshape(1, d))


def _project_common(x2d, g_pre_mix, w_t, ln_g, ln_b, uv_dtype, q_dtype, q_scale, v_heads_t):
    d = x2d.shape[1]
    aw, iw = N_HEADS * HEAD_DIM, IDX_HEADS * IDX_DIM
    h = _rmsnorm(x2d, g_pre_mix, BF16)
    uv = _project_uv(h, w_t, ln_g, ln_b, uv_dtype)
    (q,) = _matmul(h, w_t, (q_dtype,), scale=q_scale, rows=(d, aw), name="proj_q")
    k, k_bf = _matmul(h, w_t, (F32, BF16), rows=(d + aw, aw), name="proj_k")
    v_out = _matmul(h, w_t, (F32,), rows=(d + 2 * aw, aw), heads_t=v_heads_t, name="proj_v")
    v, v_t = v_out if v_heads_t is not None else (v_out[0], None)
    (qi,) = _matmul(h, w_t, (BF16,), rows=(d + 3 * aw, iw), name="proj_qi")
    (kw,) = _matmul(h, w_t, (F32,), rows=(d + 3 * aw + iw, 2 * LANES), name="proj_kidx")
    o_qm = d + 3 * aw + iw + IDX_DIM + IDX_HEADS
    o_g = o_qm + MEM_HEADS * MEM_HEAD_DIM
    (qm,) = _matmul(h, w_t, (BF16,), rows=(o_qm, o_g - o_qm), name="proj_qm")
    (gates,) = _matmul(h, w_t, (BF16,), act="sigmoid", rows=(o_g, N_BRANCH * d), name="proj_gates")
    kidx = kw[:, :IDX_DIM]
    wi = kw[:, IDX_DIM:IDX_DIM + IDX_HEADS]
    return uv, q, k, k_bf, v, v_t, qi, kidx, wi, qm, gates


def kernel(x_prompt, x_sample, mem_prompt, cache_k, cache_v, cache_kidx, cache_mem_k, cache_mem_v, page_table, g_pre_mix, g_post_mix, g_pre_ffn, g_post_ffn, g_mem, ln_v_g, ln_v_b, w_in, w_s, b_s, w_oa, w_ob, w_om, w_out, w_mem_kv, w_gu, w_down):
    depth = w_in.shape[0]
    b, t, d = x_prompt.shape
    db, dt, _ = x_sample.shape
    assert dt == 1, "sample path handles one new token per sequence"
    n_mem = mem_prompt.shape[1]
    aw = N_HEADS * HEAD_DIM
    mw = MEM_HEADS * MEM_HEAD_DIM
    gw = d // 2
    past = page_table.shape[1] * PAGE_SIZE

    yp = x_prompt.reshape(b * t, d)
    ys = x_sample.reshape(db * dt, d)
    outs = [[] for _ in range(9)]
    for l in range(depth):
        ws = w_in[l].T
        woa, wob, wom, wout = (w.astype(BF16) for w in (w_oa[l], w_ob[l], w_om[l], w_out[l]))
        wdown, wmem = w_down[l].astype(BF16), w_mem_kv[l].T.astype(BF16)

        uv, q, k, k_bf, v, v_t, qi, kidx, wi, qm, gates = _project_common(
            yp, g_pre_mix[l], ws, ln_v_g[l], ln_v_b[l], BF16, BF16, DSA_LOG2_SCALE, (b, BF16))
        y_a = _gmlp_spatial(uv, w_s[l], b_s[l])
        y_b = _dsa_prompt(
            q.reshape(b, t, aw), qi.reshape(b, t, -1), jnp.swapaxes(wi.reshape(b, t, IDX_HEADS), 1, 2),
            k_bf.reshape(b, t, aw), kidx.astype(BF16).reshape(b, t, IDX_DIM), v_t).reshape(b * t, aw)
        hm = _rmsnorm(mem_prompt.reshape(b * n_mem, d), g_mem[l], BF16)
        (mkv,) = _matmul(hm, wmem, (F32,), name="proj_mem_kv")
        mk = mkv[:, :mw].reshape(b, n_mem, mw)
        mv = mkv[:, mw:].reshape(b, n_mem, mw)
        y_m = _mem_attend(qm.reshape(b, t, mw), mk, mv).reshape(b * t, mw)
        x1, h2 = _merge(yp, gates, y_a, y_b, y_m, woa, wob, wom, wout, g_post_mix[l], g_pre_ffn[l])
        yp = _ffn_down(_ffn_up(h2, w_gu[l]), x1, wdown, g_post_ffn[l])
        outs[0].append(k.reshape(b, t, N_HEADS, HEAD_DIM))
        outs[1].append(v.reshape(b, t, N_HEADS, HEAD_DIM))
        outs[2].append(kidx.reshape(b, t, IDX_DIM))
        outs[3].append(mk.reshape(b, n_mem, MEM_HEADS, MEM_HEAD_DIM))
        outs[4].append(mv.reshape(b, n_mem, MEM_HEADS, MEM_HEAD_DIM))

        uv, q, k, _, v, _, qi, kidx, wi, qm, gates = _project_common(
            ys, g_pre_mix[l], ws, ln_v_g[l], ln_v_b[l], F32, F32, None, None)
        y_a = _gmlp_first_row(uv, w_s[l], b_s[l])
        pages = page_table + l * cache_k.shape[1]
        idx = _sample_indices(qi, wi, kidx, cache_kidx, pages)
        y_b = _sample_attend(q, k, v, idx, cache_k, cache_v, pages)
        qm_pad = jnp.pad(qm.reshape(db, 1, mw), ((0, 0), (0, 2 * SUBLANES - 1), (0, 0)))
        y_m = _mem_attend(qm_pad, cache_mem_k[l].reshape(db, n_mem, mw),
                          cache_mem_v[l].reshape(db, n_mem, mw))[:, 0, :]
        x1, h2 = _merge(ys, gates, y_a, y_b, y_m, woa, wob, wom, wout, g_post_mix[l], g_pre_ffn[l])
        ys = _ffn_down(_ffn_up(h2, w_gu[l]), x1, wdown, g_post_ffn[l])
        outs[5].append(k.reshape(db, dt, N_HEADS, HEAD_DIM))
        outs[6].append(v.reshape(db, dt, N_HEADS, HEAD_DIM))
        outs[7].append(kidx.reshape(db, dt, IDX_DIM))
        outs[8].append(uv[:, gw:].reshape(db, dt, gw))

    return (yp.reshape(b, t, d), ys.reshape(db, dt, d)) + tuple(jnp.stack(o) for o in outs)
```

```python
import functools

import jax
import jax.numpy as jnp
from jax import lax
from jax.experimental import pallas as pl
from jax.experimental.pallas import tpu as pltpu

EPS = 1e-6
CHUNK = 128
GM_GROUPS = 8
N_HEADS = 8
HEAD_DIM = 128
IDX_HEADS = 16
IDX_DIM = 128
TOPK_MAX = 256
MEM_HEADS = 4
MEM_HEAD_DIM = 128
N_BRANCH = 3
PAGE_SIZE = 128

LANES = 128
SUBLANES = 8
VMEM_LIMIT_BYTES = 56 * 2**20
INT_MIN = -2**31
KEY_LOWEST = INT_MIN + 0x800000
KEY_INF = 0x7F800000
NEG = -1e30
LOG2_E = 1.4426950408889634
DSA_LOG2_SCALE = HEAD_DIM ** -0.5 * LOG2_E
ONES_ROWS = 16
COUNT_ROWS = 64
BF16_ROWS = 16
COARSE_COUNT_ROWS = 128
MM_TM, MM_TN = 2048, 1024
FFN_UP_TM = 8192

F32 = jnp.float32
BF16 = jnp.bfloat16
NT_DIMS = (((1,), (1,)), ((), ()))


def _params(*sem):
    return pltpu.CompilerParams(dimension_semantics=sem, vmem_limit_bytes=VMEM_LIMIT_BYTES)


def _tile(n, pref):
    if n <= pref:
        return n
    t = pref
    while n % t:
        t //= 2
    return t


def _key_to_float(key):
    key = jnp.clip(key, KEY_LOWEST, KEY_INF)
    return pltpu.bitcast(key ^ ((key >> 31) & 0x7FFFFFFF), F32)


def _key16_to_float(key):
    key = jnp.clip(key, KEY_LOWEST >> 16, KEY_INF >> 16)
    return pltpu.bitcast((key ^ ((key >> 31) & 0x7FFF)) << 16, F32)


def _key16_to_key(key):
    key = jnp.clip(key, -2**15, 2**15 - 1)
    return (key << 16) + ((key >> 31) & 0xFFFF)


def _build_key(count_ge, topk, like, bits):
    base = jnp.where(count_ge(jnp.zeros_like(like)) >= topk, 0, -(1 << (bits - 1))).astype(jnp.int32)

    def body(b, t):
        cand = t | jnp.left_shift(jnp.int32(1), bits - 2 - b)
        return jnp.where(count_ge(cand) >= topk, cand, t)

    return lax.fori_loop(0, bits - 1, body, base)


def _kth_largest_key(count_ge, topk, like, count_ge_bf16=None):
    if count_ge_bf16 is None:
        return _build_key(count_ge, topk, like, 32)
    k16 = _build_key(count_ge_bf16, topk, like, 16)
    lo, hi = _key16_to_key(k16 - 1), _key16_to_key(k16 + 1)

    def bisect(_, lohi):
        lo, hi = lohi
        mid = lo + ((hi - lo) >> 1)
        ok = count_ge(mid) >= topk
        return jnp.where(ok, mid, lo), jnp.where(ok, hi, mid)

    lo, _ = lax.fori_loop(0, 17, bisect, (lo, hi))
    return lo


def _rmsnorm_rows(x, g):
    return x * lax.rsqrt(jnp.mean(x * x, axis=-1, keepdims=True) + EPS) * g


def _rmsnorm_kernel(x_ref, g_ref, o_ref):
    o_ref[...] = _rmsnorm_rows(x_ref[...], g_ref[...]).astype(o_ref.dtype)


def _rmsnorm(x, g, out_dtype):
    m, d = x.shape
    tm = _tile(m, 512)
    return pl.pallas_call(
        _rmsnorm_kernel,
        grid=(m // tm,),
        in_specs=[pl.BlockSpec((tm, d), lambda i: (i, 0)), pl.BlockSpec((1, d), lambda i: (0, 0))],
        out_specs=pl.BlockSpec((tm, d), lambda i: (i, 0)),
        out_shape=jax.ShapeDtypeStruct((m, d), out_dtype),
        compiler_params=_params("parallel"),
        name="rmsnorm",
    )(x, g.reshape(1, d))


def _mm_kernel(x_ref, w_ref, *o_refs, act, scale, heads_t):
    acc = lax.dot_general(x_ref[...], w_ref[...].astype(BF16), NT_DIMS, preferred_element_type=F32)
    if act == "sigmoid":
        acc = jax.nn.sigmoid(acc)
    if scale is not None:
        acc = acc * scale
    if heads_t:
        o_refs, t_ref = o_refs[:-1], o_refs[-1]
        rows = HEAD_DIM + ONES_ROWS
        for h in range(acc.shape[1] // HEAD_DIM):
            t_ref[0, h * rows:h * rows + HEAD_DIM, :] = acc[:, h * HEAD_DIM:(h + 1) * HEAD_DIM].T.astype(t_ref.dtype)
            t_ref[0, h * rows + HEAD_DIM:(h + 1) * rows, :] = jnp.ones((ONES_ROWS, acc.shape[0]), t_ref.dtype)
    for o_ref in o_refs:
        o_ref[...] = acc.astype(o_ref.dtype)


def _matmul(x, w_t, out_dtypes, act=None, scale=None, rows=None, heads_t=None, name="matmul"):
    m, k = x.shape
    row0, n = rows if rows is not None else (0, w_t.shape[0])
    tm = _tile(m if heads_t is None else m // heads_t[0], MM_TM)
    tn = _tile(n, MM_TN if w_t.dtype == BF16 else MM_TN // 2)
    assert row0 % SUBLANES == 0
    out_specs = [pl.BlockSpec((tm, tn), lambda i, j: (i, j)) for _ in out_dtypes]
    out_shapes = [jax.ShapeDtypeStruct((m, n), dt) for dt in out_dtypes]
    if heads_t is not None:
        batch, dt = heads_t
        per_batch = m // batch // tm
        t_rows = tn // HEAD_DIM * (HEAD_DIM + ONES_ROWS)
        out_specs.append(pl.BlockSpec((1, t_rows, tm), lambda i, j: (i // per_batch, j, i % per_batch)))
        out_shapes.append(jax.ShapeDtypeStruct((batch, n // tn * t_rows, m // batch), dt))
    outs = pl.pallas_call(
        functools.partial(_mm_kernel, act=act, scale=scale, heads_t=heads_t is not None),
        grid=(m // tm, n // tn),
        in_specs=[pl.BlockSpec((tm, k), lambda i, j: (i, 0)),
                  pl.BlockSpec((pl.Element(tn), pl.Element(k)),
                               lambda i, j: (pl.multiple_of(row0 + j * tn, SUBLANES), 0))],
        out_specs=out_specs,
        out_shape=out_shapes,
        compiler_params=_params("parallel", "parallel"),
        name=name,
    )(x, w_t)
    return outs


def _uv_kernel(x_ref, w_ref, lng_ref, lnb_ref, o_ref):
    y = jax.nn.gelu(lax.dot_general(x_ref[...], w_ref[...].astype(BF16), NT_DIMS,
                                    preferred_element_type=F32))

    @pl.when(pl.program_id(1) == 0)
    def _():
        o_ref[...] = y.astype(o_ref.dtype)

    @pl.when(pl.program_id(1) == 1)
    def _():
        yc = y - jnp.mean(y, axis=-1, keepdims=True)
        yn = yc * lax.rsqrt(jnp.mean(yc * yc, axis=-1, keepdims=True) + EPS)
        o_ref[...] = (yn * lng_ref[...] + lnb_ref[...]).astype(o_ref.dtype)


def _project_uv(h, w_t, ln_g, ln_b, out_dtype):
    m, k = h.shape
    gw = ln_g.shape[0]
    tm = _tile(m, 1024)
    return pl.pallas_call(
        _uv_kernel,
        grid=(m // tm, 2),
        in_specs=[
            pl.BlockSpec((tm, k), lambda i, j: (i, 0)),
            pl.BlockSpec((gw, k), lambda i, j: (j, 0)),
            pl.BlockSpec((1, gw), lambda i, j: (0, 0)),
            pl.BlockSpec((1, gw), lambda i, j: (0, 0)),
        ],
        out_specs=pl.BlockSpec((tm, gw), lambda i, j: (i, j)),
        out_shape=jax.ShapeDtypeStruct((m, 2 * gw), out_dtype),
        compiler_params=_params("parallel", "arbitrary"),
        name="project_uv",
    )(h, w_t, ln_g.reshape(1, gw), ln_b.reshape(1, gw))


def _gmlp_kernel(u_ref, v_ref, ws_ref, bt_ref, o_ref, *, n_chunks):
    c = CHUNK
    gd = u_ref.shape[1] // GM_GROUPS
    causal = lax.broadcasted_iota(jnp.int32, (c, c), 1) <= lax.broadcasted_iota(jnp.int32, (c, c), 0)
    for g in range(GM_GROUPS):
        w = jnp.where(causal, ws_ref[g], 0.0).astype(v_ref.dtype)
        bcol = bt_ref[:, g:g + 1]
        for ci in range(n_chunks):
            rows = slice(ci * c, (ci + 1) * c)
            cols = slice(g * gd, (g + 1) * gd)
            mixed = jnp.dot(w, v_ref[rows, cols], preferred_element_type=F32) + bcol
            o_ref[rows, cols] = (u_ref[rows, cols].astype(F32) * mixed).astype(o_ref.dtype)


def _gmlp_spatial(uv, w_s, b_s):
    m = uv.shape[0]
    gw = uv.shape[1] // 2
    tm = _tile(m, 4 * CHUNK)
    return pl.pallas_call(
        functools.partial(_gmlp_kernel, n_chunks=tm // CHUNK),
        grid=(m // tm,),
        in_specs=[
            pl.BlockSpec((tm, gw), lambda i: (i, 0)),
            pl.BlockSpec((tm, gw), lambda i: (i, 1)),
            pl.BlockSpec((GM_GROUPS, CHUNK, CHUNK), lambda i: (0, 0, 0)),
            pl.BlockSpec((CHUNK, GM_GROUPS), lambda i: (0, 0)),
        ],
        out_specs=pl.BlockSpec((tm, gw), lambda i: (i, 0)),
        out_shape=jax.ShapeDtypeStruct((m, gw), BF16),
        compiler_params=_params("parallel"),
        name="gmlp_spatial",
    )(uv, uv, w_s, b_s.T)


def _gmlp_first_kernel(u_ref, v_ref, w_ref, b_ref, o_ref):
    o_ref[...] = (u_ref[...] * (w_ref[...] * v_ref[...] + b_ref[...])).astype(o_ref.dtype)


def _gmlp_first_row(uv, w_s, b_s):
    m = uv.shape[0]
    gw = uv.shape[1] // 2
    gd = gw // GM_GROUPS
    w_row = jnp.repeat(w_s[:, 0, 0], gd).reshape(1, gw)
    b_row = jnp.repeat(b_s[:, 0], gd).reshape(1, gw)
    return pl.pallas_call(
        _gmlp_first_kernel,
        grid=(1,),
        in_specs=[
            pl.BlockSpec((m, gw), lambda i: (0, 0)),
            pl.BlockSpec((m, gw), lambda i: (0, 1)),
            pl.BlockSpec((1, gw), lambda i: (0, 0)),
            pl.BlockSpec((1, gw), lambda i: (0, 0)),
        ],
        out_specs=pl.BlockSpec((m, gw), lambda i: (0, 0)),
        out_shape=jax.ShapeDtypeStruct((m, gw), BF16),
        compiler_params=_params("arbitrary"),
        name="gmlp_first_row",
    )(uv, uv, w_row, b_row)


def _dsa_kernel(q_ref, qi_ref, wit_ref, qin_ref, witn_ref, k_ref, kidx_ref, vt_ref, o_ref,
                sc_ref, sc16_ref, m_ref, acc_ref, s_ref, p_ref, *, tq, tk, topk, wscale):
    i = pl.program_id(1)
    last = pl.num_programs(1) - 1
    cur = lax.rem(i, 2)
    n_chunks = ((i + 1) * tq + tk - 1) // tk
    n_next = ((i + 2) * tq + tk - 1) // tk
    vrows = vt_ref.shape[1] // N_HEADS

    def idx_chunk(c, qi_r, wit_r, blk, slot):
        k0 = pl.multiple_of(c * tk, tk)
        kc = kidx_ref[0, pl.ds(k0, tk), :]
        acc = jnp.zeros((tk, tq), F32)
        for h in range(IDX_HEADS):
            s = lax.dot_general(kc, qi_r[0, :, h * IDX_DIM:(h + 1) * IDX_DIM], NT_DIMS,
                                preferred_element_type=F32)
            acc = acc + jnp.maximum(s, 0.0) * (wit_r[0, h:h + 1, :] * wscale)
        kpos = k0 + lax.broadcasted_iota(jnp.int32, (tk, 1), 0)
        qpos = blk * tq + lax.broadcasted_iota(jnp.int32, (1, tq), 1)
        sc = jnp.where(kpos <= qpos, acc, -jnp.inf)
        sc_ref[slot, pl.ds(k0, tk), :] = sc
        sc16_ref[slot, pl.ds(k0, tk), :] = sc.astype(BF16)

    def pad_odd(n, slot):
        @pl.when(n % 2 == 1)
        def _():
            rows = pl.ds(pl.multiple_of(n * tk, tk), tk)
            sc_ref[slot, rows, :] = jnp.full((tk, tq), -jnp.inf, F32)
            sc16_ref[slot, rows, :] = jnp.full((tk, tq), -jnp.inf, BF16)

    @pl.when(i == 0)
    def _():
        def body(c, carry):
            idx_chunk(c, qi_ref, wit_ref, i, cur)
            return carry

        lax.fori_loop(0, n_chunks, body, 0)
        pad_odd(n_chunks, cur)

    def count_ge(cand):
        cand_f = _key_to_float(cand)

        def body(c, cnt):
            k0 = pl.multiple_of(c * (2 * tk), 2 * tk)
            hit = jnp.where(sc_ref[cur, pl.ds(k0, 2 * tk), :] >= cand_f, 1.0, 0.0)
            return cnt + jnp.sum(hit.reshape(2 * tk // COUNT_ROWS, COUNT_ROWS, tq), axis=0)

        cnt = lax.fori_loop(0, (n_chunks + 1) // 2, body, jnp.zeros((COUNT_ROWS, tq), F32))
        return jnp.sum(cnt, axis=0, keepdims=True)

    def count_ge_bf16(cand16):
        cand_c = jnp.broadcast_to(_key16_to_float(cand16), (BF16_ROWS, tq)).astype(BF16)
        one, zero = jnp.ones((), BF16), jnp.zeros((), BF16)
        parts = 2 * tk // COARSE_COUNT_ROWS

        def body(c, cnt):
            k0 = pl.multiple_of(c * (2 * tk), 2 * tk)
            x = sc16_ref[cur, pl.ds(k0, 2 * tk), :].reshape(2 * tk // BF16_ROWS, BF16_ROWS, tq)
            hit = jnp.where(x >= cand_c[None], one, zero).reshape(parts, COARSE_COUNT_ROWS, tq)
            for part in range(parts):
                cnt = cnt + hit[part]
            return cnt

        cnt = lax.fori_loop(0, (n_chunks + 1) // 2, body, jnp.zeros((COARSE_COUNT_ROWS, tq), BF16))
        return jnp.sum(cnt.astype(F32), axis=0, keepdims=True)

    thr = _key_to_float(_kth_largest_key(count_ge, topk, jnp.zeros((1, tq), jnp.int32), count_ge_bf16))

    m_ref[...] = jnp.full(m_ref.shape, NEG, F32)
    acc_ref[...] = jnp.zeros(acc_ref.shape, F32)

    def att_chunk(c):
        k0 = pl.multiple_of(c * tk, tk)
        bias = jnp.where(sc_ref[cur, pl.ds(k0, tk), :] >= thr, 0.0, NEG)
        m_old, m_new = [], []
        for h in range(N_HEADS):
            hs = slice(h * HEAD_DIM, (h + 1) * HEAD_DIM)
            s = lax.dot_general(k_ref[0, pl.ds(k0, tk), hs], q_ref[0, :, hs], NT_DIMS,
                                preferred_element_type=F32) + bias
            s_ref[h] = s
            m_old.append(m_ref[h])
            m_new.append(jnp.maximum(m_old[h], jnp.max(s, axis=0, keepdims=True)))
            m_ref[h] = m_new[h]
        for h in range(N_HEADS):
            p_ref[h] = jnp.exp2(s_ref[h] - m_new[h][0:1, :]).astype(BF16)
        for h in range(N_HEADS):
            pv = jnp.dot(vt_ref[0, h * vrows:(h + 1) * vrows, pl.ds(k0, tk)], p_ref[h],
                         preferred_element_type=F32)
            acc_ref[h] = jnp.exp2(m_old[h] - m_new[h])[0:1, :] * acc_ref[h] + pv

    @pl.when(i < last)
    def _():
        def fused(c, carry):
            att_chunk(c)
            idx_chunk(c, qin_ref, witn_ref, i + 1, 1 - cur)
            return carry

        def tail(c, carry):
            idx_chunk(c, qin_ref, witn_ref, i + 1, 1 - cur)
            return carry

        lax.fori_loop(0, n_chunks, fused, 0)
        lax.fori_loop(n_chunks, n_next, tail, 0)
        pad_odd(n_next, 1 - cur)

    @pl.when(i == last)
    def _():
        def body(c, carry):
            att_chunk(c)
            return carry

        lax.fori_loop(0, n_chunks, body, 0)

    for h in range(N_HEADS):
        o = acc_ref[h, 0:HEAD_DIM, :] / acc_ref[h, HEAD_DIM:HEAD_DIM + 1, :]
        o_ref[0, :, h * HEAD_DIM:(h + 1) * HEAD_DIM] = o.T.astype(o_ref.dtype)


def _dsa_prompt(q, qi, wi_t, k, kidx, v_t, tq=256, tk=256):
    b, t, aw = q.shape
    tq = _tile(t, tq)
    tk = _tile(tq, tk)
    topk = min(TOPK_MAX, t // 4)
    once = pl.Buffered(1)
    nq = t // tq

    def next_block(i):
        return jnp.minimum(i + 1, nq - 1)

    return pl.pallas_call(
        functools.partial(_dsa_kernel, tq=tq, tk=tk, topk=topk,
                          wscale=IDX_HEADS ** -0.5 * IDX_DIM ** -0.5),
        grid=(b, nq),
        in_specs=[
            pl.BlockSpec((1, tq, aw), lambda bi, i: (bi, i, 0)),
            pl.BlockSpec((1, tq, IDX_HEADS * IDX_DIM), lambda bi, i: (bi, i, 0)),
            pl.BlockSpec((1, IDX_HEADS, tq), lambda bi, i: (bi, 0, i)),
            pl.BlockSpec((1, tq, IDX_HEADS * IDX_DIM), lambda bi, i: (bi, next_block(i), 0)),
            pl.BlockSpec((1, IDX_HEADS, tq), lambda bi, i: (bi, 0, next_block(i))),
            pl.BlockSpec((1, t, aw), lambda bi, i: (bi, 0, 0), pipeline_mode=once),
            pl.BlockSpec((1, t, IDX_DIM), lambda bi, i: (bi, 0, 0), pipeline_mode=once),
            pl.BlockSpec((1, v_t.shape[1], t), lambda bi, i: (bi, 0, 0), pipeline_mode=once),
        ],
        out_specs=pl.BlockSpec((1, tq, aw), lambda bi, i: (bi, i, 0)),
        out_shape=jax.ShapeDtypeStruct((b, t, aw), BF16),
        scratch_shapes=[
            pltpu.VMEM((2, t + tk, tq), F32),
            pltpu.VMEM((2, t + tk, tq), BF16),
            pltpu.VMEM((N_HEADS, SUBLANES, tq), F32),
            pltpu.VMEM((N_HEADS, HEAD_DIM + ONES_ROWS, tq), F32),
            pltpu.VMEM((N_HEADS, tk, tq), F32),
            pltpu.VMEM((N_HEADS, tk, tq), BF16),
        ],
        compiler_params=_params("parallel", "arbitrary"),
        name="dsa_prompt",
    )(q, qi, wi_t, qi, wi_t, k, kidx, v_t)


def _memattn_kernel(qm_ref, mk_ref, mv_ref, o_ref):
    scale = MEM_HEAD_DIM ** -0.5
    for h in range(MEM_HEADS):
        hs = slice(h * MEM_HEAD_DIM, (h + 1) * MEM_HEAD_DIM)
        s = lax.dot_general(qm_ref[0, :, hs], mk_ref[0, :, hs].astype(BF16), NT_DIMS,
                            preferred_element_type=F32) * scale
        e = jnp.exp(s - jnp.max(s, axis=-1, keepdims=True))
        p = e / jnp.sum(e, axis=-1, keepdims=True)
        o = jnp.dot(p.astype(BF16), mv_ref[0, :, hs].astype(BF16), preferred_element_type=F32)
        o_ref[0, :, hs] = o.astype(o_ref.dtype)


def _mem_attend(qm, mk, mv):
    b, t, mw = qm.shape
    n_mem = mk.shape[1]
    tm = _tile(t, 512)
    return pl.pallas_call(
        _memattn_kernel,
        grid=(b, t // tm),
        in_specs=[
            pl.BlockSpec((1, tm, mw), lambda bi, i: (bi, i, 0)),
            pl.BlockSpec((1, n_mem, mw), lambda bi, i: (bi, 0, 0)),
            pl.BlockSpec((1, n_mem, mw), lambda bi, i: (bi, 0, 0)),
        ],
        out_specs=pl.BlockSpec((1, tm, mw), lambda bi, i: (bi, i, 0)),
        out_shape=jax.ShapeDtypeStruct((b, t, mw), BF16),
        compiler_params=_params("parallel", "parallel"),
        name="mem_attend",
    )(qm, mk, mv)


def _strict_triangle(n, lower):
    r = lax.broadcasted_iota(jnp.int32, (n, n), 0)
    c = lax.broadcasted_iota(jnp.int32, (n, n), 1)
    return jnp.where((c < r) if lower else (r < c), 1.0, 0.0).astype(BF16)


def _index_order_rank(flag):
    rows = flag.shape[0]
    within = jnp.dot(flag.astype(BF16), _strict_triangle(LANES, False), preferred_element_type=F32)
    per_row = jnp.broadcast_to(jnp.sum(flag, axis=1, keepdims=True), flag.shape)
    before = jnp.dot(_strict_triangle(rows, True), per_row.astype(BF16), preferred_element_type=F32)
    return before + within


def _sindex_kernel(pg_ref, qi_ref, wi_ref, knew_ref, cache_ref, idx_ref, kbuf, sem, sc_ref, pos_ref,
                   *, n_pages, topk, wscale, group):
    b = pl.program_id(0)
    nb = pl.num_programs(0)
    slot = lax.rem(b, 2)

    def page_copy(bb, p, sl):
        return pltpu.make_async_copy(cache_ref.at[pg_ref[bb, p]], kbuf.at[sl, p], sem.at[sl])

    def start_all(bb, sl):
        lax.fori_loop(0, n_pages, lambda p, c: (page_copy(bb, p, sl).start(), c)[1], 0)

    @pl.when(b == 0)
    def _():
        start_all(b, slot)

    @pl.when(b + 1 < nb)
    def _():
        start_all(b + 1, 1 - slot)

    lax.fori_loop(0, n_pages, lambda p, c: (page_copy(b, p, slot).wait(), c)[1], 0)

    qi = qi_ref[0]
    w = wi_ref[0] * wscale
    for g in range(n_pages // group):
        kc = kbuf[slot, g * group:(g + 1) * group].reshape(group * PAGE_SIZE, IDX_DIM).astype(BF16)
        s = lax.dot_general(qi, kc, NT_DIMS, preferred_element_type=F32)
        r = jnp.sum(jnp.maximum(s, 0.0) * w, axis=0, keepdims=True)
        for j in range(group):
            sc_ref[g * group + j:g * group + j + 1, :] = r[:, j * PAGE_SIZE:(j + 1) * PAGE_SIZE]
    s_new = jnp.sum(qi.astype(F32) * knew_ref[0].astype(BF16).astype(F32), axis=1, keepdims=True)
    s_new = jnp.sum(jnp.maximum(s_new, 0.0) * w, axis=0, keepdims=True)
    pad_rows = sc_ref.shape[0] - n_pages
    first = (lax.broadcasted_iota(jnp.int32, (pad_rows, LANES), 0) == 0) & (
        lax.broadcasted_iota(jnp.int32, (pad_rows, LANES), 1) == 0)
    sc_ref[n_pages:, :] = jnp.where(first, s_new, -jnp.inf)

    s = sc_ref[...]

    def count(hit):
        return jnp.sum(jnp.sum(jnp.where(hit, 1.0, 0.0), axis=1, keepdims=True), axis=0, keepdims=True)

    thr = _key_to_float(_kth_largest_key(lambda cand: count(s >= _key_to_float(cand)), topk,
                                         jnp.zeros((1, 1), jnp.int32)))
    above = s > thr
    tied = jnp.where(s == thr, 1.0, 0.0)
    keep_tied = (tied > 0.5) & (_index_order_rank(tied) < topk - count(above))
    chosen = jnp.where(above | keep_tied, 1.0, 0.0)
    pos_ref[...] = jnp.where(chosen > 0.5, _index_order_rank(chosen), -1.0)

    slot_id = lax.broadcasted_iota(jnp.int32, (topk, LANES), 0).astype(F32)
    lane = lax.broadcasted_iota(jnp.int32, (1, LANES), 1)

    def place(p, acc):
        key_pos = (p * PAGE_SIZE + lane).astype(F32)
        return acc + jnp.where(pos_ref[pl.ds(p, 1), :] == slot_id, key_pos, 0.0)

    acc = lax.fori_loop(0, n_pages + 1, place, jnp.zeros((topk, LANES), F32))
    idx_ref[0] = jnp.sum(acc, axis=1, keepdims=True).astype(jnp.int32)


def _sample_indices(qi, wi, kidx_new, cache_kidx, pages):
    db, n_pages = pages.shape
    past = n_pages * PAGE_SIZE
    topk = min(TOPK_MAX, (past + 1) // 4)
    assert topk % SUBLANES == 0
    group = 16 if n_pages % 16 == 0 else 1
    rows = (n_pages + 1 + SUBLANES - 1) // SUBLANES * SUBLANES
    grid_spec = pltpu.PrefetchScalarGridSpec(
        num_scalar_prefetch=1,
        grid=(db,),
        in_specs=[
            pl.BlockSpec((1, IDX_HEADS, IDX_DIM), lambda b, pg: (b, 0, 0)),
            pl.BlockSpec((1, IDX_HEADS, 1), lambda b, pg: (b, 0, 0)),
            pl.BlockSpec((1, 1, IDX_DIM), lambda b, pg: (b, 0, 0)),
            pl.BlockSpec(memory_space=pl.ANY),
        ],
        out_specs=pl.BlockSpec((1, topk, 1), lambda b, pg: (b, 0, 0)),
        scratch_shapes=[
            pltpu.VMEM((2, n_pages, PAGE_SIZE, IDX_DIM), F32),
            pltpu.SemaphoreType.DMA((2,)),
            pltpu.VMEM((rows, LANES), F32),
            pltpu.VMEM((rows, LANES), F32),
        ],
    )
    idx = pl.pallas_call(
        functools.partial(_sindex_kernel, n_pages=n_pages, topk=topk,
                          wscale=IDX_HEADS ** -0.5 * IDX_DIM ** -0.5, group=group),
        grid_spec=grid_spec,
        out_shape=jax.ShapeDtypeStruct((db, topk, 1), jnp.int32),
        compiler_params=_params("arbitrary"),
        name="sample_indices",
    )(pages, qi.reshape(db, IDX_HEADS, IDX_DIM), wi.reshape(db, IDX_HEADS, 1),
      kidx_new.reshape(db, 1, IDX_DIM), cache_kidx.reshape(-1, PAGE_SIZE, IDX_DIM))
    return idx.reshape(db, topk)


def _sgather_kernel(idx_ref, pg_ref, q_ref, knew_ref, vnew_ref, ck_ref, cv_ref, o_ref, kbuf, vbuf, sem,
                    *, past, topk):
    b = pl.program_id(0)
    nb = pl.num_programs(0)
    slot = lax.rem(b, 2)

    def start_all(bb, sl):
        def body(j, c):
            i = idx_ref[bb, j]
            ip = jnp.minimum(i, past - 1)
            page = pg_ref[bb, ip // PAGE_SIZE]
            off = lax.rem(ip, PAGE_SIZE)

            @pl.when(i >= past)
            def _():
                pltpu.make_async_copy(knew_ref.at[bb], kbuf.at[sl, j], sem.at[0, sl]).start()
                pltpu.make_async_copy(vnew_ref.at[bb], vbuf.at[sl, j], sem.at[1, sl]).start()

            @pl.when(i < past)
            def _():
                pltpu.make_async_copy(ck_ref.at[page, off], kbuf.at[sl, j], sem.at[0, sl]).start()
                pltpu.make_async_copy(cv_ref.at[page, off], vbuf.at[sl, j], sem.at[1, sl]).start()

            return c

        lax.fori_loop(0, topk, body, 0)

    @pl.when(b == 0)
    def _():
        start_all(b, slot)

    @pl.when(b + 1 < nb)
    def _():
        start_all(b + 1, 1 - slot)

    def wait_body(j, c):
        pltpu.make_async_copy(knew_ref.at[b], kbuf.at[slot, j], sem.at[0, slot]).wait()
        pltpu.make_async_copy(vnew_ref.at[b], vbuf.at[slot, j], sem.at[1, slot]).wait()
        return c

    lax.fori_loop(0, topk, wait_body, 0)

    k = kbuf[slot]
    v = vbuf[slot]
    s = jnp.sum(k * q_ref[...], axis=2, keepdims=True) * HEAD_DIM ** -0.5
    e = jnp.exp(s - jnp.max(s, axis=0, keepdims=True))
    p = e / jnp.sum(e, axis=0, keepdims=True)
    o_ref[...] = jnp.sum(p * v, axis=0, keepdims=True).astype(o_ref.dtype)


def _sample_attend(q, k_new, v_new, idx, cache_k, cache_v, pages):
    db, n_pages = pages.shape
    topk = idx.shape[1]
    tile = (N_HEADS, HEAD_DIM)
    grid_spec = pltpu.PrefetchScalarGridSpec(
        num_scalar_prefetch=2,
        grid=(db,),
        in_specs=[
            pl.BlockSpec((1,) + tile, lambda b, ix, pg: (b, 0, 0)),
            pl.BlockSpec(memory_space=pl.ANY),
            pl.BlockSpec(memory_space=pl.ANY),
            pl.BlockSpec(memory_space=pl.ANY),
            pl.BlockSpec(memory_space=pl.ANY),
        ],
        out_specs=pl.BlockSpec((1,) + tile, lambda b, ix, pg: (b, 0, 0)),
        scratch_shapes=[
            pltpu.VMEM((2, topk) + tile, F32),
            pltpu.VMEM((2, topk) + tile, F32),
            pltpu.SemaphoreType.DMA((2, 2)),
        ],
    )
    out = pl.pallas_call(
        functools.partial(_sgather_kernel, past=n_pages * PAGE_SIZE, topk=topk),
        grid_spec=grid_spec,
        out_shape=jax.ShapeDtypeStruct((db,) + tile, BF16),
        compiler_params=_params("arbitrary"),
        name="sample_attend",
    )(idx, pages, q.reshape((db,) + tile), k_new.reshape((db,) + tile), v_new.reshape((db,) + tile),
      cache_k.reshape((-1, PAGE_SIZE) + tile), cache_v.reshape((-1, PAGE_SIZE) + tile))
    return out.reshape(db, N_HEADS * HEAD_DIM)


def _merge_kernel(x_ref, g_ref, ya_ref, yb_ref, ym_ref, woa_ref, wob_ref, wom_ref, wout_ref,
                  gpm_ref, gpf_ref, x1_ref, h2_ref):
    d = x_ref.shape[1]
    merged = (g_ref[:, 0:d].astype(F32) * jnp.dot(ya_ref[...], woa_ref[...], preferred_element_type=F32)
              + g_ref[:, d:2 * d].astype(F32) * jnp.dot(yb_ref[...], wob_ref[...], preferred_element_type=F32)
              + g_ref[:, 2 * d:3 * d].astype(F32) * jnp.dot(ym_ref[...], wom_ref[...], preferred_element_type=F32))
    o = jnp.dot(merged.astype(BF16), wout_ref[...], preferred_element_type=F32)
    x1 = x_ref[...] + _rmsnorm_rows(o, gpm_ref[...])
    x1_ref[...] = x1
    h2_ref[...] = _rmsnorm_rows(x1, gpf_ref[...]).astype(h2_ref.dtype)


def _merge(x, gates, y_a, y_b, y_m, w_oa, w_ob, w_om, w_out, g_post_mix, g_pre_ffn):
    m, d = x.shape
    tm = _tile(m, 256)
    once = pl.Buffered(1)

    def rows(width):
        return pl.BlockSpec((tm, width), lambda i: (i, 0))

    def whole(a):
        return pl.BlockSpec(a.shape, lambda i: (0, 0), pipeline_mode=once)

    return pl.pallas_call(
        _merge_kernel,
        grid=(m // tm,),
        in_specs=[rows(d), rows(N_BRANCH * d), rows(y_a.shape[1]), rows(y_b.shape[1]), rows(y_m.shape[1]),
                  whole(w_oa), whole(w_ob), whole(w_om), whole(w_out),
                  pl.BlockSpec((1, d), lambda i: (0, 0)), pl.BlockSpec((1, d), lambda i: (0, 0))],
        out_specs=[rows(d), rows(d)],
        out_shape=[jax.ShapeDtypeStruct((m, d), F32), jax.ShapeDtypeStruct((m, d), BF16)],
        compiler_params=_params("parallel"),
        name="merge",
    )(x, gates, y_a, y_b, y_m, w_oa, w_ob, w_om, w_out, g_post_mix.reshape(1, d), g_pre_ffn.reshape(1, d))


def _ffn_up_kernel(h_ref, wg_ref, wu_ref, o_ref, *, rows):
    wg = wg_ref[...].astype(BF16)
    wu = wu_ref[...].astype(BF16)

    def body(r, carry):
        r0 = pl.multiple_of(r * rows, rows)
        h = h_ref[pl.ds(r0, rows), :]
        a = jnp.dot(h, wg, preferred_element_type=F32)
        c = jnp.dot(h, wu, preferred_element_type=F32)
        o_ref[pl.ds(r0, rows), :] = (jax.nn.silu(a) * c).astype(o_ref.dtype)
        return carry

    lax.fori_loop(0, h_ref.shape[0] // rows, body, 0)


def _ffn_up(h2, w_gu):
    m, d = h2.shape
    d_ff = w_gu.shape[1] // 2
    tm = _tile(m, FFN_UP_TM)
    tf = _tile(d_ff, 2 * LANES)
    nf = d_ff // tf
    return pl.pallas_call(
        functools.partial(_ffn_up_kernel, rows=_tile(tm, 2048)),
        grid=(m // tm, nf),
        in_specs=[
            pl.BlockSpec((tm, d), lambda i, j: (i, 0), pipeline_mode=pl.Buffered(1)),
            pl.BlockSpec((d, tf), lambda i, j: (0, j)),
            pl.BlockSpec((d, tf), lambda i, j: (0, j + nf)),
        ],
        out_specs=pl.BlockSpec((tm, tf), lambda i, j: (i, j)),
        out_shape=jax.ShapeDtypeStruct((m, d_ff), BF16),
        compiler_params=_params("parallel", "parallel"),
        name="ffn_up",
    )(h2, w_gu, w_gu)


def _ffn_down_kernel(a_ref, wd_ref, x1_ref, gpf_ref, o_ref):
    y = jnp.dot(a_ref[...], wd_ref[...], preferred_element_type=F32)
    o_ref[...] = x1_ref[...] + _rmsnorm_rows(y, gpf_ref[...])


def _ffn_down(act, x1, w_down, g_post_ffn):
    m, d_ff = act.shape
    d = w_down.shape[1]
    tm = _tile(m, 256)
    return pl.pallas_call(
        _ffn_down_kernel,
        grid=(m // tm,),
        in_specs=[
            pl.BlockSpec((tm, d_ff), lambda i: (i, 0)),
            pl.BlockSpec((d_ff, d), lambda i: (0, 0), pipeline_mode=pl.Buffered(1)),
            pl.BlockSpec((tm, d), lambda i: (i, 0)),
            pl.BlockSpec((1, d), lambda i: (0, 0)),
        ],
        out_specs=pl.BlockSpec((tm, d), lambda i: (i, 0)),
        out_shape=jax.ShapeDtypeStruct((m, d), F32),
        compiler_params=_params("parallel"),
        name="ffn_down",
    )(act, w_down, x1, g_post_ffn.reshape(1, d))


def _project_common(x2d, g_pre_mix, w_t, ln_g, ln_b, uv_dtype, q_dtype, q_scale, v_heads_t):
    d = x2d.shape[1]
    aw, iw = N_HEADS * HEAD_DIM, IDX_HEADS * IDX_DIM
    h = _rmsnorm(x2d, g_pre_mix, BF16)
    uv = _project_uv(h, w_t, ln_g, ln_b, uv_dtype)
    (q,) = _matmul(h, w_t, (q_dtype,), scale=q_scale, rows=(d, aw), name="proj_q")
    k, k_bf = _matmul(h, w_t, (F32, BF16), rows=(d + aw, aw), name="proj_k")
    v_out = _matmul(h, w_t, (F32,), rows=(d + 2 * aw, aw), heads_t=v_heads_t, name="proj_v")
    v, v_t = v_out if v_heads_t is not None else (v_out[0], None)
    (qi,) = _matmul(h, w_t, (BF16,), rows=(d + 3 * aw, iw), name="proj_qi")
    (kw,) = _matmul(h, w_t, (F32,), rows=(d + 3 * aw + iw, 2 * LANES), name="proj_kidx")
    o_qm = d + 3 * aw + iw + IDX_DIM + IDX_HEADS
    o_g = o_qm + MEM_HEADS * MEM_HEAD_DIM
    (qm,) = _matmul(h, w_t, (BF16,), rows=(o_qm, o_g - o_qm), name="proj_qm")
    (gates,) = _matmul(h, w_t, (BF16,), act="sigmoid", rows=(o_g, N_BRANCH * d), name="proj_gates")
    kidx = kw[:, :IDX_DIM]
    wi = kw[:, IDX_DIM:IDX_DIM + IDX_HEADS]
    return uv, q, k, k_bf, v, v_t, qi, kidx, wi, qm, gates


def kernel(x_prompt, x_sample, mem_prompt, cache_k, cache_v, cache_kidx, cache_mem_k, cache_mem_v, page_table, g_pre_mix, g_post_mix, g_pre_ffn, g_post_ffn, g_mem, ln_v_g, ln_v_b, w_in, w_s, b_s, w_oa, w_ob, w_om, w_out, w_mem_kv, w_gu, w_down):
    depth = w_in.shape[0]
    b, t, d = x_prompt.shape
    db, dt, _ = x_sample.shape
    assert dt == 1, "sample path handles one new token per sequence"
    n_mem = mem_prompt.shape[1]
    aw = N_HEADS * HEAD_DIM
    mw = MEM_HEADS * MEM_HEAD_DIM
    gw = d // 2
    past = page_table.shape[1] * PAGE_SIZE

    yp = x_prompt.reshape(b * t, d)
    ys = x_sample.reshape(db * dt, d)
    outs = [[] for _ in range(9)]
    for l in range(depth):
        ws = w_in[l].T
        woa, wob, wom, wout = (w.astype(BF16) for w in (w_oa[l], w_ob[l], w_om[l], w_out[l]))
        wdown, wmem = w_down[l].astype(BF16), w_mem_kv[l].T.astype(BF16)

        uv, q, k, k_bf, v, v_t, qi, kidx, wi, qm, gates = _project_common(
            yp, g_pre_mix[l], ws, ln_v_g[l], ln_v_b[l], BF16, BF16, DSA_LOG2_SCALE, (b, BF16))
        y_a = _gmlp_spatial(uv, w_s[l], b_s[l])
        y_b = _dsa_prompt(
            q.reshape(b, t, aw), qi.reshape(b, t, -1), jnp.swapaxes(wi.reshape(b, t, IDX_HEADS), 1, 2),
            k_bf.reshape(b, t, aw), kidx.astype(BF16).reshape(b, t, IDX_DIM), v_t).reshape(b * t, aw)
        hm = _rmsnorm(mem_prompt.reshape(b * n_mem, d), g_mem[l], BF16)
        (mkv,) = _matmul(hm, wmem, (F32,), name="proj_mem_kv")
        mk = mkv[:, :mw].reshape(b, n_mem, mw)
        mv = mkv[:, mw:].reshape(b, n_mem, mw)
        y_m = _mem_attend(qm.reshape(b, t, mw), mk, mv).reshape(b * t, mw)
        x1, h2 = _merge(yp, gates, y_a, y_b, y_m, woa, wob, wom, wout, g_post_mix[l], g_pre_ffn[l])
        yp = _ffn_down(_ffn_up(h2, w_gu[l]), x1, wdown, g_post_ffn[l])
        outs[0].append(k.reshape(b, t, N_HEADS, HEAD_DIM))
        outs[1].append(v.reshape(b, t, N_HEADS, HEAD_DIM))
        outs[2].append(kidx.reshape(b, t, IDX_DIM))
        outs[3].append(mk.reshape(b, n_mem, MEM_HEADS, MEM_HEAD_DIM))
        outs[4].append(mv.reshape(b, n_mem, MEM_HEADS, MEM_HEAD_DIM))

        uv, q, k, _, v, _, qi, kidx, wi, qm, gates = _project_common(
            ys, g_pre_mix[l], ws, ln_v_g[l], ln_v_b[l], F32, F32, None, None)
        y_a = _gmlp_first_row(uv, w_s[l], b_s[l])
        pages = page_table + l * cache_k.shape[1]
        idx = _sample_indices(qi, wi, kidx, cache_kidx, pages)
        y_b = _sample_attend(q, k, v, idx, cache_k, cache_v, pages)
        qm_pad = jnp.pad(qm.reshape(db, 1, mw), ((0, 0), (0, 2 * SUBLANES - 1), (0, 0)))
        y_m = _mem_attend(qm_pad, cache_mem_k[l].reshape(db, n_mem, mw),
                          cache_mem_v[l].reshape(db, n_mem, mw))[:, 0, :]
        x1, h2 = _merge(ys, gates, y_a, y_b, y_m, woa, wob, wom, wout, g_post_mix[l], g_pre_ffn[l])
        ys = _ffn_down(_ffn_up(h2, w_gu[l]), x1, wdown, g_post_ffn[l])
        outs[5].append(k.reshape(db, dt, N_HEADS, HEAD_DIM))
        outs[6].append(v.reshape(db, dt, N_HEADS, HEAD_DIM))
        outs[7].append(kidx.reshape(db, dt, IDX_DIM))
        outs[8].append(uv[:, gw:].reshape(db, dt, gw))

    return (yp.reshape(b, t, d), ys.reshape(db, dt, d)) + tuple(jnp.stack(o) for o in outs)
```

```python
import functools

import jax
import jax.numpy as jnp
from jax import lax
from jax.experimental import pallas as pl
from jax.experimental.pallas import tpu as pltpu

EPS = 1e-6
CHUNK = 128
GM_GROUPS = 8
N_HEADS = 8
HEAD_DIM = 128
IDX_HEADS = 16
IDX_DIM = 128
TOPK_MAX = 256
MEM_HEADS = 4
MEM_HEAD_DIM = 128
N_BRANCH = 3
PAGE_SIZE = 128

LANES = 128
SUBLANES = 8
VMEM_LIMIT_BYTES = 56 * 2**20
INT_MIN = -2**31
KEY_LOWEST = INT_MIN + 0x800000
KEY_INF = 0x7F800000
NEG = -1e30
LOG2_E = 1.4426950408889634
DSA_LOG2_SCALE = HEAD_DIM ** -0.5 * LOG2_E
ONES_ROWS = 16
COUNT_ROWS = 64
BF16_ROWS = 16
COARSE_COUNT_ROWS = 128
MM_TM, MM_TN = 2048, 1024
FFN_UP_TM = 8192

F32 = jnp.float32
BF16 = jnp.bfloat16
NT_DIMS = (((1,), (1,)), ((), ()))


def _params(*sem):
    return pltpu.CompilerParams(dimension_semantics=sem, vmem_limit_bytes=VMEM_LIMIT_BYTES)


def _tile(n, pref):
    if n <= pref:
        return n
    t = pref
    while n % t:
        t //= 2
    return t


def _key_to_float(key):
    key = jnp.clip(key, KEY_LOWEST, KEY_INF)
    return pltpu.bitcast(key ^ ((key >> 31) & 0x7FFFFFFF), F32)


def _key16_to_float(key):
    key = jnp.clip(key, KEY_LOWEST >> 16, KEY_INF >> 16)
    return pltpu.bitcast((key ^ ((key >> 31) & 0x7FFF)) << 16, F32)


def _key16_to_key(key):
    key = jnp.clip(key, -2**15, 2**15 - 1)
    return (key << 16) + ((key >> 31) & 0xFFFF)


def _build_key(count_ge, topk, like, bits):
    base = jnp.where(count_ge(jnp.zeros_like(like)) >= topk, 0, -(1 << (bits - 1))).astype(jnp.int32)

    def body(b, t):
        cand = t | jnp.left_shift(jnp.int32(1), bits - 2 - b)
        return jnp.where(count_ge(cand) >= topk, cand, t)

    return lax.fori_loop(0, bits - 1, body, base)


def _kth_largest_key(count_ge, topk, like, count_ge_bf16=None):
    if count_ge_bf16 is None:
        return _build_key(count_ge, topk, like, 32)
    k16 = _build_key(count_ge_bf16, topk, like, 16)
    lo, hi = _key16_to_key(k16 - 1), _key16_to_key(k16 + 1)

    def bisect(_, lohi):
        lo, hi = lohi
        mid = lo + ((hi - lo) >> 1)
        ok = count_ge(mid) >= topk
        return jnp.where(ok, mid, lo), jnp.where(ok, hi, mid)

    lo, _ = lax.fori_loop(0, 17, bisect, (lo, hi))
    return lo


def _rmsnorm_rows(x, g):
    return x * lax.rsqrt(jnp.mean(x * x, axis=-1, keepdims=True) + EPS) * g


def _rmsnorm_kernel(x_ref, g_ref, o_ref):
    o_ref[...] = _rmsnorm_rows(x_ref[...], g_ref[...]).astype(o_ref.dtype)


def _rmsnorm(x, g, out_dtype):
    m, d = x.shape
    tm = _tile(m, 512)
    return pl.pallas_call(
        _rmsnorm_kernel,
        grid=(m // tm,),
        in_specs=[pl.BlockSpec((tm, d), lambda i: (i, 0)), pl.BlockSpec((1, d), lambda i: (0, 0))],
        out_specs=pl.BlockSpec((tm, d), lambda i: (i, 0)),
        out_shape=jax.ShapeDtypeStruct((m, d), out_dtype),
        compiler_params=_params("parallel"),
        name="rmsnorm",
    )(x, g.reshape(1, d))


def _mm_kernel(x_ref, w_ref, *o_refs, act, scale, heads_t):
    acc = lax.dot_general(x_ref[...], w_ref[...].astype(BF16), NT_DIMS, preferred_element_type=F32)
    if act == "sigmoid":
        acc = jax.nn.sigmoid(acc)
    if scale is not None:
        acc = acc * scale
    if heads_t:
        o_refs, t_ref = o_refs[:-1], o_refs[-1]
        rows = HEAD_DIM + ONES_ROWS
        for h in range(acc.shape[1] // HEAD_DIM):
            t_ref[0, h * rows:h * rows + HEAD_DIM, :] = acc[:, h * HEAD_DIM:(h + 1) * HEAD_DIM].T.astype(t_ref.dtype)
            t_ref[0, h * rows + HEAD_DIM:(h + 1) * rows, :] = jnp.ones((ONES_ROWS, acc.shape[0]), t_ref.dtype)
    for o_ref in o_refs:
        o_ref[...] = acc.astype(o_ref.dtype)


def _matmul(x, w_t, out_dtypes, act=None, scale=None, rows=None, heads_t=None, name="matmul"):
    m, k = x.shape
    row0, n = rows if rows is not None else (0, w_t.shape[0])
    tm = _tile(m if heads_t is None else m // heads_t[0], MM_TM)
    tn = _tile(n, MM_TN if w_t.dtype == BF16 else MM_TN // 2)
    assert row0 % SUBLANES == 0
    out_specs = [pl.BlockSpec((tm, tn), lambda i, j: (i, j)) for _ in out_dtypes]
    out_shapes = [jax.ShapeDtypeStruct((m, n), dt) for dt in out_dtypes]
    if heads_t is not None:
        batch, dt = heads_t
        per_batch = m // batch // tm
        t_rows = tn // HEAD_DIM * (HEAD_DIM + ONES_ROWS)
        out_specs.append(pl.BlockSpec((1, t_rows, tm), lambda i, j: (i // per_batch, j, i % per_batch)))
        out_shapes.append(jax.ShapeDtypeStruct((batch, n // tn * t_rows, m // batch), dt))
    outs = pl.pallas_call(
        functools.partial(_mm_kernel, act=act, scale=scale, heads_t=heads_t is not None),
        grid=(m // tm, n // tn),
        in_specs=[pl.BlockSpec((tm, k), lambda i, j: (i, 0)),
                  pl.BlockSpec((pl.Element(tn), pl.Element(k)),
                               lambda i, j: (pl.multiple_of(row0 + j * tn, SUBLANES), 0))],
        out_specs=out_specs,
        out_shape=out_shapes,
        compiler_params=_params("parallel", "parallel"),
        name=name,
    )(x, w_t)
    return outs


def _uv_kernel(x_ref, w_ref, lng_ref, lnb_ref, o_ref):
    y = jax.nn.gelu(lax.dot_general(x_ref[...], w_ref[...].astype(BF16), NT_DIMS,
                                    preferred_element_type=F32))

    @pl.when(pl.program_id(1) == 0)
    def _():
        o_ref[...] = y.astype(o_ref.dtype)

    @pl.when(pl.program_id(1) == 1)
    def _():
        yc = y - jnp.mean(y, axis=-1, keepdims=True)
        yn = yc * lax.rsqrt(jnp.mean(yc * yc, axis=-1, keepdims=True) + EPS)
        o_ref[...] = (yn * lng_ref[...] + lnb_ref[...]).astype(o_ref.dtype)


def _project_uv(h, w_t, ln_g, ln_b, out_dtype):
    m, k = h.shape
    gw = ln_g.shape[0]
    tm = _tile(m, 1024)
    return pl.pallas_call(
        _uv_kernel,
        grid=(m // tm, 2),
        in_specs=[
            pl.BlockSpec((tm, k), lambda i, j: (i, 0)),
            pl.BlockSpec((gw, k), lambda i, j: (j, 0)),
            pl.BlockSpec((1, gw), lambda i, j: (0, 0)),
            pl.BlockSpec((1, gw), lambda i, j: (0, 0)),
        ],
        out_specs=pl.BlockSpec((tm, gw), lambda i, j: (i, j)),
        out_shape=jax.ShapeDtypeStruct((m, 2 * gw), out_dtype),
        compiler_params=_params("parallel", "arbitrary"),
        name="project_uv",
    )(h, w_t, ln_g.reshape(1, gw), ln_b.reshape(1, gw))


def _gmlp_kernel(u_ref, v_ref, ws_ref, bt_ref, o_ref, *, n_chunks):
    c = CHUNK
    gd = u_ref.shape[1] // GM_GROUPS
    causal = lax.broadcasted_iota(jnp.int32, (c, c), 1) <= lax.broadcasted_iota(jnp.int32, (c, c), 0)
    for g in range(GM_GROUPS):
        w = jnp.where(causal, ws_ref[g], 0.0).astype(v_ref.dtype)
        bcol = bt_ref[:, g:g + 1]
        for ci in range(n_chunks):
            rows = slice(ci * c, (ci + 1) * c)
            cols = slice(g * gd, (g + 1) * gd)
            mixed = jnp.dot(w, v_ref[rows, cols], preferred_element_type=F32) + bcol
            o_ref[rows, cols] = (u_ref[rows, cols].astype(F32) * mixed).astype(o_ref.dtype)


def _gmlp_spatial(uv, w_s, b_s):
    m = uv.shape[0]
    gw = uv.shape[1] // 2
    tm = _tile(m, 4 * CHUNK)
    return pl.pallas_call(
        functools.partial(_gmlp_kernel, n_chunks=tm // CHUNK),
        grid=(m // tm,),
        in_specs=[
            pl.BlockSpec((tm, gw), lambda i: (i, 0)),
            pl.BlockSpec((tm, gw), lambda i: (i, 1)),
            pl.BlockSpec((GM_GROUPS, CHUNK, CHUNK), lambda i: (0, 0, 0)),
            pl.BlockSpec((CHUNK, GM_GROUPS), lambda i: (0, 0)),
        ],
        out_specs=pl.BlockSpec((tm, gw), lambda i: (i, 0)),
        out_shape=jax.ShapeDtypeStruct((m, gw), BF16),
        compiler_params=_params("parallel"),
        name="gmlp_spatial",
    )(uv, uv, w_s, b_s.T)


def _gmlp_first_kernel(u_ref, v_ref, w_ref, b_ref, o_ref):
    o_ref[...] = (u_ref[...] * (w_ref[...] * v_ref[...] + b_ref[...])).astype(o_ref.dtype)


def _gmlp_first_row(uv, w_s, b_s):
    m = uv.shape[0]
    gw = uv.shape[1] // 2
    gd = gw // GM_GROUPS
    w_row = jnp.repeat(w_s[:, 0, 0], gd).reshape(1, gw)
    b_row = jnp.repeat(b_s[:, 0], gd).reshape(1, gw)
    return pl.pallas_call(
        _gmlp_first_kernel,
        grid=(1,),
        in_specs=[
            pl.BlockSpec((m, gw), lambda i: (0, 0)),
            pl.BlockSpec((m, gw), lambda i: (0, 1)),
            pl.BlockSpec((1, gw), lambda i: (0, 0)),
            pl.BlockSpec((1, gw), lambda i: (0, 0)),
        ],
        out_specs=pl.BlockSpec((m, gw), lambda i: (0, 0)),
        out_shape=jax.ShapeDtypeStruct((m, gw), BF16),
        compiler_params=_params("arbitrary"),
        name="gmlp_first_row",
    )(uv, uv, w_row, b_row)


def _dsa_kernel(q_ref, qi_ref, wit_ref, qin_ref, witn_ref, k_ref, kidx_ref, vt_ref, o_ref,
                sc_ref, sc16_ref, m_ref, acc_ref, s_ref, p_ref, *, tq, tk, topk, wscale):
    i = pl.program_id(1)
    last = pl.num_programs(1) - 1
    cur = lax.rem(i, 2)
    n_chunks = ((i + 1) * tq + tk - 1) // tk
    n_next = ((i + 2) * tq + tk - 1) // tk
    vrows = vt_ref.shape[1] // N_HEADS

    def idx_chunk(c, qi_r, wit_r, blk, slot):
        k0 = pl.multiple_of(c * tk, tk)
        kc = kidx_ref[0, pl.ds(k0, tk), :]
        acc = jnp.zeros((tk, tq), F32)
        for h in range(IDX_HEADS):
            s = lax.dot_general(kc, qi_r[0, :, h * IDX_DIM:(h + 1) * IDX_DIM], NT_DIMS,
                                preferred_element_type=F32)
            acc = acc + jnp.maximum(s, 0.0) * (wit_r[0, h:h + 1, :] * wscale)
        kpos = k0 + lax.broadcasted_iota(jnp.int32, (tk, 1), 0)
        qpos = blk * tq + lax.broadcasted_iota(jnp.int32, (1, tq), 1)
        sc = jnp.where(kpos <= qpos, acc, -jnp.inf)
        sc_ref[slot, pl.ds(k0, tk), :] = sc
        sc16_ref[slot, pl.ds(k0, tk), :] = sc.astype(BF16)

    def pad_odd(n, slot):
        @pl.when(n % 2 == 1)
        def _():
            rows = pl.ds(pl.multiple_of(n * tk, tk), tk)
            sc_ref[slot, rows, :] = jnp.full((tk, tq), -jnp.inf, F32)
            sc16_ref[slot, rows, :] = jnp.full((tk, tq), -jnp.inf, BF16)

    @pl.when(i == 0)
    def _():
        def body(c, carry):
            idx_chunk(c, qi_ref, wit_ref, i, cur)
            return carry

        lax.fori_loop(0, n_chunks, body, 0)
        pad_odd(n_chunks, cur)

    def count_ge(cand):
        cand_f = _key_to_float(cand)

        def body(c, cnt):
            k0 = pl.multiple_of(c * (2 * tk), 2 * tk)
            hit = jnp.where(sc_ref[cur, pl.ds(k0, 2 * tk), :] >= cand_f, 1.0, 0.0)
            return cnt + jnp.sum(hit.reshape(2 * tk // COUNT_ROWS, COUNT_ROWS, tq), axis=0)

        cnt = lax.fori_loop(0, (n_chunks + 1) // 2, body, jnp.zeros((COUNT_ROWS, tq), F32))
        return jnp.sum(cnt, axis=0, keepdims=True)

    def count_ge_bf16(cand16):
        cand_c = jnp.broadcast_to(_key16_to_float(cand16), (BF16_ROWS, tq)).astype(BF16)
        one, zero = jnp.ones((), BF16), jnp.zeros((), BF16)
        parts = 2 * tk // COARSE_COUNT_ROWS

        def body(c, cnt):
            k0 = pl.multiple_of(c * (2 * tk), 2 * tk)
            x = sc16_ref[cur, pl.ds(k0, 2 * tk), :].reshape(2 * tk // BF16_ROWS, BF16_ROWS, tq)
            hit = jnp.where(x >= cand_c[None], one, zero).reshape(parts, COARSE_COUNT_ROWS, tq)
            for part in range(parts):
                cnt = cnt + hit[part]
            return cnt

        cnt = lax.fori_loop(0, (n_chunks + 1) // 2, body, jnp.zeros((COARSE_COUNT_ROWS, tq), BF16))
        return jnp.sum(cnt.astype(F32), axis=0, keepdims=True)

    thr = _key_to_float(_kth_largest_key(count_ge, topk, jnp.zeros((1, tq), jnp.int32), count_ge_bf16))

    m_ref[...] = jnp.full(m_ref.shape, NEG, F32)
    acc_ref[...] = jnp.zeros(acc_ref.shape, F32)

    def att_chunk(c):
        k0 = pl.multiple_of(c * tk, tk)
        bias = jnp.where(sc_ref[cur, pl.ds(k0, tk), :] >= thr, 0.0, NEG)
        m_old, m_new = [], []
        for h in range(N_HEADS):
            hs = slice(h * HEAD_DIM, (h + 1) * HEAD_DIM)
            s = lax.dot_general(k_ref[0, pl.ds(k0, tk), hs], q_ref[0, :, hs], NT_DIMS,
                                preferred_element_type=F32) + bias
            s_ref[h] = s
            m_old.append(m_ref[h])
            m_new.append(jnp.maximum(m_old[h], jnp.max(s, axis=0, keepdims=True)))
            m_ref[h] = m_new[h]
        for h in range(N_HEADS):
            p_ref[h] = jnp.exp2(s_ref[h] - m_new[h][0:1, :]).astype(BF16)
        for h in range(N_HEADS):
            pv = jnp.dot(vt_ref[0, h * vrows:(h + 1) * vrows, pl.ds(k0, tk)], p_ref[h],
                         preferred_element_type=F32)
            acc_ref[h] = jnp.exp2(m_old[h] - m_new[h])[0:1, :] * acc_ref[h] + pv

    @pl.when(i < last)
    def _():
        def fused(c, carry):
            att_chunk(c)
            idx_chunk(c, qin_ref, witn_ref, i + 1, 1 - cur)
            return carry

        def tail(c, carry):
            idx_chunk(c, qin_ref, witn_ref, i + 1, 1 - cur)
            return carry

        lax.fori_loop(0, n_chunks, fused, 0)
        lax.fori_loop(n_chunks, n_next, tail, 0)
        pad_odd(n_next, 1 - cur)

    @pl.when(i == last)
    def _():
        def body(c, carry):
            att_chunk(c)
            return carry

        lax.fori_loop(0, n_chunks, body, 0)

    for h in range(N_HEADS):
        o = acc_ref[h, 0:HEAD_DIM, :] / acc_ref[h, HEAD_DIM:HEAD_DIM + 1, :]
        o_ref[0, :, h * HEAD_DIM:(h + 1) * HEAD_DIM] = o.T.astype(o_ref.dtype)


def _dsa_prompt(q, qi, wi_t, k, kidx, v_t, tq=256, tk=256):
    b, t, aw = q.shape
    tq = _tile(t, tq)
    tk = _tile(tq, tk)
    topk = min(TOPK_MAX, t // 4)
    once = pl.Buffered(1)
    nq = t // tq

    def next_block(i):
        return jnp.minimum(i + 1, nq - 1)

    return pl.pallas_call(
        functools.partial(_dsa_kernel, tq=tq, tk=tk, topk=topk,
                          wscale=IDX_HEADS ** -0.5 * IDX_DIM ** -0.5),
        grid=(b, nq),
        in_specs=[
            pl.BlockSpec((1, tq, aw), lambda bi, i: (bi, i, 0)),
            pl.BlockSpec((1, tq, IDX_HEADS * IDX_DIM), lambda bi, i: (bi, i, 0)),
            pl.BlockSpec((1, IDX_HEADS, tq), lambda bi, i: (bi, 0, i)),
            pl.BlockSpec((1, tq, IDX_HEADS * IDX_DIM), lambda bi, i: (bi, next_block(i), 0)),
            pl.BlockSpec((1, IDX_HEADS, tq), lambda bi, i: (bi, 0, next_block(i))),
            pl.BlockSpec((1, t, aw), lambda bi, i: (bi, 0, 0), pipeline_mode=once),
            pl.BlockSpec((1, t, IDX_DIM), lambda bi, i: (bi, 0, 0), pipeline_mode=once),
            pl.BlockSpec((1, v_t.shape[1], t), lambda bi, i: (bi, 0, 0), pipeline_mode=once),
        ],
        out_specs=pl.BlockSpec((1, tq, aw), lambda bi, i: (bi, i, 0)),
        out_shape=jax.ShapeDtypeStruct((b, t, aw), BF16),
        scratch_shapes=[
            pltpu.VMEM((2, t + tk, tq), F32),
            pltpu.VMEM((2, t + tk, tq), BF16),
            pltpu.VMEM((N_HEADS, SUBLANES, tq), F32),
            pltpu.VMEM((N_HEADS, HEAD_DIM + ONES_ROWS, tq), F32),
            pltpu.VMEM((N_HEADS, tk, tq), F32),
            pltpu.VMEM((N_HEADS, tk, tq), BF16),
        ],
        compiler_params=_params("parallel", "arbitrary"),
        name="dsa_prompt",
    )(q, qi, wi_t, qi, wi_t, k, kidx, v_t)


def _memattn_kernel(qm_ref, mk_ref, mv_ref, o_ref):
    scale = MEM_HEAD_DIM ** -0.5
    for h in range(MEM_HEADS):
        hs = slice(h * MEM_HEAD_DIM, (h + 1) * MEM_HEAD_DIM)
        s = lax.dot_general(qm_ref[0, :, hs], mk_ref[0, :, hs].astype(BF16), NT_DIMS,
                            preferred_element_type=F32) * scale
        e = jnp.exp(s - jnp.max(s, axis=-1, keepdims=True))
        p = e / jnp.sum(e, axis=-1, keepdims=True)
        o = jnp.dot(p.astype(BF16), mv_ref[0, :, hs].astype(BF16), preferred_element_type=F32)
        o_ref[0, :, hs] = o.astype(o_ref.dtype)


def _mem_attend(qm, mk, mv):
    b, t, mw = qm.shape
    n_mem = mk.shape[1]
    tm = _tile(t, 512)
    return pl.pallas_call(
        _memattn_kernel,
        grid=(b, t // tm),
        in_specs=[
            pl.BlockSpec((1, tm, mw), lambda bi, i: (bi, i, 0)),
            pl.BlockSpec((1, n_mem, mw), lambda bi, i: (bi, 0, 0)),
            pl.BlockSpec((1, n_mem, mw), lambda bi, i: (bi, 0, 0)),
        ],
        out_specs=pl.BlockSpec((1, tm, mw), lambda bi, i: (bi, i, 0)),
        out_shape=jax.ShapeDtypeStruct((b, t, mw), BF16),
        compiler_params=_params("parallel", "parallel"),
        name="mem_attend",
    )(qm, mk, mv)


def _strict_triangle(n, lower):
    r = lax.broadcasted_iota(jnp.int32, (n, n), 0)
    c = lax.broadcasted_iota(jnp.int32, (n, n), 1)
    return jnp.where((c < r) if lower else (r < c), 1.0, 0.0).astype(BF16)


def _index_order_rank(flag):
    rows = flag.shape[0]
    within = jnp.dot(flag.astype(BF16), _strict_triangle(LANES, False), preferred_element_type=F32)
    per_row = jnp.broadcast_to(jnp.sum(flag, axis=1, keepdims=True), flag.shape)
    before = jnp.dot(_strict_triangle(rows, True), per_row.astype(BF16), preferred_element_type=F32)
    return before + within


def _sindex_kernel(pg_ref, qi_ref, wi_ref, knew_ref, cache_ref, idx_ref, kbuf, sem, sc_ref, pos_ref,
                   *, n_pages, topk, wscale, group):
    b = pl.program_id(0)
    nb = pl.num_programs(0)
    slot = lax.rem(b, 2)

    def page_copy(bb, p, sl):
        return pltpu.make_async_copy(cache_ref.at[pg_ref[bb, p]], kbuf.at[sl, p], sem.at[sl])

    def start_all(bb, sl):
        lax.fori_loop(0, n_pages, lambda p, c: (page_copy(bb, p, sl).start(), c)[1], 0)

    @pl.when(b == 0)
    def _():
        start_all(b, slot)

    @pl.when(b + 1 < nb)
    def _():
        start_all(b + 1, 1 - slot)

    lax.fori_loop(0, n_pages, lambda p, c: (page_copy(b, p, slot).wait(), c)[1], 0)

    qi = qi_ref[0]
    w = wi_ref[0] * wscale
    for g in range(n_pages // group):
        kc = kbuf[slot, g * group:(g + 1) * group].reshape(group * PAGE_SIZE, IDX_DIM).astype(BF16)
        s = lax.dot_general(qi, kc, NT_DIMS, preferred_element_type=F32)
        r = jnp.sum(jnp.maximum(s, 0.0) * w, axis=0, keepdims=True)
        for j in range(group):
            sc_ref[g * group + j:g * group + j + 1, :] = r[:, j * PAGE_SIZE:(j + 1) * PAGE_SIZE]
    s_new = jnp.sum(qi.astype(F32) * knew_ref[0].astype(BF16).astype(F32), axis=1, keepdims=True)
    s_new = jnp.sum(jnp.maximum(s_new, 0.0) * w, axis=0, keepdims=True)
    pad_rows = sc_ref.shape[0] - n_pages
    first = (lax.broadcasted_iota(jnp.int32, (pad_rows, LANES), 0) == 0) & (
        lax.broadcasted_iota(jnp.int32, (pad_rows, LANES), 1) == 0)
    sc_ref[n_pages:, :] = jnp.where(first, s_new, -jnp.inf)

    s = sc_ref[...]

    def count(hit):
        return jnp.sum(jnp.sum(jnp.where(hit, 1.0, 0.0), axis=1, keepdims=True), axis=0, keepdims=True)

    thr = _key_to_float(_kth_largest_key(lambda cand: count(s >= _key_to_float(cand)), topk,
                                         jnp.zeros((1, 1), jnp.int32)))
    above = s > thr
    tied = jnp.where(s == thr, 1.0, 0.0)
    keep_tied = (tied > 0.5) & (_index_order_rank(tied) < topk - count(above))
    chosen = jnp.where(above | keep_tied, 1.0, 0.0)
    pos_ref[...] = jnp.where(chosen > 0.5, _index_order_rank(chosen), -1.0)

    slot_id = lax.broadcasted_iota(jnp.int32, (topk, LANES), 0).astype(F32)
    lane = lax.broadcasted_iota(jnp.int32, (1, LANES), 1)

    def place(p, acc):
        key_pos = (p * PAGE_SIZE + lane).astype(F32)
        return acc + jnp.where(pos_ref[pl.ds(p, 1), :] == slot_id, key_pos, 0.0)

    acc = lax.fori_loop(0, n_pages + 1, place, jnp.zeros((topk, LANES), F32))
    idx_ref[0] = jnp.sum(acc, axis=1, keepdims=True).astype(jnp.int32)


def _sample_indices(qi, wi, kidx_new, cache_kidx, pages):
    db, n_pages = pages.shape
    past = n_pages * PAGE_SIZE
    topk = min(TOPK_MAX, (past + 1) // 4)
    assert topk % SUBLANES == 0
    group = 16 if n_pages % 16 == 0 else 1
    rows = (n_pages + 1 + SUBLANES - 1) // SUBLANES * SUBLANES
    grid_spec = pltpu.PrefetchScalarGridSpec(
        num_scalar_prefetch=1,
        grid=(db,),
        in_specs=[
            pl.BlockSpec((1, IDX_HEADS, IDX_DIM), lambda b, pg: (b, 0, 0)),
            pl.BlockSpec((1, IDX_HEADS, 1), lambda b, pg: (b, 0, 0)),
            pl.BlockSpec((1, 1, IDX_DIM), lambda b, pg: (b, 0, 0)),
            pl.BlockSpec(memory_space=pl.ANY),
        ],
        out_specs=pl.BlockSpec((1, topk, 1), lambda b, pg: (b, 0, 0)),
        scratch_shapes=[
            pltpu.VMEM((2, n_pages, PAGE_SIZE, IDX_DIM), F32),
            pltpu.SemaphoreType.DMA((2,)),
            pltpu.VMEM((rows, LANES), F32),
            pltpu.VMEM((rows, LANES), F32),
        ],
    )
    idx = pl.pallas_call(
        functools.partial(_sindex_kernel, n_pages=n_pages, topk=topk,
                          wscale=IDX_HEADS ** -0.5 * IDX_DIM ** -0.5, group=group),
        grid_spec=grid_spec,
        out_shape=jax.ShapeDtypeStruct((db, topk, 1), jnp.int32),
        compiler_params=_params("arbitrary"),
        name="sample_indices",
    )(pages, qi.reshape(db, IDX_HEADS, IDX_DIM), wi.reshape(db, IDX_HEADS, 1),
      kidx_new.reshape(db, 1, IDX_DIM), cache_kidx.reshape(-1, PAGE_SIZE, IDX_DIM))
    return idx.reshape(db, topk)


def _sgather_kernel(idx_ref, pg_ref, q_ref, knew_ref, vnew_ref, ck_ref, cv_ref, o_ref, kbuf, vbuf, sem,
                    *, past, topk):
    b = pl.program_id(0)
    nb = pl.num_programs(0)
    slot = lax.rem(b, 2)

    def start_all(bb, sl):
        def body(j, c):
            i = idx_ref[bb, j]
            ip = jnp.minimum(i, past - 1)
            page = pg_ref[bb, ip // PAGE_SIZE]
            off = lax.rem(ip, PAGE_SIZE)

            @pl.when(i >= past)
            def _():
                pltpu.make_async_copy(knew_ref.at[bb], kbuf.at[sl, j], sem.at[0, sl]).start()
                pltpu.make_async_copy(vnew_ref.at[bb], vbuf.at[sl, j], sem.at[1, sl]).start()

            @pl.when(i < past)
            def _():
                pltpu.make_async_copy(ck_ref.at[page, off], kbuf.at[sl, j], sem.at[0, sl]).start()
                pltpu.make_async_copy(cv_ref.at[page, off], vbuf.at[sl, j], sem.at[1, sl]).start()

            return c

        lax.fori_loop(0, topk, body, 0)

    @pl.when(b == 0)
    def _():
        start_all(b, slot)

    @pl.when(b + 1 < nb)
    def _():
        start_all(b + 1, 1 - slot)

    def wait_body(j, c):
        pltpu.make_async_copy(knew_ref.at[b], kbuf.at[slot, j], sem.at[0, slot]).wait()
        pltpu.make_async_copy(vnew_ref.at[b], vbuf.at[slot, j], sem.at[1, slot]).wait()
        return c

    lax.fori_loop(0, topk, wait_body, 0)

    k = kbuf[slot]
    v = vbuf[slot]
    s = jnp.sum(k * q_ref[...], axis=2, keepdims=True) * HEAD_DIM ** -0.5
    e = jnp.exp(s - jnp.max(s, axis=0, keepdims=True))
    p = e / jnp.sum(e, axis=0, keepdims=True)
    o_ref[...] = jnp.sum(p * v, axis=0, keepdims=True).astype(o_ref.dtype)


def _sample_attend(q, k_new, v_new, idx, cache_k, cache_v, pages):
    db, n_pages = pages.shape
    topk = idx.shape[1]
    tile = (N_HEADS, HEAD_DIM)
    grid_spec = pltpu.PrefetchScalarGridSpec(
        num_scalar_prefetch=2,
        grid=(db,),
        in_specs=[
            pl.BlockSpec((1,) + tile, lambda b, ix, pg: (b, 0, 0)),
            pl.BlockSpec(memory_space=pl.ANY),
            pl.BlockSpec(memory_space=pl.ANY),
            pl.BlockSpec(memory_space=pl.ANY),
            pl.BlockSpec(memory_space=pl.ANY),
        ],
        out_specs=pl.BlockSpec((1,) + tile, lambda b, ix, pg: (b, 0, 0)),
        scratch_shapes=[
            pltpu.VMEM((2, topk) + tile, F32),
            pltpu.VMEM((2, topk) + tile, F32),
            pltpu.SemaphoreType.DMA((2, 2)),
        ],
    )
    out = pl.pallas_call(
        functools.partial(_sgather_kernel, past=n_pages * PAGE_SIZE, topk=topk),
        grid_spec=grid_spec,
        out_shape=jax.ShapeDtypeStruct((db,) + tile, BF16),
        compiler_params=_params("arbitrary"),
        name="sample_attend",
    )(idx, pages, q.reshape((db,) + tile), k_new.reshape((db,) + tile), v_new.reshape((db,) + tile),
      cache_k.reshape((-1, PAGE_SIZE) + tile), cache_v.reshape((-1, PAGE_SIZE) + tile))
    return out.reshape(db, N_HEADS * HEAD_DIM)


def _merge_kernel(x_ref, g_ref, ya_ref, yb_ref, ym_ref, woa_ref, wob_ref, wom_ref, wout_ref,
                  gpm_ref, gpf_ref, x1_ref, h2_ref):
    d = x_ref.shape[1]
    merged = (g_ref[:, 0:d].astype(F32) * jnp.dot(ya_ref[...], woa_ref[...], preferred_element_type=F32)
              + g_ref[:, d:2 * d].astype(F32) * jnp.dot(yb_ref[...], wob_ref[...], preferred_element_type=F32)
              + g_ref[:, 2 * d:3 * d].astype(F32) * jnp.dot(ym_ref[...], wom_ref[...], preferred_element_type=F32))
    o = jnp.dot(merged.astype(BF16), wout_ref[...], preferred_element_type=F32)
    x1 = x_ref[...] + _rmsnorm_rows(o, gpm_ref[...])
    x1_ref[...] = x1
    h2_ref[...] = _rmsnorm_rows(x1, gpf_ref[...]).astype(h2_ref.dtype)


def _merge(x, gates, y_a, y_b, y_m, w_oa, w_ob, w_om, w_out, g_post_mix, g_pre_ffn):
    m, d = x.shape
    tm = _tile(m, 256)
    once = pl.Buffered(1)

    def rows(width):
        return pl.BlockSpec((tm, width), lambda i: (i, 0))

    def whole(a):
        return pl.BlockSpec(a.shape, lambda i: (0, 0), pipeline_mode=once)

    return pl.pallas_call(
        _merge_kernel,
        grid=(m // tm,),
        in_specs=[rows(d), rows(N_BRANCH * d), rows(y_a.shape[1]), rows(y_b.shape[1]), rows(y_m.shape[1]),
                  whole(w_oa), whole(w_ob), whole(w_om), whole(w_out),
                  pl.BlockSpec((1, d), lambda i: (0, 0)), pl.BlockSpec((1, d), lambda i: (0, 0))],
        out_specs=[rows(d), rows(d)],
        out_shape=[jax.ShapeDtypeStruct((m, d), F32), jax.ShapeDtypeStruct((m, d), BF16)],
        compiler_params=_params("parallel"),
        name="merge",
    )(x, gates, y_a, y_b, y_m, w_oa, w_ob, w_om, w_out, g_post_mix.reshape(1, d), g_pre_ffn.reshape(1, d))


def _ffn_up_kernel(h_ref, wg_ref, wu_ref, o_ref, *, rows):
    wg = wg_ref[...].astype(BF16)
    wu = wu_ref[...].astype(BF16)

    def body(r, carry):
        r0 = pl.multiple_of(r * rows, rows)
        h = h_ref[pl.ds(r0, rows), :]
        a = jnp.dot(h, wg, preferred_element_type=F32)
        c = jnp.dot(h, wu, preferred_element_type=F32)
        o_ref[pl.ds(r0, rows), :] = (jax.nn.silu(a) * c).astype(o_ref.dtype)
        return carry

    lax.fori_loop(0, h_ref.shape[0] // rows, body, 0)


def _ffn_up(h2, w_gu):
    m, d = h2.shape
    d_ff = w_gu.shape[1] // 2
    tm = _tile(m, FFN_UP_TM)
    tf = _tile(d_ff, 2 * LANES)
    nf = d_ff // tf
    return pl.pallas_call(
        functools.partial(_ffn_up_kernel, rows=_tile(tm, 2048)),
        grid=(m // tm, nf),
        in_specs=[
            pl.BlockSpec((tm, d), lambda i, j: (i, 0), pipeline_mode=pl.Buffered(1)),
            pl.BlockSpec((d, tf), lambda i, j: (0, j)),
            pl.BlockSpec((d, tf), lambda i, j: (0, j + nf)),
        ],
        out_specs=pl.BlockSpec((tm, tf), lambda i, j: (i, j)),
        out_shape=jax.ShapeDtypeStruct((m, d_ff), BF16),
        compiler_params=_params("parallel", "parallel"),
        name="ffn_up",
    )(h2, w_gu, w_gu)


def _ffn_down_kernel(a_ref, wd_ref, x1_ref, gpf_ref, o_ref):
    y = jnp.dot(a_ref[...], wd_ref[...], preferred_element_type=F32)
    o_ref[...] = x1_ref[...] + _rmsnorm_rows(y, gpf_ref[...])


def _ffn_down(act, x1, w_down, g_post_ffn):
    m, d_ff = act.shape
    d = w_down.shape[1]
    tm = _tile(m, 512)
    return pl.pallas_call(
        _ffn_down_kernel,
        grid=(m // tm,),
        in_specs=[
            pl.BlockSpec((tm, d_ff), lambda i: (i, 0)),
            pl.BlockSpec((d_ff, d), lambda i: (0, 0), pipeline_mode=pl.Buffered(1)),
            pl.BlockSpec((tm, d), lambda i: (i, 0)),
            pl.BlockSpec((1, d), lambda i: (0, 0)),
        ],
        out_specs=pl.BlockSpec((tm, d), lambda i: (i, 0)),
        out_shape=jax.ShapeDtypeStruct((m, d), F32),
        compiler_params=_params("parallel"),
        name="ffn_down",
    )(act, w_down, x1, g_post_ffn.reshape(1, d))


def _project_common(x2d, g_pre_mix, w_t, ln_g, ln_b, uv_dtype, q_dtype, q_scale, v_heads_t):
    d = x2d.shape[1]
    aw, iw = N_HEADS * HEAD_DIM, IDX_HEADS * IDX_DIM
    h = _rmsnorm(x2d, g_pre_mix, BF16)
    uv = _project_uv(h, w_t, ln_g, ln_b, uv_dtype)
    (q,) = _matmul(h, w_t, (q_dtype,), scale=q_scale, rows=(d, aw), name="proj_q")
    k, k_bf = _matmul(h, w_t, (F32, BF16), rows=(d + aw, aw), name="proj_k")
    v_out = _matmul(h, w_t, (F32,), rows=(d + 2 * aw, aw), heads_t=v_heads_t, name="proj_v")
    v, v_t = v_out if v_heads_t is not None else (v_out[0], None)
    (qi,) = _matmul(h, w_t, (BF16,), rows=(d + 3 * aw, iw), name="proj_qi")
    (kw,) = _matmul(h, w_t, (F32,), rows=(d + 3 * aw + iw, 2 * LANES), name="proj_kidx")
    o_qm = d + 3 * aw + iw + IDX_DIM + IDX_HEADS
    o_g = o_qm + MEM_HEADS * MEM_HEAD_DIM
    (qm,) = _matmul(h, w_t, (BF16,), rows=(o_qm, o_g - o_qm), name="proj_qm")
    (gates,) = _matmul(h, w_t, (BF16,), act="sigmoid", rows=(o_g, N_BRANCH * d), name="proj_gates")
    kidx = kw[:, :IDX_DIM]
    wi = kw[:, IDX_DIM:IDX_DIM + IDX_HEADS]
    return uv, q, k, k_bf, v, v_t, qi, kidx, wi, qm, gates


def kernel(x_prompt, x_sample, mem_prompt, cache_k, cache_v, cache_kidx, cache_mem_k, cache_mem_v, page_table, g_pre_mix, g_post_mix, g_pre_ffn, g_post_ffn, g_mem, ln_v_g, ln_v_b, w_in, w_s, b_s, w_oa, w_ob, w_om, w_out, w_mem_kv, w_gu, w_down):
    depth = w_in.shape[0]
    b, t, d = x_prompt.shape
    db, dt, _ = x_sample.shape
    assert dt == 1, "sample path handles one new token per sequence"
    n_mem = mem_prompt.shape[1]
    aw = N_HEADS * HEAD_DIM
    mw = MEM_HEADS * MEM_HEAD_DIM
    gw = d // 2
    past = page_table.shape[1] * PAGE_SIZE

    yp = x_prompt.reshape(b * t, d)
    ys = x_sample.reshape(db * dt, d)
    outs = [[] for _ in range(9)]
    for l in range(depth):
        ws = w_in[l].T
        woa, wob, wom, wout = (w.astype(BF16) for w in (w_oa[l], w_ob[l], w_om[l], w_out[l]))
        wdown, wmem = w_down[l].astype(BF16), w_mem_kv[l].T.astype(BF16)

        uv, q, k, k_bf, v, v_t, qi, kidx, wi, qm, gates = _project_common(
            yp, g_pre_mix[l], ws, ln_v_g[l], ln_v_b[l], BF16, BF16, DSA_LOG2_SCALE, (b, BF16))
        y_a = _gmlp_spatial(uv, w_s[l], b_s[l])
        y_b = _dsa_prompt(
            q.reshape(b, t, aw), qi.reshape(b, t, -1), jnp.swapaxes(wi.reshape(b, t, IDX_HEADS), 1, 2),
            k_bf.reshape(b, t, aw), kidx.astype(BF16).reshape(b, t, IDX_DIM), v_t).reshape(b * t, aw)
        hm = _rmsnorm(mem_prompt.reshape(b * n_mem, d), g_mem[l], BF16)
        (mkv,) = _matmul(hm, wmem, (F32,), name="proj_mem_kv")
        mk = mkv[:, :mw].reshape(b, n_mem, mw)
        mv = mkv[:, mw:].reshape(b, n_mem, mw)
        y_m = _mem_attend(qm.reshape(b, t, mw), mk, mv).reshape(b * t, mw)
        x1, h2 = _merge(yp, gates, y_a, y_b, y_m, woa, wob, wom, wout, g_post_mix[l], g_pre_ffn[l])
        yp = _ffn_down(_ffn_up(h2, w_gu[l]), x1, wdown, g_post_ffn[l])
        outs[0].append(k.reshape(b, t, N_HEADS, HEAD_DIM))
        outs[1].append(v.reshape(b, t, N_HEADS, HEAD_DIM))
        outs[2].append(kidx.reshape(b, t, IDX_DIM))
        outs[3].append(mk.reshape(b, n_mem, MEM_HEADS, MEM_HEAD_DIM))
        outs[4].append(mv.reshape(b, n_mem, MEM_HEADS, MEM_HEAD_DIM))

        uv, q, k, _, v, _, qi, kidx, wi, qm, gates = _project_common(
            ys, g_pre_mix[l], ws, ln_v_g[l], ln_v_b[l], F32, F32, None, None)
        y_a = _gmlp_first_row(uv, w_s[l], b_s[l])
        pages = page_table + l * cache_k.shape[1]
        idx = _sample_indices(qi, wi, kidx, cache_kidx, pages)
        y_b = _sample_attend(q, k, v, idx, cache_k, cache_v, pages)
        qm_pad = jnp.pad(qm.reshape(db, 1, mw), ((0, 0), (0, 2 * SUBLANES - 1), (0, 0)))
        y_m = _mem_attend(qm_pad, cache_mem_k[l].reshape(db, n_mem, mw),
                          cache_mem_v[l].reshape(db, n_mem, mw))[:, 0, :]
        x1, h2 = _merge(ys, gates, y_a, y_b, y_m, woa, wob, wom, wout, g_post_mix[l], g_pre_ffn[l])
        ys = _ffn_down(_ffn_up(h2, w_gu[l]), x1, wdown, g_post_ffn[l])
        outs[5].append(k.reshape(db, dt, N_HEADS, HEAD_DIM))
        outs[6].append(v.reshape(db, dt, N_HEADS, HEAD_DIM))
        outs[7].append(kidx.reshape(db, dt, IDX_DIM))
        outs[8].append(uv[:, gw:].reshape(db, dt, gw))

    return (yp.reshape(b, t, d), ys.reshape(db, dt, d)) + tuple(jnp.stack(o) for o in outs)
```

```python
import functools

import jax
import jax.numpy as jnp
from jax import lax
from jax.experimental import pallas as pl
from jax.experimental.pallas import tpu as pltpu

EPS = 1e-6
CHUNK = 128
GM_GROUPS = 8
N_HEADS = 8
HEAD_DIM = 128
IDX_HEADS = 16
IDX_DIM = 128
TOPK_MAX = 256
MEM_HEADS = 4
MEM_HEAD_DIM = 128
N_BRANCH = 3
PAGE_SIZE = 128

LANES = 128
SUBLANES = 8
VMEM_LIMIT_BYTES = 56 * 2**20
INT_MIN = -2**31
KEY_LOWEST = INT_MIN + 0x800000
KEY_INF = 0x7F800000
NEG = -1e30
LOG2_E = 1.4426950408889634
DSA_LOG2_SCALE = HEAD_DIM ** -0.5 * LOG2_E
ONES_ROWS = 16
COUNT_ROWS = 64
BF16_ROWS = 16
COARSE_COUNT_ROWS = 128
MM_TM, MM_TN = 2048, 1024
FFN_UP_TM = 8192

F32 = jnp.float32
BF16 = jnp.bfloat16
NT_DIMS = (((1,), (1,)), ((), ()))


def _params(*sem):
    return pltpu.CompilerParams(dimension_semantics=sem, vmem_limit_bytes=VMEM_LIMIT_BYTES)


def _tile(n, pref):
    if n <= pref:
        return n
    t = pref
    while n % t:
        t //= 2
    return t


def _key_to_float(key):
    key = jnp.clip(key, KEY_LOWEST, KEY_INF)
    return pltpu.bitcast(key ^ ((key >> 31) & 0x7FFFFFFF), F32)


def _key16_to_float(key):
    key = jnp.clip(key, KEY_LOWEST >> 16, KEY_INF >> 16)
    return pltpu.bitcast((key ^ ((key >> 31) & 0x7FFF)) << 16, F32)


def _key16_to_key(key):
    key = jnp.clip(key, -2**15, 2**15 - 1)
    return (key << 16) + ((key >> 31) & 0xFFFF)


def _build_key(count_ge, topk, like, bits):
    base = jnp.where(count_ge(jnp.zeros_like(like)) >= topk, 0, -(1 << (bits - 1))).astype(jnp.int32)

    def body(b, t):
        cand = t | jnp.left_shift(jnp.int32(1), bits - 2 - b)
        return jnp.where(count_ge(cand) >= topk, cand, t)

    return lax.fori_loop(0, bits - 1, body, base)


def _kth_largest_key(count_ge, topk, like, count_ge_bf16=None):
    if count_ge_bf16 is None:
        return _build_key(count_ge, topk, like, 32)
    k16 = _build_key(count_ge_bf16, topk, like, 16)
    lo, hi = _key16_to_key(k16 - 1), _key16_to_key(k16 + 1)

    def bisect(_, lohi):
        lo, hi = lohi
        mid = lo + ((hi - lo) >> 1)
        ok = count_ge(mid) >= topk
        return jnp.where(ok, mid, lo), jnp.where(ok, hi, mid)

    lo, _ = lax.fori_loop(0, 17, bisect, (lo, hi))
    return lo


def _rmsnorm_rows(x, g):
    return x * lax.rsqrt(jnp.mean(x * x, axis=-1, keepdims=True) + EPS) * g


def _rmsnorm_kernel(x_ref, g_ref, o_ref):
    o_ref[...] = _rmsnorm_rows(x_ref[...], g_ref[...]).astype(o_ref.dtype)


def _rmsnorm(x, g, out_dtype):
    m, d = x.shape
    tm = _tile(m, 512)
    return pl.pallas_call(
        _rmsnorm_kernel,
        grid=(m // tm,),
        in_specs=[pl.BlockSpec((tm, d), lambda i: (i, 0)), pl.BlockSpec((1, d), lambda i: (0, 0))],
        out_specs=pl.BlockSpec((tm, d), lambda i: (i, 0)),
        out_shape=jax.ShapeDtypeStruct((m, d), out_dtype),
        compiler_params=_params("parallel"),
        name="rmsnorm",
    )(x, g.reshape(1, d))


def _mm_kernel(x_ref, w_ref, *o_refs, act, scale, heads_t):
    acc = lax.dot_general(x_ref[...], w_ref[...].astype(BF16), NT_DIMS, preferred_element_type=F32)
    if act == "sigmoid":
        acc = jax.nn.sigmoid(acc)
    if scale is not None:
        acc = acc * scale
    if heads_t:
        o_refs, t_ref = o_refs[:-1], o_refs[-1]
        rows = HEAD_DIM + ONES_ROWS
        for h in range(acc.shape[1] // HEAD_DIM):
            t_ref[0, h * rows:h * rows + HEAD_DIM, :] = acc[:, h * HEAD_DIM:(h + 1) * HEAD_DIM].T.astype(t_ref.dtype)
            t_ref[0, h * rows + HEAD_DIM:(h + 1) * rows, :] = jnp.ones((ONES_ROWS, acc.shape[0]), t_ref.dtype)
    for o_ref in o_refs:
        o_ref[...] = acc.astype(o_ref.dtype)


def _matmul(x, w_t, out_dtypes, act=None, scale=None, rows=None, heads_t=None, name="matmul"):
    m, k = x.shape
    row0, n = rows if rows is not None else (0, w_t.shape[0])
    tm = _tile(m if heads_t is None else m // heads_t[0], MM_TM)
    narrow = w_t.dtype == BF16 or all(jnp.dtype(dt).itemsize <= 2 for dt in out_dtypes)
    tn = _tile(n, MM_TN if narrow else MM_TN // 2)
    assert row0 % SUBLANES == 0
    out_specs = [pl.BlockSpec((tm, tn), lambda i, j: (i, j)) for _ in out_dtypes]
    out_shapes = [jax.ShapeDtypeStruct((m, n), dt) for dt in out_dtypes]
    if heads_t is not None:
        batch, dt = heads_t
        per_batch = m // batch // tm
        t_rows = tn // HEAD_DIM * (HEAD_DIM + ONES_ROWS)
        out_specs.append(pl.BlockSpec((1, t_rows, tm), lambda i, j: (i // per_batch, j, i % per_batch)))
        out_shapes.append(jax.ShapeDtypeStruct((batch, n // tn * t_rows, m // batch), dt))
    outs = pl.pallas_call(
        functools.partial(_mm_kernel, act=act, scale=scale, heads_t=heads_t is not None),
        grid=(m // tm, n // tn),
        in_specs=[pl.BlockSpec((tm, k), lambda i, j: (i, 0)),
                  pl.BlockSpec((pl.Element(tn), pl.Element(k)),
                               lambda i, j: (pl.multiple_of(row0 + j * tn, SUBLANES), 0))],
        out_specs=out_specs,
        out_shape=out_shapes,
        compiler_params=_params("parallel", "parallel"),
        name=name,
    )(x, w_t)
    return outs


def _uv_kernel(x_ref, w_ref, lng_ref, lnb_ref, o_ref):
    y = jax.nn.gelu(lax.dot_general(x_ref[...], w_ref[...].astype(BF16), NT_DIMS,
                                    preferred_element_type=F32))

    @pl.when(pl.program_id(1) == 0)
    def _():
        o_ref[...] = y.astype(o_ref.dtype)

    @pl.when(pl.program_id(1) == 1)
    def _():
        yc = y - jnp.mean(y, axis=-1, keepdims=True)
        yn = yc * lax.rsqrt(jnp.mean(yc * yc, axis=-1, keepdims=True) + EPS)
        o_ref[...] = (yn * lng_ref[...] + lnb_ref[...]).astype(o_ref.dtype)


def _project_uv(h, w_t, ln_g, ln_b, out_dtype):
    m, k = h.shape
    gw = ln_g.shape[0]
    tm = _tile(m, 1024)
    return pl.pallas_call(
        _uv_kernel,
        grid=(m // tm, 2),
        in_specs=[
            pl.BlockSpec((tm, k), lambda i, j: (i, 0)),
            pl.BlockSpec((gw, k), lambda i, j: (j, 0)),
            pl.BlockSpec((1, gw), lambda i, j: (0, 0)),
            pl.BlockSpec((1, gw), lambda i, j: (0, 0)),
        ],
        out_specs=pl.BlockSpec((tm, gw), lambda i, j: (i, j)),
        out_shape=jax.ShapeDtypeStruct((m, 2 * gw), out_dtype),
        compiler_params=_params("parallel", "arbitrary"),
        name="project_uv",
    )(h, w_t, ln_g.reshape(1, gw), ln_b.reshape(1, gw))


def _gmlp_kernel(u_ref, v_ref, ws_ref, bt_ref, o_ref, *, n_chunks):
    c = CHUNK
    gd = u_ref.shape[1] // GM_GROUPS
    causal = lax.broadcasted_iota(jnp.int32, (c, c), 1) <= lax.broadcasted_iota(jnp.int32, (c, c), 0)
    for g in range(GM_GROUPS):
        w = jnp.where(causal, ws_ref[g], 0.0).astype(v_ref.dtype)
        bcol = bt_ref[:, g:g + 1]
        for ci in range(n_chunks):
            rows = slice(ci * c, (ci + 1) * c)
            cols = slice(g * gd, (g + 1) * gd)
            mixed = jnp.dot(w, v_ref[rows, cols], preferred_element_type=F32) + bcol
            o_ref[rows, cols] = (u_ref[rows, cols].astype(F32) * mixed).astype(o_ref.dtype)


def _gmlp_spatial(uv, w_s, b_s):
    m = uv.shape[0]
    gw = uv.shape[1] // 2
    tm = _tile(m, 4 * CHUNK)
    return pl.pallas_call(
        functools.partial(_gmlp_kernel, n_chunks=tm // CHUNK),
        grid=(m // tm,),
        in_specs=[
            pl.BlockSpec((tm, gw), lambda i: (i, 0)),
            pl.BlockSpec((tm, gw), lambda i: (i, 1)),
            pl.BlockSpec((GM_GROUPS, CHUNK, CHUNK), lambda i: (0, 0, 0)),
            pl.BlockSpec((CHUNK, GM_GROUPS), lambda i: (0, 0)),
        ],
        out_specs=pl.BlockSpec((tm, gw), lambda i: (i, 0)),
        out_shape=jax.ShapeDtypeStruct((m, gw), BF16),
        compiler_params=_params("parallel"),
        name="gmlp_spatial",
    )(uv, uv, w_s, b_s.T)


def _gmlp_first_kernel(u_ref, v_ref, w_ref, b_ref, o_ref):
    o_ref[...] = (u_ref[...] * (w_ref[...] * v_ref[...] + b_ref[...])).astype(o_ref.dtype)


def _gmlp_first_row(uv, w_s, b_s):
    m = uv.shape[0]
    gw = uv.shape[1] // 2
    gd = gw // GM_GROUPS
    w_row = jnp.repeat(w_s[:, 0, 0], gd).reshape(1, gw)
    b_row = jnp.repeat(b_s[:, 0], gd).reshape(1, gw)
    return pl.pallas_call(
        _gmlp_first_kernel,
        grid=(1,),
        in_specs=[
            pl.BlockSpec((m, gw), lambda i: (0, 0)),
            pl.BlockSpec((m, gw), lambda i: (0, 1)),
            pl.BlockSpec((1, gw), lambda i: (0, 0)),
            pl.BlockSpec((1, gw), lambda i: (0, 0)),
        ],
        out_specs=pl.BlockSpec((m, gw), lambda i: (0, 0)),
        out_shape=jax.ShapeDtypeStruct((m, gw), BF16),
        compiler_params=_params("arbitrary"),
        name="gmlp_first_row",
    )(uv, uv, w_row, b_row)


def _dsa_kernel(q_ref, qi_ref, wit_ref, qin_ref, witn_ref, k_ref, kidx_ref, vt_ref, o_ref,
                sc_ref, sc16_ref, m_ref, acc_ref, s_ref, p_ref, *, tq, tk, topk, wscale):
    i = pl.program_id(1)
    last = pl.num_programs(1) - 1
    cur = lax.rem(i, 2)
    n_chunks = ((i + 1) * tq + tk - 1) // tk
    n_next = ((i + 2) * tq + tk - 1) // tk
    vrows = vt_ref.shape[1] // N_HEADS

    def idx_chunk(c, qi_r, wit_r, blk, slot):
        k0 = pl.multiple_of(c * tk, tk)
        kc = kidx_ref[0, pl.ds(k0, tk), :]
        acc = jnp.zeros((tk, tq), F32)
        for h in range(IDX_HEADS):
            s = lax.dot_general(kc, qi_r[0, :, h * IDX_DIM:(h + 1) * IDX_DIM], NT_DIMS,
                                preferred_element_type=F32)
            acc = acc + jnp.maximum(s, 0.0) * (wit_r[0, h:h + 1, :] * wscale)
        kpos = k0 + lax.broadcasted_iota(jnp.int32, (tk, 1), 0)
        qpos = blk * tq + lax.broadcasted_iota(jnp.int32, (1, tq), 1)
        sc = jnp.where(kpos <= qpos, acc, -jnp.inf)
        sc_ref[slot, pl.ds(k0, tk), :] = sc
        sc16_ref[slot, pl.ds(k0, tk), :] = sc.astype(BF16)

    def pad_odd(n, slot):
        @pl.when(n % 2 == 1)
        def _():
            rows = pl.ds(pl.multiple_of(n * tk, tk), tk)
            sc_ref[slot, rows, :] = jnp.full((tk, tq), -jnp.inf, F32)
            sc16_ref[slot, rows, :] = jnp.full((tk, tq), -jnp.inf, BF16)

    @pl.when(i == 0)
    def _():
        def body(c, carry):
            idx_chunk(c, qi_ref, wit_ref, i, cur)
            return carry

        lax.fori_loop(0, n_chunks, body, 0)
        pad_odd(n_chunks, cur)

    def count_ge(cand):
        cand_f = _key_to_float(cand)

        def body(c, cnt):
            k0 = pl.multiple_of(c * (2 * tk), 2 * tk)
            hit = jnp.where(sc_ref[cur, pl.ds(k0, 2 * tk), :] >= cand_f, 1.0, 0.0)
            return cnt + jnp.sum(hit.reshape(2 * tk // COUNT_ROWS, COUNT_ROWS, tq), axis=0)

        cnt = lax.fori_loop(0, (n_chunks + 1) // 2, body, jnp.zeros((COUNT_ROWS, tq), F32))
        return jnp.sum(cnt, axis=0, keepdims=True)

    def count_ge_bf16(cand16):
        cand_c = jnp.broadcast_to(_key16_to_float(cand16), (BF16_ROWS, tq)).astype(BF16)
        one, zero = jnp.ones((), BF16), jnp.zeros((), BF16)
        parts = 2 * tk // COARSE_COUNT_ROWS

        def body(c, cnt):
            k0 = pl.multiple_of(c * (2 * tk), 2 * tk)
            x = sc16_ref[cur, pl.ds(k0, 2 * tk), :].reshape(2 * tk // BF16_ROWS, BF16_ROWS, tq)
            hit = jnp.where(x >= cand_c[None], one, zero).reshape(parts, COARSE_COUNT_ROWS, tq)
            for part in range(parts):
                cnt = cnt + hit[part]
            return cnt

        cnt = lax.fori_loop(0, (n_chunks + 1) // 2, body, jnp.zeros((COARSE_COUNT_ROWS, tq), BF16))
        return jnp.sum(cnt.astype(F32), axis=0, keepdims=True)

    thr = _key_to_float(_kth_largest_key(count_ge, topk, jnp.zeros((1, tq), jnp.int32), count_ge_bf16))

    m_ref[...] = jnp.full(m_ref.shape, NEG, F32)
    acc_ref[...] = jnp.zeros(acc_ref.shape, F32)

    def att_chunk(c):
        k0 = pl.multiple_of(c * tk, tk)
        bias = jnp.where(sc_ref[cur, pl.ds(k0, tk), :] >= thr, 0.0, NEG)
        m_old, m_new = [], []
        for h in range(N_HEADS):
            hs = slice(h * HEAD_DIM, (h + 1) * HEAD_DIM)
            s = lax.dot_general(k_ref[0, pl.ds(k0, tk), hs], q_ref[0, :, hs], NT_DIMS,
                                preferred_element_type=F32) + bias
            s_ref[h] = s
            m_old.append(m_ref[h])
            m_new.append(jnp.maximum(m_old[h], jnp.max(s, axis=0, keepdims=True)))
            m_ref[h] = m_new[h]
        for h in range(N_HEADS):
            p_ref[h] = jnp.exp2(s_ref[h] - m_new[h][0:1, :]).astype(BF16)
        for h in range(N_HEADS):
            pv = jnp.dot(vt_ref[0, h * vrows:(h + 1) * vrows, pl.ds(k0, tk)], p_ref[h],
                         preferred_element_type=F32)
            acc_ref[h] = jnp.exp2(m_old[h] - m_new[h])[0:1, :] * acc_ref[h] + pv

    @pl.when(i < last)
    def _():
        def fused(c, carry):
            att_chunk(c)
            idx_chunk(c, qin_ref, witn_ref, i + 1, 1 - cur)
            return carry

        def tail(c, carry):
            idx_chunk(c, qin_ref, witn_ref, i + 1, 1 - cur)
            return carry

        lax.fori_loop(0, n_chunks, fused, 0)
        lax.fori_loop(n_chunks, n_next, tail, 0)
        pad_odd(n_next, 1 - cur)

    @pl.when(i == last)
    def _():
        def body(c, carry):
            att_chunk(c)
            return carry

        lax.fori_loop(0, n_chunks, body, 0)

    for h in range(N_HEADS):
        o = acc_ref[h, 0:HEAD_DIM, :] / acc_ref[h, HEAD_DIM:HEAD_DIM + 1, :]
        o_ref[0, :, h * HEAD_DIM:(h + 1) * HEAD_DIM] = o.T.astype(o_ref.dtype)


def _dsa_prompt(q, qi, wi_t, k, kidx, v_t, tq=256, tk=256):
    b, t, aw = q.shape
    tq = _tile(t, tq)
    tk = _tile(tq, tk)
    topk = min(TOPK_MAX, t // 4)
    once = pl.Buffered(1)
    nq = t // tq

    def next_block(i):
        return jnp.minimum(i + 1, nq - 1)

    return pl.pallas_call(
        functools.partial(_dsa_kernel, tq=tq, tk=tk, topk=topk,
                          wscale=IDX_HEADS ** -0.5 * IDX_DIM ** -0.5),
        grid=(b, nq),
        in_specs=[
            pl.BlockSpec((1, tq, aw), lambda bi, i: (bi, i, 0)),
            pl.BlockSpec((1, tq, IDX_HEADS * IDX_DIM), lambda bi, i: (bi, i, 0)),
            pl.BlockSpec((1, IDX_HEADS, tq), lambda bi, i: (bi, 0, i)),
            pl.BlockSpec((1, tq, IDX_HEADS * IDX_DIM), lambda bi, i: (bi, next_block(i), 0)),
            pl.BlockSpec((1, IDX_HEADS, tq), lambda bi, i: (bi, 0, next_block(i))),
            pl.BlockSpec((1, t, aw), lambda bi, i: (bi, 0, 0), pipeline_mode=once),
            pl.BlockSpec((1, t, IDX_DIM), lambda bi, i: (bi, 0, 0), pipeline_mode=once),
            pl.BlockSpec((1, v_t.shape[1], t), lambda bi, i: (bi, 0, 0), pipeline_mode=once),
        ],
        out_specs=pl.BlockSpec((1, tq, aw), lambda bi, i: (bi, i, 0)),
        out_shape=jax.ShapeDtypeStruct((b, t, aw), BF16),
        scratch_shapes=[
            pltpu.VMEM((2, t + tk, tq), F32),
            pltpu.VMEM((2, t + tk, tq), BF16),
            pltpu.VMEM((N_HEADS, SUBLANES, tq), F32),
            pltpu.VMEM((N_HEADS, HEAD_DIM + ONES_ROWS, tq), F32),
            pltpu.VMEM((N_HEADS, tk, tq), F32),
            pltpu.VMEM((N_HEADS, tk, tq), BF16),
        ],
        compiler_params=_params("parallel", "arbitrary"),
        name="dsa_prompt",
    )(q, qi, wi_t, qi, wi_t, k, kidx, v_t)


def _memattn_kernel(qm_ref, mk_ref, mv_ref, o_ref):
    scale = MEM_HEAD_DIM ** -0.5
    for h in range(MEM_HEADS):
        hs = slice(h * MEM_HEAD_DIM, (h + 1) * MEM_HEAD_DIM)
        s = lax.dot_general(qm_ref[0, :, hs], mk_ref[0, :, hs].astype(BF16), NT_DIMS,
                            preferred_element_type=F32) * scale
        e = jnp.exp(s - jnp.max(s, axis=-1, keepdims=True))
        p = e / jnp.sum(e, axis=-1, keepdims=True)
        o = jnp.dot(p.astype(BF16), mv_ref[0, :, hs].astype(BF16), preferred_element_type=F32)
        o_ref[0, :, hs] = o.astype(o_ref.dtype)


def _mem_attend(qm, mk, mv):
    b, t, mw = qm.shape
    n_mem = mk.shape[1]
    tm = _tile(t, 512)
    return pl.pallas_call(
        _memattn_kernel,
        grid=(b, t // tm),
        in_specs=[
            pl.BlockSpec((1, tm, mw), lambda bi, i: (bi, i, 0)),
            pl.BlockSpec((1, n_mem, mw), lambda bi, i: (bi, 0, 0)),
            pl.BlockSpec((1, n_mem, mw), lambda bi, i: (bi, 0, 0)),
        ],
        out_specs=pl.BlockSpec((1, tm, mw), lambda bi, i: (bi, i, 0)),
        out_shape=jax.ShapeDtypeStruct((b, t, mw), BF16),
        compiler_params=_params("parallel", "parallel"),
        name="mem_attend",
    )(qm, mk, mv)


def _strict_triangle(n, lower):
    r = lax.broadcasted_iota(jnp.int32, (n, n), 0)
    c = lax.broadcasted_iota(jnp.int32, (n, n), 1)
    return jnp.where((c < r) if lower else (r < c), 1.0, 0.0).astype(BF16)


def _index_order_rank(flag):
    rows = flag.shape[0]
    within = jnp.dot(flag.astype(BF16), _strict_triangle(LANES, False), preferred_element_type=F32)
    per_row = jnp.broadcast_to(jnp.sum(flag, axis=1, keepdims=True), flag.shape)
    before = jnp.dot(_strict_triangle(rows, True), per_row.astype(BF16), preferred_element_type=F32)
    return before + within


def _sindex_kernel(pg_ref, qi_ref, wi_ref, knew_ref, cache_ref, idx_ref, kbuf, sem, sc_ref, pos_ref,
                   *, n_pages, topk, wscale, group):
    b = pl.program_id(0)
    nb = pl.num_programs(0)
    slot = lax.rem(b, 2)

    def page_copy(bb, p, sl):
        return pltpu.make_async_copy(cache_ref.at[pg_ref[bb, p]], kbuf.at[sl, p], sem.at[sl])

    def start_all(bb, sl):
        lax.fori_loop(0, n_pages, lambda p, c: (page_copy(bb, p, sl).start(), c)[1], 0)

    @pl.when(b == 0)
    def _():
        start_all(b, slot)

    @pl.when(b + 1 < nb)
    def _():
        start_all(b + 1, 1 - slot)

    lax.fori_loop(0, n_pages, lambda p, c: (page_copy(b, p, slot).wait(), c)[1], 0)

    qi = qi_ref[0]
    w = wi_ref[0] * wscale
    for g in range(n_pages // group):
        kc = kbuf[slot, g * group:(g + 1) * group].reshape(group * PAGE_SIZE, IDX_DIM).astype(BF16)
        s = lax.dot_general(qi, kc, NT_DIMS, preferred_element_type=F32)
        r = jnp.sum(jnp.maximum(s, 0.0) * w, axis=0, keepdims=True)
        for j in range(group):
            sc_ref[g * group + j:g * group + j + 1, :] = r[:, j * PAGE_SIZE:(j + 1) * PAGE_SIZE]
    s_new = jnp.sum(qi.astype(F32) * knew_ref[0].astype(BF16).astype(F32), axis=1, keepdims=True)
    s_new = jnp.sum(jnp.maximum(s_new, 0.0) * w, axis=0, keepdims=True)
    pad_rows = sc_ref.shape[0] - n_pages
    first = (lax.broadcasted_iota(jnp.int32, (pad_rows, LANES), 0) == 0) & (
        lax.broadcasted_iota(jnp.int32, (pad_rows, LANES), 1) == 0)
    sc_ref[n_pages:, :] = jnp.where(first, s_new, -jnp.inf)

    s = sc_ref[...]

    def count(hit):
        return jnp.sum(jnp.sum(jnp.where(hit, 1.0, 0.0), axis=1, keepdims=True), axis=0, keepdims=True)

    thr = _key_to_float(_kth_largest_key(lambda cand: count(s >= _key_to_float(cand)), topk,
                                         jnp.zeros((1, 1), jnp.int32)))
    above = s > thr
    tied = jnp.where(s == thr, 1.0, 0.0)
    keep_tied = (tied > 0.5) & (_index_order_rank(tied) < topk - count(above))
    chosen = jnp.where(above | keep_tied, 1.0, 0.0)
    pos_ref[...] = jnp.where(chosen > 0.5, _index_order_rank(chosen), -1.0)

    slot_id = lax.broadcasted_iota(jnp.int32, (topk, LANES), 0).astype(F32)
    lane = lax.broadcasted_iota(jnp.int32, (1, LANES), 1)

    def place(p, acc):
        key_pos = (p * PAGE_SIZE + lane).astype(F32)
        return acc + jnp.where(pos_ref[pl.ds(p, 1), :] == slot_id, key_pos, 0.0)

    acc = lax.fori_loop(0, n_pages + 1, place, jnp.zeros((topk, LANES), F32))
    idx_ref[0] = jnp.sum(acc, axis=1, keepdims=True).astype(jnp.int32)


def _sample_indices(qi, wi, kidx_new, cache_kidx, pages):
    db, n_pages = pages.shape
    past = n_pages * PAGE_SIZE
    topk = min(TOPK_MAX, (past + 1) // 4)
    assert topk % SUBLANES == 0
    group = 16 if n_pages % 16 == 0 else 1
    rows = (n_pages + 1 + SUBLANES - 1) // SUBLANES * SUBLANES
    grid_spec = pltpu.PrefetchScalarGridSpec(
        num_scalar_prefetch=1,
        grid=(db,),
        in_specs=[
            pl.BlockSpec((1, IDX_HEADS, IDX_DIM), lambda b, pg: (b, 0, 0)),
            pl.BlockSpec((1, IDX_HEADS, 1), lambda b, pg: (b, 0, 0)),
            pl.BlockSpec((1, 1, IDX_DIM), lambda b, pg: (b, 0, 0)),
            pl.BlockSpec(memory_space=pl.ANY),
        ],
        out_specs=pl.BlockSpec((1, topk, 1), lambda b, pg: (b, 0, 0)),
        scratch_shapes=[
            pltpu.VMEM((2, n_pages, PAGE_SIZE, IDX_DIM), F32),
            pltpu.SemaphoreType.DMA((2,)),
            pltpu.VMEM((rows, LANES), F32),
            pltpu.VMEM((rows, LANES), F32),
        ],
    )
    idx = pl.pallas_call(
        functools.partial(_sindex_kernel, n_pages=n_pages, topk=topk,
                          wscale=IDX_HEADS ** -0.5 * IDX_DIM ** -0.5, group=group),
        grid_spec=grid_spec,
        out_shape=jax.ShapeDtypeStruct((db, topk, 1), jnp.int32),
        compiler_params=_params("arbitrary"),
        name="sample_indices",
    )(pages, qi.reshape(db, IDX_HEADS, IDX_DIM), wi.reshape(db, IDX_HEADS, 1),
      kidx_new.reshape(db, 1, IDX_DIM), cache_kidx.reshape(-1, PAGE_SIZE, IDX_DIM))
    return idx.reshape(db, topk)


def _sgather_kernel(idx_ref, pg_ref, q_ref, knew_ref, vnew_ref, ck_ref, cv_ref, o_ref, kbuf, vbuf, sem,
                    *, past, topk):
    b = pl.program_id(0)
    nb = pl.num_programs(0)
    slot = lax.rem(b, 2)

    def start_all(bb, sl):
        def body(j, c):
            i = idx_ref[bb, j]
            ip = jnp.minimum(i, past - 1)
            page = pg_ref[bb, ip // PAGE_SIZE]
            off = lax.rem(ip, PAGE_SIZE)

            @pl.when(i >= past)
            def _():
                pltpu.make_async_copy(knew_ref.at[bb], kbuf.at[sl, j], sem.at[0, sl]).start()
                pltpu.make_async_copy(vnew_ref.at[bb], vbuf.at[sl, j], sem.at[1, sl]).start()

            @pl.when(i < past)
            def _():
                pltpu.make_async_copy(ck_ref.at[page, off], kbuf.at[sl, j], sem.at[0, sl]).start()
                pltpu.make_async_copy(cv_ref.at[page, off], vbuf.at[sl, j], sem.at[1, sl]).start()

            return c

        lax.fori_loop(0, topk, body, 0)

    @pl.when(b == 0)
    def _():
        start_all(b, slot)

    @pl.when(b + 1 < nb)
    def _():
        start_all(b + 1, 1 - slot)

    def wait_body(j, c):
        pltpu.make_async_copy(knew_ref.at[b], kbuf.at[slot, j], sem.at[0, slot]).wait()
        pltpu.make_async_copy(vnew_ref.at[b], vbuf.at[slot, j], sem.at[1, slot]).wait()
        return c

    lax.fori_loop(0, topk, wait_body, 0)

    k = kbuf[slot]
    v = vbuf[slot]
    s = jnp.sum(k * q_ref[...], axis=2, keepdims=True) * HEAD_DIM ** -0.5
    e = jnp.exp(s - jnp.max(s, axis=0, keepdims=True))
    p = e / jnp.sum(e, axis=0, keepdims=True)
    o_ref[...] = jnp.sum(p * v, axis=0, keepdims=True).astype(o_ref.dtype)


def _sample_attend(q, k_new, v_new, idx, cache_k, cache_v, pages):
    db, n_pages = pages.shape
    topk = idx.shape[1]
    tile = (N_HEADS, HEAD_DIM)
    grid_spec = pltpu.PrefetchScalarGridSpec(
        num_scalar_prefetch=2,
        grid=(db,),
        in_specs=[
            pl.BlockSpec((1,) + tile, lambda b, ix, pg: (b, 0, 0)),
            pl.BlockSpec(memory_space=pl.ANY),
            pl.BlockSpec(memory_space=pl.ANY),
            pl.BlockSpec(memory_space=pl.ANY),
            pl.BlockSpec(memory_space=pl.ANY),
        ],
        out_specs=pl.BlockSpec((1,) + tile, lambda b, ix, pg: (b, 0, 0)),
        scratch_shapes=[
            pltpu.VMEM((2, topk) + tile, F32),
            pltpu.VMEM((2, topk) + tile, F32),
            pltpu.SemaphoreType.DMA((2, 2)),
        ],
    )
    out = pl.pallas_call(
        functools.partial(_sgather_kernel, past=n_pages * PAGE_SIZE, topk=topk),
        grid_spec=grid_spec,
        out_shape=jax.ShapeDtypeStruct((db,) + tile, BF16),
        compiler_params=_params("arbitrary"),
        name="sample_attend",
    )(idx, pages, q.reshape((db,) + tile), k_new.reshape((db,) + tile), v_new.reshape((db,) + tile),
      cache_k.reshape((-1, PAGE_SIZE) + tile), cache_v.reshape((-1, PAGE_SIZE) + tile))
    return out.reshape(db, N_HEADS * HEAD_DIM)


def _merge_kernel(x_ref, g_ref, ya_ref, yb_ref, ym_ref, woa_ref, wob_ref, wom_ref, wout_ref,
                  gpm_ref, gpf_ref, x1_ref, h2_ref):
    d = x_ref.shape[1]
    merged = (g_ref[:, 0:d].astype(F32) * jnp.dot(ya_ref[...], woa_ref[...], preferred_element_type=F32)
              + g_ref[:, d:2 * d].astype(F32) * jnp.dot(yb_ref[...], wob_ref[...], preferred_element_type=F32)
              + g_ref[:, 2 * d:3 * d].astype(F32) * jnp.dot(ym_ref[...], wom_ref[...], preferred_element_type=F32))
    o = jnp.dot(merged.astype(BF16), wout_ref[...], preferred_element_type=F32)
    x1 = x_ref[...] + _rmsnorm_rows(o, gpm_ref[...])
    x1_ref[...] = x1
    h2_ref[...] = _rmsnorm_rows(x1, gpf_ref[...]).astype(h2_ref.dtype)


def _merge(x, gates, y_a, y_b, y_m, w_oa, w_ob, w_om, w_out, g_post_mix, g_pre_ffn):
    m, d = x.shape
    tm = _tile(m, 256)
    once = pl.Buffered(1)

    def rows(width):
        return pl.BlockSpec((tm, width), lambda i: (i, 0))

    def whole(a):
        return pl.BlockSpec(a.shape, lambda i: (0, 0), pipeline_mode=once)

    return pl.pallas_call(
        _merge_kernel,
        grid=(m // tm,),
        in_specs=[rows(d), rows(N_BRANCH * d), rows(y_a.shape[1]), rows(y_b.shape[1]), rows(y_m.shape[1]),
                  whole(w_oa), whole(w_ob), whole(w_om), whole(w_out),
                  pl.BlockSpec((1, d), lambda i: (0, 0)), pl.BlockSpec((1, d), lambda i: (0, 0))],
        out_specs=[rows(d), rows(d)],
        out_shape=[jax.ShapeDtypeStruct((m, d), F32), jax.ShapeDtypeStruct((m, d), BF16)],
        compiler_params=_params("parallel"),
        name="merge",
    )(x, gates, y_a, y_b, y_m, w_oa, w_ob, w_om, w_out, g_post_mix.reshape(1, d), g_pre_ffn.reshape(1, d))


def _ffn_up_kernel(h_ref, wg_ref, wu_ref, o_ref, *, rows):
    wg = wg_ref[...].astype(BF16)
    wu = wu_ref[...].astype(BF16)

    def body(r, carry):
        r0 = pl.multiple_of(r * rows, rows)
        h = h_ref[pl.ds(r0, rows), :]
        a = jnp.dot(h, wg, preferred_element_type=F32)
        c = jnp.dot(h, wu, preferred_element_type=F32)
        o_ref[pl.ds(r0, rows), :] = (jax.nn.silu(a) * c).astype(o_ref.dtype)
        return carry

    lax.fori_loop(0, h_ref.shape[0] // rows, body, 0)


def _ffn_up(h2, w_gu):
    m, d = h2.shape
    d_ff = w_gu.shape[1] // 2
    tm = _tile(m, FFN_UP_TM)
    tf = _tile(d_ff, 2 * LANES)
    nf = d_ff // tf
    return pl.pallas_call(
        functools.partial(_ffn_up_kernel, rows=_tile(tm, 2048)),
        grid=(m // tm, nf),
        in_specs=[
            pl.BlockSpec((tm, d), lambda i, j: (i, 0), pipeline_mode=pl.Buffered(1)),
            pl.BlockSpec((d, tf), lambda i, j: (0, j)),
            pl.BlockSpec((d, tf), lambda i, j: (0, j + nf)),
        ],
        out_specs=pl.BlockSpec((tm, tf), lambda i, j: (i, j)),
        out_shape=jax.ShapeDtypeStruct((m, d_ff), BF16),
        compiler_params=_params("parallel", "parallel"),
        name="ffn_up",
    )(h2, w_gu, w_gu)


def _ffn_down_kernel(a_ref, wd_ref, x1_ref, gpf_ref, o_ref):
    y = jnp.dot(a_ref[...], wd_ref[...], preferred_element_type=F32)
    o_ref[...] = x1_ref[...] + _rmsnorm_rows(y, gpf_ref[...])


def _ffn_down(act, x1, w_down, g_post_ffn):
    m, d_ff = act.shape
    d = w_down.shape[1]
    tm = _tile(m, 256)
    return pl.pallas_call(
        _ffn_down_kernel,
        grid=(m // tm,),
        in_specs=[
            pl.BlockSpec((tm, d_ff), lambda i: (i, 0)),
            pl.BlockSpec((d_ff, d), lambda i: (0, 0), pipeline_mode=pl.Buffered(1)),
            pl.BlockSpec((tm, d), lambda i: (i, 0)),
            pl.BlockSpec((1, d), lambda i: (0, 0)),
        ],
        out_specs=pl.BlockSpec((tm, d), lambda i: (i, 0)),
        out_shape=jax.ShapeDtypeStruct((m, d), F32),
        compiler_params=_params("parallel"),
        name="ffn_down",
    )(act, w_down, x1, g_post_ffn.reshape(1, d))


def _project_common(x2d, g_pre_mix, w_t, ln_g, ln_b, uv_dtype, q_dtype, q_scale, v_heads_t):
    d = x2d.shape[1]
    aw, iw = N_HEADS * HEAD_DIM, IDX_HEADS * IDX_DIM
    h = _rmsnorm(x2d, g_pre_mix, BF16)
    uv = _project_uv(h, w_t, ln_g, ln_b, uv_dtype)
    (q,) = _matmul(h, w_t, (q_dtype,), scale=q_scale, rows=(d, aw), name="proj_q")
    k, k_bf = _matmul(h, w_t, (F32, BF16), rows=(d + aw, aw), name="proj_k")
    v_out = _matmul(h, w_t, (F32,), rows=(d + 2 * aw, aw), heads_t=v_heads_t, name="proj_v")
    v, v_t = v_out if v_heads_t is not None else (v_out[0], None)
    (qi,) = _matmul(h, w_t, (BF16,), rows=(d + 3 * aw, iw), name="proj_qi")
    (kw,) = _matmul(h, w_t, (F32,), rows=(d + 3 * aw + iw, 2 * LANES), name="proj_kidx")
    o_qm = d + 3 * aw + iw + IDX_DIM + IDX_HEADS
    o_g = o_qm + MEM_HEADS * MEM_HEAD_DIM
    (qm,) = _matmul(h, w_t, (BF16,), rows=(o_qm, o_g - o_qm), name="proj_qm")
    (gates,) = _matmul(h, w_t, (BF16,), act="sigmoid", rows=(o_g, N_BRANCH * d), name="proj_gates")
    kidx = kw[:, :IDX_DIM]
    wi = kw[:, IDX_DIM:IDX_DIM + IDX_HEADS]
    return uv, q, k, k_bf, v, v_t, qi, kidx, wi, qm, gates


def kernel(x_prompt, x_sample, mem_prompt, cache_k, cache_v, cache_kidx, cache_mem_k, cache_mem_v, page_table, g_pre_mix, g_post_mix, g_pre_ffn, g_post_ffn, g_mem, ln_v_g, ln_v_b, w_in, w_s, b_s, w_oa, w_ob, w_om, w_out, w_mem_kv, w_gu, w_down):
    depth = w_in.shape[0]
    b, t, d = x_prompt.shape
    db, dt, _ = x_sample.shape
    assert dt == 1, "sample path handles one new token per sequence"
    n_mem = mem_prompt.shape[1]
    aw = N_HEADS * HEAD_DIM
    mw = MEM_HEADS * MEM_HEAD_DIM
    gw = d // 2
    past = page_table.shape[1] * PAGE_SIZE

    yp = x_prompt.reshape(b * t, d)
    ys = x_sample.reshape(db * dt, d)
    outs = [[] for _ in range(9)]
    for l in range(depth):
        ws = w_in[l].T
        woa, wob, wom, wout = (w.astype(BF16) for w in (w_oa[l], w_ob[l], w_om[l], w_out[l]))
        wdown, wmem = w_down[l].astype(BF16), w_mem_kv[l].T.astype(BF16)

        uv, q, k, k_bf, v, v_t, qi, kidx, wi, qm, gates = _project_common(
            yp, g_pre_mix[l], ws, ln_v_g[l], ln_v_b[l], BF16, BF16, DSA_LOG2_SCALE, (b, BF16))
        y_a = _gmlp_spatial(uv, w_s[l], b_s[l])
        y_b = _dsa_prompt(
            q.reshape(b, t, aw), qi.reshape(b, t, -1), jnp.swapaxes(wi.reshape(b, t, IDX_HEADS), 1, 2),
            k_bf.reshape(b, t, aw), kidx.astype(BF16).reshape(b, t, IDX_DIM), v_t).reshape(b * t, aw)
        hm = _rmsnorm(mem_prompt.reshape(b * n_mem, d), g_mem[l], BF16)
        (mkv,) = _matmul(hm, wmem, (F32,), name="proj_mem_kv")
        mk = mkv[:, :mw].reshape(b, n_mem, mw)
        mv = mkv[:, mw:].reshape(b, n_mem, mw)
        y_m = _mem_attend(qm.reshape(b, t, mw), mk, mv).reshape(b * t, mw)
        x1, h2 = _merge(yp, gates, y_a, y_b, y_m, woa, wob, wom, wout, g_post_mix[l], g_pre_ffn[l])
        yp = _ffn_down(_ffn_up(h2, w_gu[l]), x1, wdown, g_post_ffn[l])
        outs[0].append(k.reshape(b, t, N_HEADS, HEAD_DIM))
        outs[1].append(v.reshape(b, t, N_HEADS, HEAD_DIM))
        outs[2].append(kidx.reshape(b, t, IDX_DIM))
        outs[3].append(mk.reshape(b, n_mem, MEM_HEADS, MEM_HEAD_DIM))
        outs[4].append(mv.reshape(b, n_mem, MEM_HEADS, MEM_HEAD_DIM))

        uv, q, k, _, v, _, qi, kidx, wi, qm, gates = _project_common(
            ys, g_pre_mix[l], ws, ln_v_g[l], ln_v_b[l], F32, F32, None, None)
        y_a = _gmlp_first_row(uv, w_s[l], b_s[l])
        pages = page_table + l * cache_k.shape[1]
        idx = _sample_indices(qi, wi, kidx, cache_kidx, pages)
        y_b = _sample_attend(q, k, v, idx, cache_k, cache_v, pages)
        qm_pad = jnp.pad(qm.reshape(db, 1, mw), ((0, 0), (0, 2 * SUBLANES - 1), (0, 0)))
        y_m = _mem_attend(qm_pad, cache_mem_k[l].reshape(db, n_mem, mw),
                          cache_mem_v[l].reshape(db, n_mem, mw))[:, 0, :]
        x1, h2 = _merge(ys, gates, y_a, y_b, y_m, woa, wob, wom, wout, g_post_mix[l], g_pre_ffn[l])
        ys = _ffn_down(_ffn_up(h2, w_gu[l]), x1, wdown, g_post_ffn[l])
        outs[5].append(k.reshape(db, dt, N_HEADS, HEAD_DIM))
        outs[6].append(v.reshape(db, dt, N_HEADS, HEAD_DIM))
        outs[7].append(kidx.reshape(db, dt, IDX_DIM))
        outs[8].append(uv[:, gw:].reshape(db, dt, gw))

    return (yp.reshape(b, t, d), ys.reshape(db, dt, d)) + tuple(jnp.stack(o) for o in outs)
```
